```python
import math
import jax, jax.numpy as jnp
from jax import lax
import numpy as np

D_MODEL = 1024
BATCH = 4
SEQ = 8192
DEPTH = 4

HEAD_DIM = 64
N_RET_HEADS = 8
N_MOBA_HEADS = 8
RET_WIDTH = N_RET_HEADS * HEAD_DIM
MOBA_WIDTH = N_MOBA_HEADS * HEAD_DIM
RET_CHUNK = 128
MOBA_BLOCK = 256
MOBA_TOPK = 3
MOBA_Q_BLOCK = 64
GMLP_CHUNK = 128
GMLP_GROUPS = 8
GMLP_WIDTH = 2 * D_MODEL
FFN_HIDDEN = 4 * D_MODEL
ROPE_THETA = 10000.0
LN_EPS = 1e-5
N_EVEN = (DEPTH + 1) // 2
N_ODD = DEPTH // 2
DEEPNORM_ALPHA = (2 * DEPTH) ** 0.25
DEEPNORM_BETA = (8 * DEPTH) ** -0.25
IN_COLS = 4 * RET_WIDTH + 3 * MOBA_WIDTH
IN_SPLITS = [RET_WIDTH * i for i in range(1, 5)] + [4 * RET_WIDTH + MOBA_WIDTH * i for i in range(1, 3)]

kernel_name = "retnet_moba_gmlp_deepnorm_hybrid"


def layer_norm(x, g, b):
    xf = x.astype(jnp.float32)
    mu = jnp.mean(xf, -1, keepdims=True)
    var = jnp.mean(jnp.square(xf - mu), -1, keepdims=True)
    return ((xf - mu) * lax.rsqrt(var + LN_EPS) * g.astype(jnp.float32) + b.astype(jnp.float32)).astype(x.dtype)


def rope_tables(positions):
    inv_freq = ROPE_THETA ** (-jnp.arange(0, HEAD_DIM, 2, dtype=jnp.float32) / HEAD_DIM)
    ang = positions.astype(jnp.float32)[..., None] * inv_freq
    return jnp.cos(ang)[:, None], jnp.sin(ang)[:, None]


def apply_rope(t, cos, sin):
    half = HEAD_DIM // 2
    t1, t2 = t[..., :half], t[..., half:]
    return jnp.concatenate([t1 * cos - t2 * sin, t2 * cos + t1 * sin], -1).astype(t.dtype)


def split_heads(t, n_heads):
    b, s, _ = t.shape
    return t.reshape(b, s, n_heads, HEAD_DIM).transpose(0, 2, 1, 3)


def merge_heads(t):
    b, h, s, d = t.shape
    return t.transpose(0, 2, 1, 3).reshape(b, s, h * d)


def retention(q, k, v):
    b, h, s, d = q.shape
    n = s // RET_CHUNK
    dt = q.dtype
    log_gamma = jnp.log1p(-jnp.exp2(-5.0 - jnp.arange(h, dtype=jnp.float32)))
    idx = jnp.arange(RET_CHUNK, dtype=jnp.float32)
    diff = idx[:, None] - idx[None, :]
    decay = jnp.where(diff >= 0, jnp.exp(log_gamma[:, None, None] * jnp.maximum(diff, 0.0)), 0.0)
    xi = jnp.exp(log_gamma[:, None] * (idx + 1.0))
    zeta = jnp.exp(log_gamma[:, None] * (RET_CHUNK - 1.0 - idx))
    chunk_decay = jnp.exp(log_gamma * RET_CHUNK)
    qc = q.reshape(b, h, n, RET_CHUNK, d)
    kc = k.reshape(b, h, n, RET_CHUNK, d) * (d ** -0.5)
    vc = v.reshape(b, h, n, RET_CHUNK, d)
    scores = jnp.einsum('bhnid,bhnjd->bhnij', qc, kc) * decay[:, None].astype(dt)
    inner = jnp.einsum('bhnij,bhnjd->bhnid', scores, vc)
    kv = jnp.einsum('bhnjd,bhnje->nbhde', kc * zeta[:, None, :, None].astype(dt), vc).astype(jnp.float32)

    def step(state, kv_n):
        return state * chunk_decay[:, None, None] + kv_n, state

    _, states = lax.scan(step, jnp.zeros((b, h, d, d), jnp.float32), kv)
    cross = jnp.einsum('bhnid,nbhde->bhnie', qc, states.astype(dt)) * xi[:, None, :, None].astype(dt)
    return (inner + cross).reshape(b, h, s, d)


def moba_attention(q, k, v):
    b, h, s, d = q.shape
    nb = s // MOBA_BLOCK
    nq = s // MOBA_Q_BLOCK
    topk = min(MOBA_TOPK, nb)
    scale = d ** -0.5
    k_blocks = k.reshape(b, h, nb, MOBA_BLOCK, d)
    v_blocks = v.reshape(b, h, nb, MOBA_BLOCK, d)
    k_mean = jnp.mean(k_blocks.astype(jnp.float32), axis=3)
    q_chunks = jnp.moveaxis(q.reshape(b, h, nq, MOBA_Q_BLOCK, d), 2, 0)
    bi = jnp.arange(b)[:, None, None, None]
    hi = jnp.arange(h)[None, :, None, None]
    blk_ids = jnp.arange(nb)
    q_off = jnp.arange(MOBA_Q_BLOCK)
    k_off = jnp.arange(MOBA_BLOCK)
    n_sel = topk * MOBA_BLOCK

    def one_chunk(args):
        qc, ci = args
        q_start = ci * MOBA_Q_BLOCK
        own = q_start // MOBA_BLOCK
        gate = jnp.einsum('bhqd,bhnd->bhqn', qc.astype(jnp.float32), k_mean)
        gate = jnp.where(blk_ids < own, gate, -jnp.inf)
        _, sel = lax.top_k(gate, topk)
        sel_valid = sel < own
        k_sel = k_blocks[bi, hi, sel]
        v_sel = v_blocks[bi, hi, sel]
        s_sel = jnp.einsum('bhqd,bhqkjd->bhqkj', qc, k_sel).astype(jnp.float32) * scale
        s_sel = jnp.where(sel_valid[..., None], s_sel, -jnp.inf).reshape(b, h, MOBA_Q_BLOCK, n_sel)
        k_own = lax.dynamic_index_in_dim(k_blocks, own, axis=2, keepdims=False)
        v_own = lax.dynamic_index_in_dim(v_blocks, own, axis=2, keepdims=False)
        s_own = jnp.einsum('bhqd,bhjd->bhqj', qc, k_own).astype(jnp.float32) * scale
        causal = (own * MOBA_BLOCK + k_off)[None, :] <= (q_start + q_off)[:, None]
        s_own = jnp.where(causal, s_own, -jnp.inf)
        p = jax.nn.softmax(jnp.concatenate([s_sel, s_own], -1), axis=-1).astype(qc.dtype)
        p_sel = p[..., :n_sel].reshape(b, h, MOBA_Q_BLOCK, topk, MOBA_BLOCK)
        return (jnp.einsum('bhqkj,bhqkjd->bhqd', p_sel, v_sel)
                + jnp.einsum('bhqj,bhjd->bhqd', p[..., n_sel:], v_own))

    out = lax.map(one_chunk, (q_chunks, jnp.arange(nq)))
    return jnp.moveaxis(out, 0, 2).reshape(b, h, s, d)


def retention_moba_mixer(x, cos, sin, w_in, gn_gain, w_out):
    proj = x @ w_in
    qa, ka, va, ga, qb, kb, vb = jnp.split(proj, IN_SPLITS, axis=-1)
    ya = retention(apply_rope(split_heads(qa, N_RET_HEADS), cos, sin),
                   apply_rope(split_heads(ka, N_RET_HEADS), cos, sin),
                   split_heads(va, N_RET_HEADS))
    yf = ya.astype(jnp.float32)
    mu = jnp.mean(yf, -1, keepdims=True)
    var = jnp.mean(jnp.square(yf - mu), -1, keepdims=True)
    ya = merge_heads(((yf - mu) * lax.rsqrt(var + LN_EPS)).astype(x.dtype)) * gn_gain
    ya = ya * jax.nn.silu(ga)
    yb = merge_heads(moba_attention(apply_rope(split_heads(qb, N_MOBA_HEADS), cos, sin),
                                    apply_rope(split_heads(kb, N_MOBA_HEADS), cos, sin),
                                    split_heads(vb, N_MOBA_HEADS)))
    return jnp.concatenate([ya, yb], axis=-1) @ w_out


def gmlp_mixer(x, w_uv, b_uv, ln_g, ln_b, w_s, b_s, w_out):
    b, s, _ = x.shape
    hdn = jax.nn.gelu(x @ w_uv + b_uv)
    u, v = jnp.split(hdn, 2, axis=-1)
    v = layer_norm(v, ln_g, ln_b)
    n = s // GMLP_CHUNK
    v = v.reshape(b, n, GMLP_CHUNK, GMLP_GROUPS, GMLP_WIDTH // GMLP_GROUPS)
    causal = jnp.tril(jnp.ones((GMLP_CHUNK, GMLP_CHUNK), dtype=bool))
    ws = jnp.where(causal, w_s, jnp.zeros_like(w_s))
    v = jnp.einsum('gij,bnjgc->bnigc', ws, v) + b_s.T[:, :, None]
    return (u * v.reshape(b, s, GMLP_WIDTH)) @ w_out


def channel_mixer(x, w1, w2):
    return jnp.square(jax.nn.relu(x @ w1)) @ w2


def setup_inputs(seed: int = 0) -> dict:
    key = jax.random.key(seed)
    ks = jax.random.split(key, 18)
    f32 = jnp.float32
    nrm = lambda k, shape, scale: jax.random.normal(k, shape, f32) * scale
    x = jax.random.normal(ks[0], (BATCH, SEQ, D_MODEL), f32)
    offset = jax.random.randint(ks[1], (BATCH, 1), 0, 4096, dtype=jnp.int32)
    positions = offset + jnp.arange(SEQ, dtype=jnp.int32)[None, :]
    return {
        "x": x,
        "positions": positions,
        "ln_gain": 1.0 + nrm(ks[2], (DEPTH, 2, D_MODEL), 0.02),
        "ln_bias": nrm(ks[3], (DEPTH, 2, D_MODEL), 0.02),
        "mix_w_in": nrm(ks[4], (N_EVEN, D_MODEL, IN_COLS), D_MODEL ** -0.5),
        "ret_gn_gain": 1.0 + nrm(ks[5], (N_EVEN, RET_WIDTH), 0.02),
        "mix_w_out": nrm(ks[6], (N_EVEN, RET_WIDTH + MOBA_WIDTH, D_MODEL), (RET_WIDTH + MOBA_WIDTH) ** -0.5 * DEEPNORM_BETA),
        "gmlp_w_uv": nrm(ks[7], (N_ODD, D_MODEL, 2 * GMLP_WIDTH), D_MODEL ** -0.5),
        "gmlp_b_uv": nrm(ks[8], (N_ODD, 2 * GMLP_WIDTH), 0.02),
        "gmlp_ln_gain": 1.0 + nrm(ks[9], (N_ODD, GMLP_WIDTH), 0.02),
        "gmlp_ln_bias": nrm(ks[10], (N_ODD, GMLP_WIDTH), 0.02),
        "gmlp_w_s": nrm(ks[11], (N_ODD, GMLP_GROUPS, GMLP_CHUNK, GMLP_CHUNK), GMLP_CHUNK ** -0.5),
        "gmlp_b_s": 1.0 + nrm(ks[12], (N_ODD, GMLP_GROUPS, GMLP_CHUNK), 0.02),
        "gmlp_w_out": nrm(ks[13], (N_ODD, GMLP_WIDTH, D_MODEL), GMLP_WIDTH ** -0.5 * DEEPNORM_BETA),
        "ffn_w_in": nrm(ks[14], (DEPTH, D_MODEL, FFN_HIDDEN), D_MODEL ** -0.5),
        "ffn_w_out": nrm(ks[15], (DEPTH, FFN_HIDDEN, D_MODEL), FFN_HIDDEN ** -0.5 * DEEPNORM_BETA),
    }


def reference(x, positions, ln_gain, ln_bias, mix_w_in, ret_gn_gain, mix_w_out,
              gmlp_w_uv, gmlp_b_uv, gmlp_ln_gain, gmlp_ln_bias, gmlp_w_s, gmlp_b_s,
              gmlp_w_out, ffn_w_in, ffn_w_out):
    cos, sin = rope_tables(positions)
    for layer in range(DEPTH):
        i = layer // 2
        if layer % 2 == 0:
            y = retention_moba_mixer(x, cos, sin, mix_w_in[i], ret_gn_gain[i], mix_w_out[i])
        else:
            y = gmlp_mixer(x, gmlp_w_uv[i], gmlp_b_uv[i], gmlp_ln_gain[i], gmlp_ln_bias[i],
                           gmlp_w_s[i], gmlp_b_s[i], gmlp_w_out[i])
        x = layer_norm(DEEPNORM_ALPHA * x + y, ln_gain[layer, 0], ln_bias[layer, 0])
        y = channel_mixer(x, ffn_w_in[layer], ffn_w_out[layer])
        x = layer_norm(DEEPNORM_ALPHA * x + y, ln_gain[layer, 1], ln_bias[layer, 1])
    return x
```

```python
import functools

import jax
import jax.numpy as jnp
from jax import lax
from jax.experimental import pallas as pl
from jax.experimental.pallas import tpu as pltpu

HEAD_DIM = 64
N_RET_HEADS = 8
N_MOBA_HEADS = 8
RET_WIDTH = N_RET_HEADS * HEAD_DIM
MOBA_WIDTH = N_MOBA_HEADS * HEAD_DIM
RET_CHUNK = 128
MOBA_BLOCK = 256
MOBA_TOPK = 3
GMLP_CHUNK = 128
GMLP_GROUPS = 8
ROPE_THETA = 10000.0
LN_EPS = 1e-5
DEPTH = 4
DEEPNORM_ALPHA = (2 * DEPTH) ** 0.25

LANES = 128
VMEM_LIMIT_BYTES = 52 * 1024 * 1024

BF16 = jnp.bfloat16
F32 = jnp.float32
NEG_INF = float("-inf")


def _params(*sem):
    return pltpu.CompilerParams(dimension_semantics=sem, vmem_limit_bytes=VMEM_LIMIT_BYTES)


def _layer_norm_rows(z, g, b):
    mu = jnp.mean(z, axis=-1, keepdims=True)
    d = z - mu
    var = jnp.mean(d * d, axis=-1, keepdims=True)
    return d * lax.rsqrt(var + LN_EPS) * g + b


def _rope_table_kernel(pos_ref, freq_ref, cos_ref, sin_ref):
    ang = pos_ref[...] * freq_ref[...]
    lane = lax.broadcasted_iota(jnp.int32, ang.shape, 1)
    first_half = (lane % HEAD_DIM) < (HEAD_DIM // 2)
    cos_ref[...] = jnp.cos(ang)
    sin_ref[...] = jnp.where(first_half, -jnp.sin(ang), jnp.sin(ang))


def _rope_tables(positions):
    t = positions.size
    tm = min(t, 2048)
    pos = positions.reshape(t, 1).astype(F32)
    inv_freq = ROPE_THETA ** (-jnp.arange(0, HEAD_DIM, 2, dtype=F32) / HEAD_DIM)
    freq = jnp.tile(inv_freq, LANES // (HEAD_DIM // 2)).reshape(1, LANES)
    return pl.pallas_call(
        _rope_table_kernel,
        grid=(t // tm,),
        in_specs=[pl.BlockSpec((tm, 1), lambda i: (i, 0)),
                  pl.BlockSpec((1, LANES), lambda i: (0, 0))],
        out_specs=[pl.BlockSpec((tm, LANES), lambda i: (i, 0))] * 2,
        out_shape=[jax.ShapeDtypeStruct((t, LANES), F32)] * 2,
        compiler_params=_params("parallel"),
        name="rope_tables",
    )(pos, freq)


def _in_proj_kernel(x_ref, w_ref, cos_ref, sin_ref,
                    qa_ref, ka_ref, va_ref, ga_ref, qb_ref, kb_ref, vb_ref, km_ref):
    tm = x_ref.shape[0]
    width = RET_WIDTH
    xb = x_ref[...].astype(BF16)
    reps = width // LANES
    cos = jnp.concatenate([cos_ref[...]] * reps, axis=1)
    sin = jnp.concatenate([sin_ref[...]] * reps, axis=1)
    lane = lax.broadcasted_iota(jnp.int32, (tm, width), 1)
    first_half = (lane % HEAD_DIM) < (HEAD_DIM // 2)

    def proj(g):
        return jnp.dot(xb, w_ref[:, g * width:(g + 1) * width], preferred_element_type=F32)

    def rope(t):
        partner = jnp.where(first_half,
                            pltpu.roll(t, width - HEAD_DIM // 2, 1),
                            pltpu.roll(t, HEAD_DIM // 2, 1))
        return t * cos + partner * sin

    qa_ref[...] = rope(proj(0)).astype(qa_ref.dtype)
    ka_ref[...] = rope(proj(1)).astype(ka_ref.dtype)
    va_ref[...] = proj(2).astype(va_ref.dtype)
    ga_ref[...] = proj(3).astype(ga_ref.dtype)
    qb_ref[...] = rope(proj(4)).astype(qb_ref.dtype)
    kb = rope(proj(5))
    kb_ref[...] = kb.astype(kb_ref.dtype)
    vb_ref[...] = proj(6).astype(vb_ref.dtype)
    for blk in range(tm // MOBA_BLOCK):
        km_ref[blk] = jnp.mean(kb[blk * MOBA_BLOCK:(blk + 1) * MOBA_BLOCK], axis=0, keepdims=True)


def _in_proj(x2d, w_bf16, cos_t, sin_t, tm=512):
    t, d = x2d.shape
    n = w_bf16.shape[1]
    width = RET_WIDTH
    row = lambda i: (i, 0)
    out_dtypes = [BF16, F32, BF16, F32, F32, BF16, BF16]
    return pl.pallas_call(
        _in_proj_kernel,
        grid=(t // tm,),
        in_specs=[pl.BlockSpec((tm, d), row),
                  pl.BlockSpec((d, n), lambda i: (0, 0)),
                  pl.BlockSpec((tm, LANES), row),
                  pl.BlockSpec((tm, LANES), row)],
        out_specs=[pl.BlockSpec((tm, width), row)] * 7
        + [pl.BlockSpec((tm // MOBA_BLOCK, 1, width), lambda i: (i, 0, 0))],
        out_shape=[jax.ShapeDtypeStruct((t, width), dt) for dt in out_dtypes]
        + [jax.ShapeDtypeStruct((t // MOBA_BLOCK, 1, width), F32)],
        compiler_params=_params("parallel"),
        name="in_proj",
    )(x2d, w_bf16, cos_t, sin_t)


def _retention_kernel(lg_ref, q_ref, k_ref, v_ref, g_ref, gain_ref, o_ref, state_ref):
    hp = pl.program_id(1)
    c = RET_CHUNK
    tm = q_ref.shape[0]

    @pl.when(pl.program_id(2) == 0)
    def _():
        state_ref[...] = jnp.zeros_like(state_ref)

    lg0 = lg_ref[2 * hp]
    lg1 = lg_ref[2 * hp + 1]
    lane = lax.broadcasted_iota(jnp.int32, (c, LANES), 1)
    row = lax.broadcasted_iota(jnp.int32, (c, LANES), 0)
    head0 = lane < HEAD_DIM
    lg_lane = jnp.where(head0, lg0, lg1)
    idx = row.astype(F32)
    xi = jnp.exp(lg_lane * (idx + 1.0))
    zeta = jnp.exp(lg_lane * (c - 1.0 - idx))
    chunk_decay = jnp.exp(lg_lane * float(c))
    diff = (row - lane).astype(F32)
    dpos = jnp.maximum(diff, 0.0)
    decay0 = jnp.where(diff >= 0, jnp.exp(lg0 * dpos), 0.0)
    decay1 = jnp.where(diff >= 0, jnp.exp(lg1 * dpos), 0.0)
    same_head = (row < HEAD_DIM) == head0
    gain = gain_ref[...]
    nt = (((1,), (1,)), ((), ()))
    tn = (((0,), (0,)), ((), ()))

    for ci in range(tm // c):
        sl = slice(ci * c, (ci + 1) * c)
        q = q_ref[sl, :]
        kf = k_ref[sl, :] * (HEAD_DIM ** -0.5)
        v = v_ref[sl, :]
        kb = kf.astype(BF16)
        zero = jnp.zeros_like(q)
        q0 = jnp.where(head0, q, zero)
        q1 = jnp.where(head0, zero, q)
        s0 = lax.dot_general(q0, kb, nt, preferred_element_type=F32) * decay0
        s1 = lax.dot_general(q1, kb, nt, preferred_element_type=F32) * decay1
        inner = jnp.where(head0,
                          jnp.dot(s0.astype(BF16), v, preferred_element_type=F32),
                          jnp.dot(s1.astype(BF16), v, preferred_element_type=F32))
        state = state_ref[...]
        cross = jnp.dot(q, state.astype(BF16), preferred_element_type=F32) * xi
        kv = lax.dot_general((kf * zeta).astype(BF16), v, tn, preferred_element_type=F32)
        state_ref[...] = state * chunk_decay + jnp.where(same_head, kv, 0.0)

        y = inner + cross
        inv = 1.0 / HEAD_DIM
        mu = jnp.where(head0,
                       jnp.sum(jnp.where(head0, y, 0.0), axis=-1, keepdims=True),
                       jnp.sum(jnp.where(head0, 0.0, y), axis=-1, keepdims=True)) * inv
        d = y - mu
        dd = d * d
        var = jnp.where(head0,
                        jnp.sum(jnp.where(head0, dd, 0.0), axis=-1, keepdims=True),
                        jnp.sum(jnp.where(head0, 0.0, dd), axis=-1, keepdims=True)) * inv
        yn = d * lax.rsqrt(var + LN_EPS) * gain
        o_ref[sl, :] = (yn * jax.nn.silu(g_ref[sl, :])).astype(o_ref.dtype)


def _retention(qa, ka, va, ga, gn_gain, log_gamma, batch, tm=512):
    t, width = qa.shape
    s = t // batch
    nt = s // tm
    blk = lambda b, h, i: (b * nt + i, h)
    return pl.pallas_call(
        _retention_kernel,
        grid=(batch, width // LANES, nt),
        in_specs=[pl.BlockSpec(memory_space=pltpu.SMEM),
                  pl.BlockSpec((tm, LANES), blk),
                  pl.BlockSpec((tm, LANES), blk),
                  pl.BlockSpec((tm, LANES), blk),
                  pl.BlockSpec((tm, LANES), blk),
                  pl.BlockSpec((1, LANES), lambda b, h, i: (0, h))],
        out_specs=pl.BlockSpec((tm, LANES), blk),
        out_shape=jax.ShapeDtypeStruct((t, width), BF16),
        scratch_shapes=[pltpu.VMEM((LANES, LANES), F32)],
        compiler_params=_params("parallel", "parallel", "arbitrary"),
        name="retention",
    )(log_gamma, qa, ka, va, ga, gn_gain.reshape(1, width))


def _moba_kernel(q_ref, k_ref, v_ref, km_ref, o_ref):
    qi = pl.program_id(2)
    tq = q_ref.shape[0]
    nb = km_ref.shape[0]
    qf = q_ref[...]
    km = km_ref[...]
    lane = lax.broadcasted_iota(jnp.int32, (tq, LANES), 1)
    head0 = lane < HEAD_DIM
    blk = lax.broadcasted_iota(jnp.int32, (tq, nb), 1)
    past = blk < qi
    nt = (((1,), (1,)), ((), ()))
    k_own = k_ref[pl.ds(pl.multiple_of(qi * MOBA_BLOCK, MOBA_BLOCK), MOBA_BLOCK), :]
    v_own = v_ref[pl.ds(pl.multiple_of(qi * MOBA_BLOCK, MOBA_BLOCK), MOBA_BLOCK), :]
    r = lax.broadcasted_iota(jnp.int32, (tq, MOBA_BLOCK), 0)
    cidx = lax.broadcasted_iota(jnp.int32, (tq, MOBA_BLOCK), 1)
    causal = cidx <= r

    outs = []
    for h in range(2):
        qh = jnp.where(head0, qf, 0.0) if h == 0 else jnp.where(head0, 0.0, qf)
        gate = lax.dot_general(qh, km, nt, preferred_element_type=F32,
                               precision=lax.Precision.HIGHEST)
        g = jnp.where(past, gate, NEG_INF)
        bias = jnp.full((tq, nb), NEG_INF, F32)
        for _ in range(min(MOBA_TOPK, nb)):
            m = jnp.max(g, axis=-1, keepdims=True)
            first = jnp.min(jnp.where(g == m, blk, nb), axis=-1, keepdims=True)
            pick = blk == first
            bias = jnp.where(pick, 0.0, bias)
            g = jnp.where(pick, NEG_INF, g)
        bias = jnp.where(past, bias, NEG_INF)

        qs = (qh * (HEAD_DIM ** -0.5)).astype(BF16)
        s = lax.dot_general(qs, k_own, nt, preferred_element_type=F32)
        s = jnp.where(causal, s, NEG_INF)
        m0 = jnp.max(s, axis=-1, keepdims=True)
        p = jnp.exp(s - m0)
        l0 = jnp.sum(p, axis=-1, keepdims=True)
        acc0 = jnp.dot(p.astype(BF16), v_own, preferred_element_type=F32)

        def body(j, carry, qs=qs, bias=bias):
            m_prev, l_prev, acc = carry
            off = pl.multiple_of(j * MOBA_BLOCK, MOBA_BLOCK)
            kj = k_ref[pl.ds(off, MOBA_BLOCK), :]
            vj = v_ref[pl.ds(off, MOBA_BLOCK), :]
            bj = jnp.sum(jnp.where(blk == j, bias, 0.0), axis=-1, keepdims=True)
            sj = lax.dot_general(qs, kj, nt, preferred_element_type=F32) + bj
            m_new = jnp.maximum(m_prev, jnp.max(sj, axis=-1, keepdims=True))
            alpha = jnp.exp(m_prev - m_new)
            pj = jnp.exp(sj - m_new)
            l_new = alpha * l_prev + jnp.sum(pj, axis=-1, keepdims=True)
            acc_new = alpha * acc + jnp.dot(pj.astype(BF16), vj, preferred_element_type=F32)
            return m_new, l_new, acc_new

        _, l_fin, acc = lax.fori_loop(0, qi, body, (m0, l0, acc0))
        outs.append(acc / l_fin)
    o_ref[...] = jnp.where(head0, outs[0], outs[1]).astype(o_ref.dtype)


def _moba(qb, kb, vb, kmean, batch):
    t, width = qb.shape
    s = t // batch
    tq = MOBA_BLOCK
    nq = s // tq
    nb = s // MOBA_BLOCK
    km = kmean.reshape(batch, nb, width)
    qblk = lambda b, h, i: (b * nq + i, h)
    kvblk = lambda b, h, i: (b, h)
    return pl.pallas_call(
        _moba_kernel,
        grid=(batch, width // LANES, nq),
        in_specs=[pl.BlockSpec((tq, LANES), qblk),
                  pl.BlockSpec((s, LANES), kvblk),
                  pl.BlockSpec((s, LANES), kvblk),
                  pl.BlockSpec((None, nb, LANES), lambda b, h, i: (b, 0, h))],
        out_specs=pl.BlockSpec((tq, LANES), qblk),
        out_shape=jax.ShapeDtypeStruct((t, width), BF16),
        compiler_params=_params("parallel", "parallel", "arbitrary"),
        name="moba",
    )(qb, kb, vb, km)


def _out_proj_kernel(*refs, n_in):
    a_refs = refs[:n_in]
    w_ref, x_ref, g_ref, b_ref, o_ref = refs[n_in:]
    y = None
    off = 0
    for a_ref in a_refs:
        kdim = a_ref.shape[1]
        part = jnp.dot(a_ref[...], w_ref[off:off + kdim, :], preferred_element_type=F32)
        y = part if y is None else y + part
        off += kdim
    z = DEEPNORM_ALPHA * x_ref[...] + y
    o_ref[...] = _layer_norm_rows(z, g_ref[...], b_ref[...])


def _out_proj_ln(acts, w_bf16, x2d, gain, bias, tm=512):
    t, d = x2d.shape
    row = lambda i: (i, 0)
    const = lambda i: (0, 0)
    return pl.pallas_call(
        functools.partial(_out_proj_kernel, n_in=len(acts)),
        grid=(t // tm,),
        in_specs=[pl.BlockSpec((tm, a.shape[1]), row) for a in acts]
        + [pl.BlockSpec(w_bf16.shape, const),
           pl.BlockSpec((tm, d), row),
           pl.BlockSpec((1, d), const),
           pl.BlockSpec((1, d), const)],
        out_specs=pl.BlockSpec((tm, d), row),
        out_shape=jax.ShapeDtypeStruct((t, d), F32),
        compiler_params=_params("parallel"),
        name="out_proj_ln",
    )(*acts, w_bf16, x2d, gain.reshape(1, d), bias.reshape(1, d))


def _ffn_kernel(x_ref, w1_ref, w2_ref, g_ref, b_ref, o_ref, xb_ref, acc_ref):
    j = pl.program_id(1)

    @pl.when(j == 0)
    def _():
        xb_ref[...] = x_ref[...].astype(BF16)
        acc_ref[...] = jnp.zeros_like(acc_ref)

    h = jnp.dot(xb_ref[...], w1_ref[...], preferred_element_type=F32)
    h = jnp.square(jnp.maximum(h, 0.0)).astype(BF16)
    acc_ref[...] += jnp.dot(h, w2_ref[...], preferred_element_type=F32)

    @pl.when(j == pl.num_programs(1) - 1)
    def _():
        z = DEEPNORM_ALPHA * x_ref[...] + acc_ref[...]
        o_ref[...] = _layer_norm_rows(z, g_ref[...], b_ref[...])


def _ffn_ln(x2d, w1_bf16, w2_bf16, gain, bias, tm=1024, th=512):
    t, d = x2d.shape
    hdim = w1_bf16.shape[1]
    return pl.pallas_call(
        _ffn_kernel,
        grid=(t // tm, hdim // th),
        in_specs=[pl.BlockSpec((tm, d), lambda i, j: (i, 0)),
                  pl.BlockSpec((d, th), lambda i, j: (0, j)),
                  pl.BlockSpec((th, d), lambda i, j: (j, 0)),
                  pl.BlockSpec((1, d), lambda i, j: (0, 0)),
                  pl.BlockSpec((1, d), lambda i, j: (0, 0))],
        out_specs=pl.BlockSpec((tm, d), lambda i, j: (i, 0)),
        out_shape=jax.ShapeDtypeStruct((t, d), F32),
        scratch_shapes=[pltpu.VMEM((tm, d), BF16), pltpu.VMEM((tm, d), F32)],
        compiler_params=_params("parallel", "arbitrary"),
        name="ffn_ln",
    )(x2d, w1_bf16, w2_bf16, gain.reshape(1, d), bias.reshape(1, d))


def _gmlp_kernel(x_ref, w_ref, b_ref, lng_ref, lnb_ref, ws_ref, bst_ref, o_ref, v_ref):
    tm = x_ref.shape[0]
    gw = o_ref.shape[1]
    c = GMLP_CHUNK
    cw = gw // GMLP_GROUPS
    xb = x_ref[...].astype(BF16)
    v = jnp.dot(xb, w_ref[:, gw:], preferred_element_type=F32) + b_ref[:, gw:]
    v_ref[...] = _layer_norm_rows(jax.nn.gelu(v), lng_ref[...], lnb_ref[...]).astype(BF16)
    r = lax.broadcasted_iota(jnp.int32, (c, c), 0)
    cc = lax.broadcasted_iota(jnp.int32, (c, c), 1)
    causal = cc <= r
    bst = bst_ref[...]
    for g in range(GMLP_GROUPS):
        cols = slice(g * cw, (g + 1) * cw)
        ws = jnp.where(causal, ws_ref[g], 0.0).astype(BF16)
        u = jax.nn.gelu(jnp.dot(xb, w_ref[:, cols], preferred_element_type=F32) + b_ref[:, cols])
        for ci in range(tm // c):
            rows = slice(ci * c, (ci + 1) * c)
            sv = jnp.dot(ws, v_ref[rows, cols], preferred_element_type=F32) + bst[:, g:g + 1]
            o_ref[rows, cols] = (u[rows] * sv).astype(o_ref.dtype)


def _gmlp_gate(x2d, w_uv_bf16, b_uv, ln_g, ln_b, w_s, b_s, tm=512):
    t, d = x2d.shape
    n = w_uv_bf16.shape[1]
    gw = n // 2
    row = lambda i: (i, 0)
    const = lambda i: (0, 0)
    return pl.pallas_call(
        _gmlp_kernel,
        grid=(t // tm,),
        in_specs=[pl.BlockSpec((tm, d), row),
                  pl.BlockSpec((d, n), const),
                  pl.BlockSpec((1, n), const),
                  pl.BlockSpec((1, gw), const),
                  pl.BlockSpec((1, gw), const),
                  pl.BlockSpec(w_s.shape, lambda i: (0, 0, 0)),
                  pl.BlockSpec((GMLP_CHUNK, GMLP_GROUPS), const)],
        out_specs=pl.BlockSpec((tm, gw), row),
        out_shape=jax.ShapeDtypeStruct((t, gw), BF16),
        scratch_shapes=[pltpu.VMEM((tm, gw), BF16)],
        compiler_params=_params("parallel"),
        name="gmlp_gate",
    )(x2d, w_uv_bf16, b_uv.reshape(1, n), ln_g.reshape(1, gw), ln_b.reshape(1, gw), w_s, b_s.T)


def kernel(x, positions, ln_gain, ln_bias, mix_w_in, ret_gn_gain, mix_w_out,
           gmlp_w_uv, gmlp_b_uv, gmlp_ln_gain, gmlp_ln_bias, gmlp_w_s, gmlp_b_s,
           gmlp_w_out, ffn_w_in, ffn_w_out):
    batch, seq, d = x.shape
    depth = ln_gain.shape[0]
    h = x.reshape(batch * seq, d)
    cos_t, sin_t = _rope_tables(positions)
    log_gamma = jnp.log1p(-jnp.exp2(-5.0 - jnp.arange(N_RET_HEADS, dtype=F32)))
    for layer in range(depth):
        i = layer // 2
        if layer % 2 == 0:
            qa, ka, va, ga, qb, kb, vb, kmean = _in_proj(h, mix_w_in[i].astype(BF16), cos_t, sin_t)
            ya = _retention(qa, ka, va, ga, ret_gn_gain[i], log_gamma, batch)
            yb = _moba(qb, kb, vb, kmean, batch)
            h = _out_proj_ln([ya, yb], mix_w_out[i].astype(BF16), h,
                             ln_gain[layer, 0], ln_bias[layer, 0])
        else:
            gated = _gmlp_gate(h, gmlp_w_uv[i].astype(BF16), gmlp_b_uv[i], gmlp_ln_gain[i],
                               gmlp_ln_bias[i], gmlp_w_s[i], gmlp_b_s[i])
            h = _out_proj_ln([gated], gmlp_w_out[i].astype(BF16), h,
                             ln_gain[layer, 0], ln_bias[layer, 0])
        h = _ffn_ln(h, ffn_w_in[layer].astype(BF16), ffn_w_out[layer].astype(BF16),
                    ln_gain[layer, 1], ln_bias[layer, 1])
    return h.reshape(batch, seq, d)
```

```python
import functools

import jax
import jax.numpy as jnp
from jax import lax
from jax.experimental import pallas as pl
from jax.experimental.pallas import tpu as pltpu

HEAD_DIM = 64
N_RET_HEADS = 8
N_MOBA_HEADS = 8
RET_WIDTH = N_RET_HEADS * HEAD_DIM
MOBA_WIDTH = N_MOBA_HEADS * HEAD_DIM
RET_CHUNK = 128
MOBA_BLOCK = 256
MOBA_TOPK = 3
GMLP_CHUNK = 128
GMLP_GROUPS = 8
ROPE_THETA = 10000.0
LN_EPS = 1e-5
DEPTH = 4
DEEPNORM_ALPHA = (2 * DEPTH) ** 0.25

LANES = 128
VMEM_LIMIT_BYTES = 52 * 1024 * 1024

BF16 = jnp.bfloat16
F32 = jnp.float32
NEG_INF = float("-inf")
MOBA_GROUP = 4
MASK_BIAS = -1e30


def _params(*sem):
    return pltpu.CompilerParams(dimension_semantics=sem, vmem_limit_bytes=VMEM_LIMIT_BYTES)


def _layer_norm_rows(z, g, b):
    mu = jnp.mean(z, axis=-1, keepdims=True)
    d = z - mu
    var = jnp.mean(d * d, axis=-1, keepdims=True)
    return d * lax.rsqrt(var + LN_EPS) * g + b


def _rope_table_kernel(pos_ref, freq_ref, cos_ref, sin_ref):
    ang = pos_ref[...] * freq_ref[...]
    lane = lax.broadcasted_iota(jnp.int32, ang.shape, 1)
    first_half = (lane % HEAD_DIM) < (HEAD_DIM // 2)
    cos_ref[...] = jnp.cos(ang)
    sin_ref[...] = jnp.where(first_half, -jnp.sin(ang), jnp.sin(ang))


def _rope_tables(positions):
    t = positions.size
    tm = min(t, 2048)
    pos = positions.reshape(t, 1).astype(F32)
    inv_freq = ROPE_THETA ** (-jnp.arange(0, HEAD_DIM, 2, dtype=F32) / HEAD_DIM)
    freq = jnp.tile(inv_freq, LANES // (HEAD_DIM // 2)).reshape(1, LANES)
    return pl.pallas_call(
        _rope_table_kernel,
        grid=(t // tm,),
        in_specs=[pl.BlockSpec((tm, 1), lambda i: (i, 0)),
                  pl.BlockSpec((1, LANES), lambda i: (0, 0))],
        out_specs=[pl.BlockSpec((tm, LANES), lambda i: (i, 0))] * 2,
        out_shape=[jax.ShapeDtypeStruct((t, LANES), F32)] * 2,
        compiler_params=_params("parallel"),
        name="rope_tables",
    )(pos, freq)


def _in_proj_kernel(x_ref, w_ref, cos_ref, sin_ref,
                    qa_ref, ka_ref, va_ref, ga_ref, qb_ref, kbe_ref, kbo_ref, vbe_ref, vbo_ref,
                    km_ref, *, n_blocks):
    tm = x_ref.shape[0]
    width = RET_WIDTH
    xb = x_ref[...].astype(BF16)
    reps = width // LANES
    cos = jnp.concatenate([cos_ref[...]] * reps, axis=1)
    sin = jnp.concatenate([sin_ref[...]] * reps, axis=1)
    lane = lax.broadcasted_iota(jnp.int32, (tm, width), 1)
    first_half = (lane % HEAD_DIM) < (HEAD_DIM // 2)

    def proj(g):
        return jnp.dot(xb, w_ref[:, g * width:(g + 1) * width], preferred_element_type=F32)

    def rope(t):
        partner = jnp.where(first_half,
                            pltpu.roll(t, width - HEAD_DIM // 2, 1),
                            pltpu.roll(t, HEAD_DIM // 2, 1))
        return t * cos + partner * sin

    qa_ref[...] = rope(proj(0)).astype(qa_ref.dtype)
    ka_ref[...] = rope(proj(1)).astype(ka_ref.dtype)
    va_ref[...] = proj(2).astype(va_ref.dtype)
    ga_ref[...] = proj(3).astype(ga_ref.dtype)
    qb_ref[...] = rope(proj(4)).astype(qb_ref.dtype)
    kb = rope(proj(5))
    vb = proj(6)
    pair_lane = lane % LANES
    even_data = pair_lane < HEAD_DIM
    row = lax.broadcasted_iota(jnp.int32, (tm, width), 0)
    block = ((pl.program_id(0) * tm + row) // MOBA_BLOCK) % n_blocks
    onehot_e = (pair_lane - HEAD_DIM == block).astype(F32)
    onehot_o = (pair_lane == block).astype(F32)
    kbe_ref[...] = jnp.where(even_data, kb, onehot_e).astype(kbe_ref.dtype)
    kbo_ref[...] = jnp.where(even_data, onehot_o, kb).astype(kbo_ref.dtype)
    vbe_ref[...] = jnp.where(even_data, vb, 1.0).astype(vbe_ref.dtype)
    vbo_ref[...] = jnp.where(even_data, 1.0, vb).astype(vbo_ref.dtype)
    for blk in range(tm // MOBA_BLOCK):
        km_ref[blk] = jnp.mean(kb[blk * MOBA_BLOCK:(blk + 1) * MOBA_BLOCK], axis=0, keepdims=True)


def _in_proj(x2d, w_bf16, cos_t, sin_t, n_blocks, tm=512):
    t, d = x2d.shape
    n = w_bf16.shape[1]
    width = RET_WIDTH
    assert n_blocks <= HEAD_DIM and tm % MOBA_BLOCK == 0
    row = lambda i: (i, 0)
    out_dtypes = [BF16, F32, BF16, F32, F32, BF16, BF16, BF16, BF16]
    return pl.pallas_call(
        functools.partial(_in_proj_kernel, n_blocks=n_blocks),
        grid=(t // tm,),
        in_specs=[pl.BlockSpec((tm, d), row),
                  pl.BlockSpec((d, n), lambda i: (0, 0)),
                  pl.BlockSpec((tm, LANES), row),
                  pl.BlockSpec((tm, LANES), row)],
        out_specs=[pl.BlockSpec((tm, width), row)] * len(out_dtypes)
        + [pl.BlockSpec((tm // MOBA_BLOCK, 1, width), lambda i: (i, 0, 0))],
        out_shape=[jax.ShapeDtypeStruct((t, width), dt) for dt in out_dtypes]
        + [jax.ShapeDtypeStruct((t // MOBA_BLOCK, 1, width), F32)],
        compiler_params=_params("parallel"),
        name="in_proj",
    )(x2d, w_bf16, cos_t, sin_t)


def _retention_kernel(lg_ref, q_ref, k_ref, v_ref, g_ref, gain_ref, o_ref, state_ref):
    hp = pl.program_id(1)
    c = RET_CHUNK
    tm = q_ref.shape[0]

    @pl.when(pl.program_id(2) == 0)
    def _():
        state_ref[...] = jnp.zeros_like(state_ref)

    lg0 = lg_ref[2 * hp]
    lg1 = lg_ref[2 * hp + 1]
    lane = lax.broadcasted_iota(jnp.int32, (c, LANES), 1)
    row = lax.broadcasted_iota(jnp.int32, (c, LANES), 0)
    head0 = lane < HEAD_DIM
    lg_lane = jnp.where(head0, lg0, lg1)
    idx = row.astype(F32)
    xi = jnp.exp(lg_lane * (idx + 1.0))
    zeta = jnp.exp(lg_lane * (c - 1.0 - idx))
    chunk_decay = jnp.exp(lg_lane * float(c))
    diff = (row - lane).astype(F32)
    dpos = jnp.maximum(diff, 0.0)
    decay0 = jnp.where(diff >= 0, jnp.exp(lg0 * dpos), 0.0)
    decay1 = jnp.where(diff >= 0, jnp.exp(lg1 * dpos), 0.0)
    same_head = (row < HEAD_DIM) == head0
    gain = gain_ref[...]
    nt = (((1,), (1,)), ((), ()))
    tn = (((0,), (0,)), ((), ()))

    for ci in range(tm // c):
        sl = slice(ci * c, (ci + 1) * c)
        q = q_ref[sl, :]
        kf = k_ref[sl, :] * (HEAD_DIM ** -0.5)
        v = v_ref[sl, :]
        kb = kf.astype(BF16)
        zero = jnp.zeros_like(q)
        q0 = jnp.where(head0, q, zero)
        q1 = jnp.where(head0, zero, q)
        s0 = lax.dot_general(q0, kb, nt, preferred_element_type=F32) * decay0
        s1 = lax.dot_general(q1, kb, nt, preferred_element_type=F32) * decay1
        inner = jnp.where(head0,
                          jnp.dot(s0.astype(BF16), v, preferred_element_type=F32),
                          jnp.dot(s1.astype(BF16), v, preferred_element_type=F32))
        state = state_ref[...]
        cross = jnp.dot(q, state.astype(BF16), preferred_element_type=F32) * xi
        kv = lax.dot_general((kf * zeta).astype(BF16), v, tn, preferred_element_type=F32)
        state_ref[...] = state * chunk_decay + jnp.where(same_head, kv, 0.0)

        y = inner + cross
        inv = 1.0 / HEAD_DIM
        mu = jnp.where(head0,
                       jnp.sum(jnp.where(head0, y, 0.0), axis=-1, keepdims=True),
                       jnp.sum(jnp.where(head0, 0.0, y), axis=-1, keepdims=True)) * inv
        d = y - mu
        dd = d * d
        var = jnp.where(head0,
                        jnp.sum(jnp.where(head0, dd, 0.0), axis=-1, keepdims=True),
                        jnp.sum(jnp.where(head0, 0.0, dd), axis=-1, keepdims=True)) * inv
        yn = d * lax.rsqrt(var + LN_EPS) * gain
        o_ref[sl, :] = (yn * jax.nn.silu(g_ref[sl, :])).astype(o_ref.dtype)


def _retention(qa, ka, va, ga, gn_gain, log_gamma, batch, tm=512):
    t, width = qa.shape
    s = t // batch
    nt = s // tm
    blk = lambda b, h, i: (b * nt + i, h)
    return pl.pallas_call(
        _retention_kernel,
        grid=(batch, width // LANES, nt),
        in_specs=[pl.BlockSpec(memory_space=pltpu.SMEM),
                  pl.BlockSpec((tm, LANES), blk),
                  pl.BlockSpec((tm, LANES), blk),
                  pl.BlockSpec((tm, LANES), blk),
                  pl.BlockSpec((tm, LANES), blk),
                  pl.BlockSpec((1, LANES), lambda b, h, i: (0, h))],
        out_specs=pl.BlockSpec((tm, LANES), blk),
        out_shape=jax.ShapeDtypeStruct((t, width), BF16),
        scratch_shapes=[pltpu.VMEM((LANES, LANES), F32)],
        compiler_params=_params("parallel", "parallel", "arbitrary"),
        name="retention",
    )(log_gamma, qa, ka, va, ga, gn_gain.reshape(1, width))


def _moba_kernel(q_ref, ke_ref, ko_ref, ve_ref, vo_ref, km_ref, o_ref,
                 s_ref, sown_ref, m_ref, acc_ref):
    qi = pl.program_id(2)
    tq = q_ref.shape[0]
    nb = km_ref.shape[0]
    slab = MOBA_GROUP * MOBA_BLOCK
    qf = q_ref[...]
    km = km_ref[...]
    lane = lax.broadcasted_iota(jnp.int32, (tq, LANES), 1)
    even_data = lane < HEAD_DIM
    nt = (((1,), (1,)), ((), ()))
    own = pl.multiple_of(qi * MOBA_BLOCK, MOBA_BLOCK)
    r = lax.broadcasted_iota(jnp.int32, (tq, MOBA_BLOCK), 0)
    cidx = lax.broadcasted_iota(jnp.int32, (tq, MOBA_BLOCK), 1)
    causal = cidx <= r
    n_groups = (qi + MOBA_GROUP - 1) // MOBA_GROUP
    heads = ((ke_ref, ve_ref), (ko_ref, vo_ref))

    q_biased = []
    for h, (k_ref, v_ref) in enumerate(heads):
        data = even_data if h == 0 else jnp.logical_not(even_data)
        qh = jnp.where(data, qf, 0.0)
        if h == 0:
            km_rows = jnp.concatenate([jnp.zeros((HEAD_DIM, LANES), F32), km,
                                       jnp.zeros((HEAD_DIM - nb, LANES), F32)], axis=0)
            blk = jnp.where(data, nb, lane - HEAD_DIM)
        else:
            km_rows = jnp.concatenate([km, jnp.zeros((LANES - nb, LANES), F32)], axis=0)
            blk = jnp.where(data, nb, lane)
        past = blk < qi
        gate = lax.dot_general(qh, km_rows, nt, preferred_element_type=F32,
                               precision=lax.Precision.HIGHEST)
        g = jnp.where(past, gate, NEG_INF)
        bias = jnp.full((tq, LANES), MASK_BIAS, F32)
        for _ in range(min(MOBA_TOPK, nb)):
            m = jnp.max(g, axis=-1, keepdims=True)
            first = jnp.min(jnp.where(g == m, blk, nb), axis=-1, keepdims=True)
            pick = blk == first
            bias = jnp.where(pick, 0.0, bias)
            g = jnp.where(pick, NEG_INF, g)
        bias = jnp.where(past, bias, MASK_BIAS)
        qs = qh * (HEAD_DIM ** -0.5)
        q_biased.append(jnp.where(data, qs, bias).astype(BF16))

        s_own = lax.dot_general(qs.astype(BF16), k_ref[pl.ds(own, MOBA_BLOCK), :], nt,
                                preferred_element_type=F32)
        s_own = jnp.where(causal, s_own, NEG_INF)
        sown_ref[h] = s_own
        m_ref[h] = jnp.maximum(s_own[:, :LANES], s_own[:, LANES:])

    def scores(t, carry):
        off = pl.multiple_of(t * slab, slab)
        for h, (k_ref, _) in enumerate(heads):
            sb = lax.dot_general(q_biased[h], k_ref[pl.ds(off, slab), :], nt,
                                 preferred_element_type=F32)
            s_ref[h, t] = sb
            mx = m_ref[h]
            for u in range(slab // LANES):
                mx = jnp.maximum(mx, sb[:, u * LANES:(u + 1) * LANES])
            m_ref[h] = mx
        return carry

    lax.fori_loop(0, n_groups, scores, 0)

    row_max = []
    for h, (_, v_ref) in enumerate(heads):
        mh = jnp.max(m_ref[h], axis=-1, keepdims=True)
        row_max.append(mh)
        p_own = jnp.exp(sown_ref[h] - mh).astype(BF16)
        acc_ref[h] = jnp.dot(p_own, v_ref[pl.ds(own, MOBA_BLOCK), :], preferred_element_type=F32)

    def weighted_values(t, carry):
        off = pl.multiple_of(t * slab, slab)
        for h, (_, v_ref) in enumerate(heads):
            p = jnp.exp(s_ref[h, t] - row_max[h]).astype(BF16)
            acc_ref[h] += jnp.dot(p, v_ref[pl.ds(off, slab), :], preferred_element_type=F32)
        return carry

    lax.fori_loop(0, n_groups, weighted_values, 0)

    a0 = acc_ref[0]
    a1 = acc_ref[1]
    o0 = a0 / pltpu.roll(a0, HEAD_DIM, 1)
    o1 = a1 / pltpu.roll(a1, HEAD_DIM, 1)
    o_ref[...] = jnp.where(even_data, o0, o1).astype(o_ref.dtype)


def _moba(qb, kbe, kbo, vbe, vbo, kmean, batch):
    t, width = qb.shape
    s = t // batch
    tq = MOBA_BLOCK
    nq = s // tq
    nb = s // MOBA_BLOCK
    assert nb % MOBA_GROUP == 0 and nb <= HEAD_DIM
    km = kmean.reshape(batch, nb, width)
    qblk = lambda b, h, i: (b * nq + i, h)
    kvblk = lambda b, h, i: (b, h)
    return pl.pallas_call(
        _moba_kernel,
        grid=(batch, width // LANES, nq),
        in_specs=[pl.BlockSpec((tq, LANES), qblk)]
        + [pl.BlockSpec((s, LANES), kvblk)] * 4
        + [pl.BlockSpec((None, nb, LANES), lambda b, h, i: (b, 0, h))],
        out_specs=pl.BlockSpec((tq, LANES), qblk),
        out_shape=jax.ShapeDtypeStruct((t, width), BF16),
        scratch_shapes=[pltpu.VMEM((2, nb // MOBA_GROUP, tq, MOBA_GROUP * MOBA_BLOCK), F32),
                        pltpu.VMEM((2, tq, MOBA_BLOCK), F32),
                        pltpu.VMEM((2, tq, LANES), F32),
                        pltpu.VMEM((2, tq, LANES), F32)],
        compiler_params=_params("parallel", "parallel", "arbitrary"),
        name="moba",
    )(qb, kbe, kbo, vbe, vbo, km)


def _out_proj_kernel(*refs, n_in):
    a_refs = refs[:n_in]
    w_ref, x_ref, g_ref, b_ref, o_ref = refs[n_in:]
    y = None
    off = 0
    for a_ref in a_refs:
        kdim = a_ref.shape[1]
        part = jnp.dot(a_ref[...], w_ref[off:off + kdim, :], preferred_element_type=F32)
        y = part if y is None else y + part
        off += kdim
    z = DEEPNORM_ALPHA * x_ref[...] + y
    o_ref[...] = _layer_norm_rows(z, g_ref[...], b_ref[...])


def _out_proj_ln(acts, w_bf16, x2d, gain, bias, tm=512):
    t, d = x2d.shape
    row = lambda i: (i, 0)
    const = lambda i: (0, 0)
    return pl.pallas_call(
        functools.partial(_out_proj_kernel, n_in=len(acts)),
        grid=(t // tm,),
        in_specs=[pl.BlockSpec((tm, a.shape[1]), row) for a in acts]
        + [pl.BlockSpec(w_bf16.shape, const),
           pl.BlockSpec((tm, d), row),
           pl.BlockSpec((1, d), const),
           pl.BlockSpec((1, d), const)],
        out_specs=pl.BlockSpec((tm, d), row),
        out_shape=jax.ShapeDtypeStruct((t, d), F32),
        compiler_params=_params("parallel"),
        name="out_proj_ln",
    )(*acts, w_bf16, x2d, gain.reshape(1, d), bias.reshape(1, d))


def _ffn_kernel(x_ref, w1_ref, w2_ref, g_ref, b_ref, o_ref, xb_ref, acc_ref):
    j = pl.program_id(1)

    @pl.when(j == 0)
    def _():
        xb_ref[...] = x_ref[...].astype(BF16)
        acc_ref[...] = jnp.zeros_like(acc_ref)

    h = jnp.dot(xb_ref[...], w1_ref[...], preferred_element_type=F32)
    h = jnp.square(jnp.maximum(h, 0.0)).astype(BF16)
    acc_ref[...] += jnp.dot(h, w2_ref[...], preferred_element_type=F32)

    @pl.when(j == pl.num_programs(1) - 1)
    def _():
        z = DEEPNORM_ALPHA * x_ref[...] + acc_ref[...]
        o_ref[...] = _layer_norm_rows(z, g_ref[...], b_ref[...])


def _ffn_ln(x2d, w1_bf16, w2_bf16, gain, bias, tm=1024, th=512):
    t, d = x2d.shape
    hdim = w1_bf16.shape[1]
    return pl.pallas_call(
        _ffn_kernel,
        grid=(t // tm, hdim // th),
        in_specs=[pl.BlockSpec((tm, d), lambda i, j: (i, 0)),
                  pl.BlockSpec((d, th), lambda i, j: (0, j)),
                  pl.BlockSpec((th, d), lambda i, j: (j, 0)),
                  pl.BlockSpec((1, d), lambda i, j: (0, 0)),
                  pl.BlockSpec((1, d), lambda i, j: (0, 0))],
        out_specs=pl.BlockSpec((tm, d), lambda i, j: (i, 0)),
        out_shape=jax.ShapeDtypeStruct((t, d), F32),
        scratch_shapes=[pltpu.VMEM((tm, d), BF16), pltpu.VMEM((tm, d), F32)],
        compiler_params=_params("parallel", "arbitrary"),
        name="ffn_ln",
    )(x2d, w1_bf16, w2_bf16, gain.reshape(1, d), bias.reshape(1, d))


def _gmlp_kernel(x_ref, w_ref, b_ref, lng_ref, lnb_ref, ws_ref, bst_ref, o_ref, v_ref):
    tm = x_ref.shape[0]
    gw = o_ref.shape[1]
    c = GMLP_CHUNK
    cw = gw // GMLP_GROUPS
    xb = x_ref[...].astype(BF16)
    v = jnp.dot(xb, w_ref[:, gw:], preferred_element_type=F32) + b_ref[:, gw:]
    v_ref[...] = _layer_norm_rows(jax.nn.gelu(v), lng_ref[...], lnb_ref[...]).astype(BF16)
    r = lax.broadcasted_iota(jnp.int32, (c, c), 0)
    cc = lax.broadcasted_iota(jnp.int32, (c, c), 1)
    causal = cc <= r
    bst = bst_ref[...]
    for g in range(GMLP_GROUPS):
        cols = slice(g * cw, (g + 1) * cw)
        ws = jnp.where(causal, ws_ref[g], 0.0).astype(BF16)
        u = jax.nn.gelu(jnp.dot(xb, w_ref[:, cols], preferred_element_type=F32) + b_ref[:, cols])
        for ci in range(tm // c):
            rows = slice(ci * c, (ci + 1) * c)
            sv = jnp.dot(ws, v_ref[rows, cols], preferred_element_type=F32) + bst[:, g:g + 1]
            o_ref[rows, cols] = (u[rows] * sv).astype(o_ref.dtype)


def _gmlp_gate(x2d, w_uv_bf16, b_uv, ln_g, ln_b, w_s, b_s, tm=512):
    t, d = x2d.shape
    n = w_uv_bf16.shape[1]
    gw = n // 2
    row = lambda i: (i, 0)
    const = lambda i: (0, 0)
    return pl.pallas_call(
        _gmlp_kernel,
        grid=(t // tm,),
        in_specs=[pl.BlockSpec((tm, d), row),
                  pl.BlockSpec((d, n), const),
                  pl.BlockSpec((1, n), const),
                  pl.BlockSpec((1, gw), const),
                  pl.BlockSpec((1, gw), const),
                  pl.BlockSpec(w_s.shape, lambda i: (0, 0, 0)),
                  pl.BlockSpec((GMLP_CHUNK, GMLP_GROUPS), const)],
        out_specs=pl.BlockSpec((tm, gw), row),
        out_shape=jax.ShapeDtypeStruct((t, gw), BF16),
        scratch_shapes=[pltpu.VMEM((tm, gw), BF16)],
        compiler_params=_params("parallel"),
        name="gmlp_gate",
    )(x2d, w_uv_bf16, b_uv.reshape(1, n), ln_g.reshape(1, gw), ln_b.reshape(1, gw), w_s, b_s.T)


def kernel(x, positions, ln_gain, ln_bias, mix_w_in, ret_gn_gain, mix_w_out,
           gmlp_w_uv, gmlp_b_uv, gmlp_ln_gain, gmlp_ln_bias, gmlp_w_s, gmlp_b_s,
           gmlp_w_out, ffn_w_in, ffn_w_out):
    batch, seq, d = x.shape
    depth = ln_gain.shape[0]
    h = x.reshape(batch * seq, d)
    cos_t, sin_t = _rope_tables(positions)
    log_gamma = jnp.log1p(-jnp.exp2(-5.0 - jnp.arange(N_RET_HEADS, dtype=F32)))
    for layer in range(depth):
        i = layer // 2
        if layer % 2 == 0:
            qa, ka, va, ga, qb, kbe, kbo, vbe, vbo, kmean = _in_proj(
                h, mix_w_in[i].astype(BF16), cos_t, sin_t, seq // MOBA_BLOCK)
            ya = _retention(qa, ka, va, ga, ret_gn_gain[i], log_gamma, batch)
            yb = _moba(qb, kbe, kbo, vbe, vbo, kmean, batch)
            h = _out_proj_ln([ya, yb], mix_w_out[i].astype(BF16), h,
                             ln_gain[layer, 0], ln_bias[layer, 0])
        else:
            gated = _gmlp_gate(h, gmlp_w_uv[i].astype(BF16), gmlp_b_uv[i], gmlp_ln_gain[i],
                               gmlp_ln_bias[i], gmlp_w_s[i], gmlp_b_s[i])
            h = _out_proj_ln([gated], gmlp_w_out[i].astype(BF16), h,
                             ln_gain[layer, 0], ln_bias[layer, 0])
        h = _ffn_ln(h, ffn_w_in[layer].astype(BF16), ffn_w_out[layer].astype(BF16),
                    ln_gain[layer, 1], ln_bias[layer, 1])
    return h.reshape(batch, seq, d)
```

```python
import functools

import jax
import jax.numpy as jnp
from jax import lax
from jax.experimental import pallas as pl
from jax.experimental.pallas import tpu as pltpu

HEAD_DIM = 64
N_RET_HEADS = 8
N_MOBA_HEADS = 8
RET_WIDTH = N_RET_HEADS * HEAD_DIM
MOBA_WIDTH = N_MOBA_HEADS * HEAD_DIM
RET_CHUNK = 128
MOBA_BLOCK = 256
MOBA_TOPK = 3
GMLP_CHUNK = 128
GMLP_GROUPS = 8
ROPE_THETA = 10000.0
LN_EPS = 1e-5
DEPTH = 4
DEEPNORM_ALPHA = (2 * DEPTH) ** 0.25

LANES = 128
VMEM_LIMIT_BYTES = 52 * 1024 * 1024

BF16 = jnp.bfloat16
F32 = jnp.float32
NEG_INF = float("-inf")
MOBA_GROUP = 4
MASK_BIAS = -1e30


def _params(*sem):
    return pltpu.CompilerParams(dimension_semantics=sem, vmem_limit_bytes=VMEM_LIMIT_BYTES)


def _layer_norm_rows(z, g, b):
    mu = jnp.mean(z, axis=-1, keepdims=True)
    d = z - mu
    var = jnp.mean(d * d, axis=-1, keepdims=True)
    return d * lax.rsqrt(var + LN_EPS) * g + b


def _rope_table_kernel(pos_ref, freq_ref, cos_ref, sin_ref):
    ang = pos_ref[...] * freq_ref[...]
    lane = lax.broadcasted_iota(jnp.int32, ang.shape, 1)
    first_half = (lane % HEAD_DIM) < (HEAD_DIM // 2)
    cos_ref[...] = jnp.cos(ang)
    sin_ref[...] = jnp.where(first_half, -jnp.sin(ang), jnp.sin(ang))


def _rope_tables(positions):
    t = positions.size
    tm = min(t, 2048)
    pos = positions.reshape(t, 1).astype(F32)
    inv_freq = ROPE_THETA ** (-jnp.arange(0, HEAD_DIM, 2, dtype=F32) / HEAD_DIM)
    freq = jnp.tile(inv_freq, LANES // (HEAD_DIM // 2)).reshape(1, LANES)
    return pl.pallas_call(
        _rope_table_kernel,
        grid=(t // tm,),
        in_specs=[pl.BlockSpec((tm, 1), lambda i: (i, 0)),
                  pl.BlockSpec((1, LANES), lambda i: (0, 0))],
        out_specs=[pl.BlockSpec((tm, LANES), lambda i: (i, 0))] * 2,
        out_shape=[jax.ShapeDtypeStruct((t, LANES), F32)] * 2,
        compiler_params=_params("parallel"),
        name="rope_tables",
    )(pos, freq)


def _in_proj_kernel(x_ref, w_ref, cos_ref, sin_ref,
                    qa_ref, ka_ref, va_ref, ga_ref, qb_ref, kbe_ref, kbo_ref, vbe_ref, vbo_ref,
                    km_ref, *, n_blocks):
    tm = x_ref.shape[0]
    width = RET_WIDTH
    xb = x_ref[...].astype(BF16)
    reps = width // LANES
    cos = jnp.concatenate([cos_ref[...]] * reps, axis=1)
    sin = jnp.concatenate([sin_ref[...]] * reps, axis=1)
    lane = lax.broadcasted_iota(jnp.int32, (tm, width), 1)
    first_half = (lane % HEAD_DIM) < (HEAD_DIM // 2)

    def proj(g):
        return jnp.dot(xb, w_ref[:, g * width:(g + 1) * width], preferred_element_type=F32)

    def rope(t):
        partner = jnp.where(first_half,
                            pltpu.roll(t, width - HEAD_DIM // 2, 1),
                            pltpu.roll(t, HEAD_DIM // 2, 1))
        return t * cos + partner * sin

    qa_ref[...] = rope(proj(0)).astype(qa_ref.dtype)
    ka_ref[...] = rope(proj(1)).astype(ka_ref.dtype)
    va_ref[...] = proj(2).astype(va_ref.dtype)
    ga_ref[...] = proj(3).astype(ga_ref.dtype)
    qb_ref[...] = rope(proj(4)).astype(qb_ref.dtype)
    kb = rope(proj(5))
    vb = proj(6)
    pair_lane = lane % LANES
    even_data = pair_lane < HEAD_DIM
    row = lax.broadcasted_iota(jnp.int32, (tm, width), 0)
    block = ((pl.program_id(0) * tm + row) // MOBA_BLOCK) % n_blocks
    onehot_e = (pair_lane - HEAD_DIM == block).astype(F32)
    onehot_o = (pair_lane == block).astype(F32)
    kbe_ref[...] = jnp.where(even_data, kb, onehot_e).astype(kbe_ref.dtype)
    kbo_ref[...] = jnp.where(even_data, onehot_o, kb).astype(kbo_ref.dtype)
    vbe_ref[...] = jnp.where(even_data, vb, 1.0).astype(vbe_ref.dtype)
    vbo_ref[...] = jnp.where(even_data, 1.0, vb).astype(vbo_ref.dtype)
    for blk in range(tm // MOBA_BLOCK):
        km_ref[blk] = jnp.mean(kb[blk * MOBA_BLOCK:(blk + 1) * MOBA_BLOCK], axis=0, keepdims=True)


def _in_proj(x2d, w_bf16, cos_t, sin_t, n_blocks, tm=512):
    t, d = x2d.shape
    n = w_bf16.shape[1]
    width = RET_WIDTH
    assert n_blocks <= HEAD_DIM and tm % MOBA_BLOCK == 0
    row = lambda i: (i, 0)
    out_dtypes = [BF16, F32, BF16, F32, F32, BF16, BF16, BF16, BF16]
    return pl.pallas_call(
        functools.partial(_in_proj_kernel, n_blocks=n_blocks),
        grid=(t // tm,),
        in_specs=[pl.BlockSpec((tm, d), row),
                  pl.BlockSpec((d, n), lambda i: (0, 0)),
                  pl.BlockSpec((tm, LANES), row),
                  pl.BlockSpec((tm, LANES), row)],
        out_specs=[pl.BlockSpec((tm, width), row)] * len(out_dtypes)
        + [pl.BlockSpec((tm // MOBA_BLOCK, 1, width), lambda i: (i, 0, 0))],
        out_shape=[jax.ShapeDtypeStruct((t, width), dt) for dt in out_dtypes]
        + [jax.ShapeDtypeStruct((t // MOBA_BLOCK, 1, width), F32)],
        compiler_params=_params("parallel"),
        name="in_proj",
    )(x2d, w_bf16, cos_t, sin_t)


def _retention_kernel(lg_ref, q_ref, k_ref, v_ref, g_ref, gain_ref, o_ref, state_ref):
    hp = pl.program_id(1)
    c = RET_CHUNK
    tm = q_ref.shape[0]

    @pl.when(pl.program_id(2) == 0)
    def _():
        state_ref[...] = jnp.zeros_like(state_ref)

    lg0 = lg_ref[2 * hp]
    lg1 = lg_ref[2 * hp + 1]
    lane = lax.broadcasted_iota(jnp.int32, (c, LANES), 1)
    row = lax.broadcasted_iota(jnp.int32, (c, LANES), 0)
    head0 = lane < HEAD_DIM
    lg_lane = jnp.where(head0, lg0, lg1)
    idx = row.astype(F32)
    xi = jnp.exp(lg_lane * (idx + 1.0))
    zeta = jnp.exp(lg_lane * (c - 1.0 - idx))
    chunk_decay = jnp.exp(lg_lane * float(c))
    diff = (row - lane).astype(F32)
    dpos = jnp.maximum(diff, 0.0)
    decay0 = jnp.where(diff >= 0, jnp.exp(lg0 * dpos), 0.0)
    decay1 = jnp.where(diff >= 0, jnp.exp(lg1 * dpos), 0.0)
    same_head = (row < HEAD_DIM) == head0
    gain = gain_ref[...]
    nt = (((1,), (1,)), ((), ()))
    tn = (((0,), (0,)), ((), ()))

    for ci in range(tm // c):
        sl = slice(ci * c, (ci + 1) * c)
        q = q_ref[sl, :]
        kf = k_ref[sl, :] * (HEAD_DIM ** -0.5)
        v = v_ref[sl, :]
        kb = kf.astype(BF16)
        zero = jnp.zeros_like(q)
        q0 = jnp.where(head0, q, zero)
        q1 = jnp.where(head0, zero, q)
        s0 = lax.dot_general(q0, kb, nt, preferred_element_type=F32) * decay0
        s1 = lax.dot_general(q1, kb, nt, preferred_element_type=F32) * decay1
        inner = jnp.where(head0,
                          jnp.dot(s0.astype(BF16), v, preferred_element_type=F32),
                          jnp.dot(s1.astype(BF16), v, preferred_element_type=F32))
        state = state_ref[...]
        cross = jnp.dot(q, state.astype(BF16), preferred_element_type=F32) * xi
        kv = lax.dot_general((kf * zeta).astype(BF16), v, tn, preferred_element_type=F32)
        state_ref[...] = state * chunk_decay + jnp.where(same_head, kv, 0.0)

        y = inner + cross
        inv = 1.0 / HEAD_DIM
        mu = jnp.where(head0,
                       jnp.sum(jnp.where(head0, y, 0.0), axis=-1, keepdims=True),
                       jnp.sum(jnp.where(head0, 0.0, y), axis=-1, keepdims=True)) * inv
        d = y - mu
        dd = d * d
        var = jnp.where(head0,
                        jnp.sum(jnp.where(head0, dd, 0.0), axis=-1, keepdims=True),
                        jnp.sum(jnp.where(head0, 0.0, dd), axis=-1, keepdims=True)) * inv
        yn = d * lax.rsqrt(var + LN_EPS) * gain
        o_ref[sl, :] = (yn * jax.nn.silu(g_ref[sl, :])).astype(o_ref.dtype)


def _retention(qa, ka, va, ga, gn_gain, log_gamma, batch, tm=512):
    t, width = qa.shape
    s = t // batch
    nt = s // tm
    blk = lambda b, h, i: (b * nt + i, h)
    return pl.pallas_call(
        _retention_kernel,
        grid=(batch, width // LANES, nt),
        in_specs=[pl.BlockSpec(memory_space=pltpu.SMEM),
                  pl.BlockSpec((tm, LANES), blk),
                  pl.BlockSpec((tm, LANES), blk),
                  pl.BlockSpec((tm, LANES), blk),
                  pl.BlockSpec((tm, LANES), blk),
                  pl.BlockSpec((1, LANES), lambda b, h, i: (0, h))],
        out_specs=pl.BlockSpec((tm, LANES), blk),
        out_shape=jax.ShapeDtypeStruct((t, width), BF16),
        scratch_shapes=[pltpu.VMEM((LANES, LANES), F32)],
        compiler_params=_params("parallel", "parallel", "arbitrary"),
        name="retention",
    )(log_gamma, qa, ka, va, ga, gn_gain.reshape(1, width))


def _moba_select_kernel(q_ref, km_ref, qe_ref, qo_ref):
    ts = q_ref.shape[0]
    nb = km_ref.shape[0]
    qf = q_ref[...]
    km = km_ref[...]
    lane = lax.broadcasted_iota(jnp.int32, (ts, LANES), 1)
    even_data = lane < HEAD_DIM
    nt = (((1,), (1,)), ((), ()))
    blk = lax.broadcasted_iota(jnp.int32, (nb, ts), 0).astype(F32)
    col = lax.broadcasted_iota(jnp.int32, (nb, ts), 1)
    own = ((pl.program_id(2) * ts + col) // MOBA_BLOCK).astype(F32)
    past = blk < own
    place_row = lax.broadcasted_iota(jnp.int32, (nb, LANES), 0)
    place_lane = lax.broadcasted_iota(jnp.int32, (nb, LANES), 1)

    for h, out_ref in enumerate((qe_ref, qo_ref)):
        data = even_data if h == 0 else jnp.logical_not(even_data)
        qh = jnp.where(data, qf, 0.0)
        gate = lax.dot_general(km, qh, nt, preferred_element_type=F32,
                               precision=lax.Precision.HIGHEST)
        g = jnp.where(past, gate, NEG_INF)
        sel = jnp.zeros((nb, ts), F32)
        for _ in range(min(MOBA_TOPK, nb)):
            m = jnp.max(g, axis=0, keepdims=True)
            first = jnp.min(jnp.where(g == m, blk, float(nb)), axis=0, keepdims=True)
            pick = blk == first
            sel = jnp.where(pick, 1.0, sel)
            g = jnp.where(pick, NEG_INF, g)
        sel = jnp.where(past, sel, 0.0)
        sel = jnp.where(blk == own, 1.0, sel)
        spare_base = HEAD_DIM if h == 0 else 0
        place = (place_lane == place_row + spare_base).astype(BF16)
        placed = lax.dot_general(sel.astype(BF16), place, (((0,), (0,)), ((), ())),
                                 preferred_element_type=F32)
        bias = (1.0 - placed) * MASK_BIAS
        out_ref[...] = jnp.where(data, qh * (HEAD_DIM ** -0.5), bias).astype(out_ref.dtype)


def _moba_select(qb, kmean, batch, ts=1024):
    t, width = qb.shape
    s = t // batch
    nb = s // MOBA_BLOCK
    ts = min(ts, s)
    assert nb <= HEAD_DIM and ts % MOBA_BLOCK == 0
    km = kmean.reshape(batch, nb, width)
    qblk = lambda b, h, i: (b * (s // ts) + i, h)
    return pl.pallas_call(
        _moba_select_kernel,
        grid=(batch, width // LANES, s // ts),
        in_specs=[pl.BlockSpec((ts, LANES), qblk),
                  pl.BlockSpec((None, nb, LANES), lambda b, h, i: (b, 0, h))],
        out_specs=[pl.BlockSpec((ts, LANES), qblk)] * 2,
        out_shape=[jax.ShapeDtypeStruct((t, width), BF16)] * 2,
        compiler_params=_params("parallel", "parallel", "parallel"),
        name="moba_select",
    )(qb, km)


def _moba_kernel(qe_ref, qo_ref, ke_ref, ko_ref, ve_ref, vo_ref, o_ref, s_ref, m_ref, acc_ref):
    qi = pl.program_id(2)
    tq = qe_ref.shape[0]
    slab = MOBA_GROUP * MOBA_BLOCK
    nt = (((1,), (1,)), ((), ()))
    last = qi // MOBA_GROUP
    heads = ((qe_ref, ke_ref, ve_ref), (qo_ref, ko_ref, vo_ref))
    r = lax.broadcasted_iota(jnp.int32, (tq, slab), 0)
    c = lax.broadcasted_iota(jnp.int32, (tq, slab), 1)
    causal = c - r <= (qi - last * MOBA_GROUP) * MOBA_BLOCK

    def scores(t, masked):
        off = pl.multiple_of(t * slab, slab)
        for h, (q_ref, k_ref, _) in enumerate(heads):
            sb = lax.dot_general(q_ref[...], k_ref[pl.ds(off, slab), :], nt,
                                 preferred_element_type=F32)
            if masked:
                sb = jnp.where(causal, sb, NEG_INF)
                mx = sb[:, :LANES]
                first = 1
            else:
                mx = m_ref[h]
                first = 0
            s_ref[h, t] = sb
            for u in range(first, slab // LANES):
                mx = jnp.maximum(mx, sb[:, u * LANES:(u + 1) * LANES])
            m_ref[h] = mx

    def weighted_values(t, first):
        off = pl.multiple_of(t * slab, slab)
        for h, (_, _, v_ref) in enumerate(heads):
            p = jnp.exp(s_ref[h, t] - row_max[h]).astype(BF16)
            pv = jnp.dot(p, v_ref[pl.ds(off, slab), :], preferred_element_type=F32)
            acc_ref[h] = pv if first else acc_ref[h] + pv

    scores(last, True)
    lax.fori_loop(0, last, lambda t, carry: (scores(t, False), carry)[1], 0)
    row_max = [jnp.max(m_ref[h], axis=-1, keepdims=True) for h in range(2)]
    weighted_values(last, True)
    lax.fori_loop(0, last, lambda t, carry: (weighted_values(t, False), carry)[1], 0)

    lane = lax.broadcasted_iota(jnp.int32, (tq, LANES), 1)
    a0 = acc_ref[0]
    a1 = acc_ref[1]
    o0 = a0 / pltpu.roll(a0, HEAD_DIM, 1)
    o1 = a1 / pltpu.roll(a1, HEAD_DIM, 1)
    o_ref[...] = jnp.where(lane < HEAD_DIM, o0, o1).astype(o_ref.dtype)


def _moba(qbe, qbo, kbe, kbo, vbe, vbo, batch):
    t, width = qbe.shape
    s = t // batch
    tq = MOBA_BLOCK
    nq = s // tq
    nb = s // MOBA_BLOCK
    assert nb % MOBA_GROUP == 0
    qblk = lambda b, h, i: (b * nq + i, h)
    kvblk = lambda b, h, i: (b, h)
    return pl.pallas_call(
        _moba_kernel,
        grid=(batch, width // LANES, nq),
        in_specs=[pl.BlockSpec((tq, LANES), qblk)] * 2 + [pl.BlockSpec((s, LANES), kvblk)] * 4,
        out_specs=pl.BlockSpec((tq, LANES), qblk),
        out_shape=jax.ShapeDtypeStruct((t, width), BF16),
        scratch_shapes=[pltpu.VMEM((2, nb // MOBA_GROUP, tq, MOBA_GROUP * MOBA_BLOCK), F32),
                        pltpu.VMEM((2, tq, LANES), F32),
                        pltpu.VMEM((2, tq, LANES), F32)],
        compiler_params=_params("parallel", "parallel", "arbitrary"),
        name="moba",
    )(qbe, qbo, kbe, kbo, vbe, vbo)


def _out_proj_kernel(*refs, n_in):
    a_refs = refs[:n_in]
    w_ref, x_ref, g_ref, b_ref, o_ref = refs[n_in:]
    y = None
    off = 0
    for a_ref in a_refs:
        kdim = a_ref.shape[1]
        part = jnp.dot(a_ref[...], w_ref[off:off + kdim, :], preferred_element_type=F32)
        y = part if y is None else y + part
        off += kdim
    z = DEEPNORM_ALPHA * x_ref[...] + y
    o_ref[...] = _layer_norm_rows(z, g_ref[...], b_ref[...])


def _out_proj_ln(acts, w_bf16, x2d, gain, bias, tm=512):
    t, d = x2d.shape
    row = lambda i: (i, 0)
    const = lambda i: (0, 0)
    return pl.pallas_call(
        functools.partial(_out_proj_kernel, n_in=len(acts)),
        grid=(t // tm,),
        in_specs=[pl.BlockSpec((tm, a.shape[1]), row) for a in acts]
        + [pl.BlockSpec(w_bf16.shape, const),
           pl.BlockSpec((tm, d), row),
           pl.BlockSpec((1, d), const),
           pl.BlockSpec((1, d), const)],
        out_specs=pl.BlockSpec((tm, d), row),
        out_shape=jax.ShapeDtypeStruct((t, d), F32),
        compiler_params=_params("parallel"),
        name="out_proj_ln",
    )(*acts, w_bf16, x2d, gain.reshape(1, d), bias.reshape(1, d))


def _ffn_kernel(x_ref, w1_ref, w2_ref, g_ref, b_ref, o_ref, xb_ref, acc_ref):
    j = pl.program_id(1)

    @pl.when(j == 0)
    def _():
        xb_ref[...] = x_ref[...].astype(BF16)
        acc_ref[...] = jnp.zeros_like(acc_ref)

    h = jnp.dot(xb_ref[...], w1_ref[...], preferred_element_type=F32)
    h = jnp.square(jnp.maximum(h, 0.0)).astype(BF16)
    acc_ref[...] += jnp.dot(h, w2_ref[...], preferred_element_type=F32)

    @pl.when(j == pl.num_programs(1) - 1)
    def _():
        z = DEEPNORM_ALPHA * x_ref[...] + acc_ref[...]
        o_ref[...] = _layer_norm_rows(z, g_ref[...], b_ref[...])


def _ffn_ln(x2d, w1_bf16, w2_bf16, gain, bias, tm=1024, th=512):
    t, d = x2d.shape
    hdim = w1_bf16.shape[1]
    return pl.pallas_call(
        _ffn_kernel,
        grid=(t // tm, hdim // th),
        in_specs=[pl.BlockSpec((tm, d), lambda i, j: (i, 0)),
                  pl.BlockSpec((d, th), lambda i, j: (0, j)),
                  pl.BlockSpec((th, d), lambda i, j: (j, 0)),
                  pl.BlockSpec((1, d), lambda i, j: (0, 0)),
                  pl.BlockSpec((1, d), lambda i, j: (0, 0))],
        out_specs=pl.BlockSpec((tm, d), lambda i, j: (i, 0)),
        out_shape=jax.ShapeDtypeStruct((t, d), F32),
        scratch_shapes=[pltpu.VMEM((tm, d), BF16), pltpu.VMEM((tm, d), F32)],
        compiler_params=_params("parallel", "arbitrary"),
        name="ffn_ln",
    )(x2d, w1_bf16, w2_bf16, gain.reshape(1, d), bias.reshape(1, d))


def _gmlp_kernel(x_ref, w_ref, b_ref, lng_ref, lnb_ref, ws_ref, bst_ref, o_ref, v_ref):
    tm = x_ref.shape[0]
    gw = o_ref.shape[1]
    c = GMLP_CHUNK
    cw = gw // GMLP_GROUPS
    xb = x_ref[...].astype(BF16)
    v = jnp.dot(xb, w_ref[:, gw:], preferred_element_type=F32) + b_ref[:, gw:]
    v_ref[...] = _layer_norm_rows(jax.nn.gelu(v), lng_ref[...], lnb_ref[...]).astype(BF16)
    r = lax.broadcasted_iota(jnp.int32, (c, c), 0)
    cc = lax.broadcasted_iota(jnp.int32, (c, c), 1)
    causal = cc <= r
    bst = bst_ref[...]
    for g in range(GMLP_GROUPS):
        cols = slice(g * cw, (g + 1) * cw)
        ws = jnp.where(causal, ws_ref[g], 0.0).astype(BF16)
        u = jax.nn.gelu(jnp.dot(xb, w_ref[:, cols], preferred_element_type=F32) + b_ref[:, cols])
        for ci in range(tm // c):
            rows = slice(ci * c, (ci + 1) * c)
            sv = jnp.dot(ws, v_ref[rows, cols], preferred_element_type=F32) + bst[:, g:g + 1]
            o_ref[rows, cols] = (u[rows] * sv).astype(o_ref.dtype)


def _gmlp_gate(x2d, w_uv_bf16, b_uv, ln_g, ln_b, w_s, b_s, tm=512):
    t, d = x2d.shape
    n = w_uv_bf16.shape[1]
    gw = n // 2
    row = lambda i: (i, 0)
    const = lambda i: (0, 0)
    return pl.pallas_call(
        _gmlp_kernel,
        grid=(t // tm,),
        in_specs=[pl.BlockSpec((tm, d), row),
                  pl.BlockSpec((d, n), const),
                  pl.BlockSpec((1, n), const),
                  pl.BlockSpec((1, gw), const),
                  pl.BlockSpec((1, gw), const),
                  pl.BlockSpec(w_s.shape, lambda i: (0, 0, 0)),
                  pl.BlockSpec((GMLP_CHUNK, GMLP_GROUPS), const)],
        out_specs=pl.BlockSpec((tm, gw), row),
        out_shape=jax.ShapeDtypeStruct((t, gw), BF16),
        scratch_shapes=[pltpu.VMEM((tm, gw), BF16)],
        compiler_params=_params("parallel"),
        name="gmlp_gate",
    )(x2d, w_uv_bf16, b_uv.reshape(1, n), ln_g.reshape(1, gw), ln_b.reshape(1, gw), w_s, b_s.T)


def kernel(x, positions, ln_gain, ln_bias, mix_w_in, ret_gn_gain, mix_w_out,
           gmlp_w_uv, gmlp_b_uv, gmlp_ln_gain, gmlp_ln_bias, gmlp_w_s, gmlp_b_s,
           gmlp_w_out, ffn_w_in, ffn_w_out):
    batch, seq, d = x.shape
    depth = ln_gain.shape[0]
    h = x.reshape(batch * seq, d)
    cos_t, sin_t = _rope_tables(positions)
    log_gamma = jnp.log1p(-jnp.exp2(-5.0 - jnp.arange(N_RET_HEADS, dtype=F32)))
    for layer in range(depth):
        i = layer // 2
        if layer % 2 == 0:
            qa, ka, va, ga, qb, kbe, kbo, vbe, vbo, kmean = _in_proj(
                h, mix_w_in[i].astype(BF16), cos_t, sin_t, seq // MOBA_BLOCK)
            ya = _retention(qa, ka, va, ga, ret_gn_gain[i], log_gamma, batch)
            qbe, qbo = _moba_select(qb, kmean, batch)
            yb = _moba(qbe, qbo, kbe, kbo, vbe, vbo, batch)
            h = _out_proj_ln([ya, yb], mix_w_out[i].astype(BF16), h,
                             ln_gain[layer, 0], ln_bias[layer, 0])
        else:
            gated = _gmlp_gate(h, gmlp_w_uv[i].astype(BF16), gmlp_b_uv[i], gmlp_ln_gain[i],
                               gmlp_ln_bias[i], gmlp_w_s[i], gmlp_b_s[i])
            h = _out_proj_ln([gated], gmlp_w_out[i].astype(BF16), h,
                             ln_gain[layer, 0], ln_bias[layer, 0])
        h = _ffn_ln(h, ffn_w_in[layer].astype(BF16), ffn_w_out[layer].astype(BF16),
                    ln_gain[layer, 1], ln_bias[layer, 1])
    return h.reshape(batch, seq, d)
```

```python
import functools

import jax
import jax.numpy as jnp
from jax import lax
from jax.experimental import pallas as pl
from jax.experimental.pallas import tpu as pltpu

HEAD_DIM = 64
N_RET_HEADS = 8
N_MOBA_HEADS = 8
RET_WIDTH = N_RET_HEADS * HEAD_DIM
MOBA_WIDTH = N_MOBA_HEADS * HEAD_DIM
RET_CHUNK = 128
MOBA_BLOCK = 256
MOBA_TOPK = 3
GMLP_CHUNK = 128
GMLP_GROUPS = 8
ROPE_THETA = 10000.0
LN_EPS = 1e-5
DEPTH = 4
DEEPNORM_ALPHA = (2 * DEPTH) ** 0.25

LANES = 128
VMEM_LIMIT_BYTES = 52 * 1024 * 1024

BF16 = jnp.bfloat16
F32 = jnp.float32
NEG_INF = float("-inf")
MOBA_GROUP = 4
MASK_BIAS = -1e30
LOG2_E = 1.4426950408889634


def _params(*sem):
    return pltpu.CompilerParams(dimension_semantics=sem, vmem_limit_bytes=VMEM_LIMIT_BYTES)


def _layer_norm_rows(z, g, b):
    mu = jnp.mean(z, axis=-1, keepdims=True)
    d = z - mu
    var = jnp.mean(d * d, axis=-1, keepdims=True)
    return d * lax.rsqrt(var + LN_EPS) * g + b


def _rope_table_kernel(pos_ref, freq_ref, cos_ref, sin_ref):
    ang = pos_ref[...] * freq_ref[...]
    lane = lax.broadcasted_iota(jnp.int32, ang.shape, 1)
    first_half = (lane % HEAD_DIM) < (HEAD_DIM // 2)
    cos_ref[...] = jnp.cos(ang)
    sin_ref[...] = jnp.where(first_half, -jnp.sin(ang), jnp.sin(ang))


def _rope_tables(positions):
    t = positions.size
    tm = min(t, 2048)
    pos = positions.reshape(t, 1).astype(F32)
    inv_freq = ROPE_THETA ** (-jnp.arange(0, HEAD_DIM, 2, dtype=F32) / HEAD_DIM)
    freq = jnp.tile(inv_freq, LANES // (HEAD_DIM // 2)).reshape(1, LANES)
    return pl.pallas_call(
        _rope_table_kernel,
        grid=(t // tm,),
        in_specs=[pl.BlockSpec((tm, 1), lambda i: (i, 0)),
                  pl.BlockSpec((1, LANES), lambda i: (0, 0))],
        out_specs=[pl.BlockSpec((tm, LANES), lambda i: (i, 0))] * 2,
        out_shape=[jax.ShapeDtypeStruct((t, LANES), F32)] * 2,
        compiler_params=_params("parallel"),
        name="rope_tables",
    )(pos, freq)


def _in_proj_kernel(x_ref, w_ref, cos_ref, sin_ref,
                    qa_ref, ka_ref, va_ref, ga_ref, qb_ref, kbe_ref, kbo_ref, vbe_ref, vbo_ref,
                    km_ref, *, n_blocks):
    tm = x_ref.shape[0]
    width = RET_WIDTH
    xb = x_ref[...].astype(BF16)
    reps = width // LANES
    cos = jnp.concatenate([cos_ref[...]] * reps, axis=1)
    sin = jnp.concatenate([sin_ref[...]] * reps, axis=1)
    lane = lax.broadcasted_iota(jnp.int32, (tm, width), 1)
    first_half = (lane % HEAD_DIM) < (HEAD_DIM // 2)

    def proj(g):
        return jnp.dot(xb, w_ref[:, g * width:(g + 1) * width], preferred_element_type=F32)

    def rope(t):
        partner = jnp.where(first_half,
                            pltpu.roll(t, width - HEAD_DIM // 2, 1),
                            pltpu.roll(t, HEAD_DIM // 2, 1))
        return t * cos + partner * sin

    qa_ref[...] = rope(proj(0)).astype(qa_ref.dtype)
    ka_ref[...] = rope(proj(1)).astype(ka_ref.dtype)
    va_ref[...] = proj(2).astype(va_ref.dtype)
    ga_ref[...] = proj(3).astype(ga_ref.dtype)
    qb_ref[...] = rope(proj(4)).astype(qb_ref.dtype)
    kb = rope(proj(5))
    vb = proj(6)
    pair_lane = lane % LANES
    even_data = pair_lane < HEAD_DIM
    row = lax.broadcasted_iota(jnp.int32, (tm, width), 0)
    block = ((pl.program_id(0) * tm + row) // MOBA_BLOCK) % n_blocks
    onehot_e = (pair_lane - HEAD_DIM == block).astype(F32)
    onehot_o = (pair_lane == block).astype(F32)
    kbe_ref[...] = jnp.where(even_data, kb, onehot_e).astype(kbe_ref.dtype)
    kbo_ref[...] = jnp.where(even_data, onehot_o, kb).astype(kbo_ref.dtype)
    vbe_ref[...] = jnp.where(even_data, vb, 1.0).astype(vbe_ref.dtype)
    vbo_ref[...] = jnp.where(even_data, 1.0, vb).astype(vbo_ref.dtype)
    for blk in range(tm // MOBA_BLOCK):
        km_ref[blk] = jnp.mean(kb[blk * MOBA_BLOCK:(blk + 1) * MOBA_BLOCK], axis=0, keepdims=True)


def _in_proj(x2d, w_bf16, cos_t, sin_t, n_blocks, tm=512):
    t, d = x2d.shape
    n = w_bf16.shape[1]
    width = RET_WIDTH
    assert n_blocks <= HEAD_DIM and tm % MOBA_BLOCK == 0
    row = lambda i: (i, 0)
    out_dtypes = [BF16, F32, BF16, F32, F32, BF16, BF16, BF16, BF16]
    return pl.pallas_call(
        functools.partial(_in_proj_kernel, n_blocks=n_blocks),
        grid=(t // tm,),
        in_specs=[pl.BlockSpec((tm, d), row),
                  pl.BlockSpec((d, n), lambda i: (0, 0)),
                  pl.BlockSpec((tm, LANES), row),
                  pl.BlockSpec((tm, LANES), row)],
        out_specs=[pl.BlockSpec((tm, width), row)] * len(out_dtypes)
        + [pl.BlockSpec((tm // MOBA_BLOCK, 1, width), lambda i: (i, 0, 0))],
        out_shape=[jax.ShapeDtypeStruct((t, width), dt) for dt in out_dtypes]
        + [jax.ShapeDtypeStruct((t // MOBA_BLOCK, 1, width), F32)],
        compiler_params=_params("parallel"),
        name="in_proj",
    )(x2d, w_bf16, cos_t, sin_t)


def _retention_kernel(lg_ref, q_ref, k_ref, v_ref, g_ref, gain_ref, o_ref, state_ref):
    hp = pl.program_id(1)
    c = RET_CHUNK
    tm = q_ref.shape[0]

    @pl.when(pl.program_id(2) == 0)
    def _():
        state_ref[...] = jnp.zeros_like(state_ref)

    lg0 = lg_ref[2 * hp]
    lg1 = lg_ref[2 * hp + 1]
    lane = lax.broadcasted_iota(jnp.int32, (c, LANES), 1)
    row = lax.broadcasted_iota(jnp.int32, (c, LANES), 0)
    head0 = lane < HEAD_DIM
    lg_lane = jnp.where(head0, lg0, lg1)
    idx = row.astype(F32)
    xi = jnp.exp(lg_lane * (idx + 1.0))
    zeta = jnp.exp(lg_lane * (c - 1.0 - idx))
    chunk_decay = jnp.exp(lg_lane * float(c))
    diff = (row - lane).astype(F32)
    dpos = jnp.maximum(diff, 0.0)
    decay0 = jnp.where(diff >= 0, jnp.exp(lg0 * dpos), 0.0)
    decay1 = jnp.where(diff >= 0, jnp.exp(lg1 * dpos), 0.0)
    same_head = (row < HEAD_DIM) == head0
    gain = gain_ref[...]
    nt = (((1,), (1,)), ((), ()))
    tn = (((0,), (0,)), ((), ()))

    for ci in range(tm // c):
        sl = slice(ci * c, (ci + 1) * c)
        q = q_ref[sl, :]
        kf = k_ref[sl, :] * (HEAD_DIM ** -0.5)
        v = v_ref[sl, :]
        kb = kf.astype(BF16)
        zero = jnp.zeros_like(q)
        q0 = jnp.where(head0, q, zero)
        q1 = jnp.where(head0, zero, q)
        s0 = lax.dot_general(q0, kb, nt, preferred_element_type=F32) * decay0
        s1 = lax.dot_general(q1, kb, nt, preferred_element_type=F32) * decay1
        inner = jnp.where(head0,
                          jnp.dot(s0.astype(BF16), v, preferred_element_type=F32),
                          jnp.dot(s1.astype(BF16), v, preferred_element_type=F32))
        state = state_ref[...]
        cross = jnp.dot(q, state.astype(BF16), preferred_element_type=F32) * xi
        kv = lax.dot_general((kf * zeta).astype(BF16), v, tn, preferred_element_type=F32)
        state_ref[...] = state * chunk_decay + jnp.where(same_head, kv, 0.0)

        y = inner + cross
        inv = 1.0 / HEAD_DIM
        mu = jnp.where(head0,
                       jnp.sum(jnp.where(head0, y, 0.0), axis=-1, keepdims=True),
                       jnp.sum(jnp.where(head0, 0.0, y), axis=-1, keepdims=True)) * inv
        d = y - mu
        dd = d * d
        var = jnp.where(head0,
                        jnp.sum(jnp.where(head0, dd, 0.0), axis=-1, keepdims=True),
                        jnp.sum(jnp.where(head0, 0.0, dd), axis=-1, keepdims=True)) * inv
        yn = d * lax.rsqrt(var + LN_EPS) * gain
        o_ref[sl, :] = (yn * jax.nn.silu(g_ref[sl, :])).astype(o_ref.dtype)


def _retention(qa, ka, va, ga, gn_gain, log_gamma, batch, tm=512):
    t, width = qa.shape
    s = t // batch
    nt = s // tm
    blk = lambda b, h, i: (b * nt + i, h)
    return pl.pallas_call(
        _retention_kernel,
        grid=(batch, width // LANES, nt),
        in_specs=[pl.BlockSpec(memory_space=pltpu.SMEM),
                  pl.BlockSpec((tm, LANES), blk),
                  pl.BlockSpec((tm, LANES), blk),
                  pl.BlockSpec((tm, LANES), blk),
                  pl.BlockSpec((tm, LANES), blk),
                  pl.BlockSpec((1, LANES), lambda b, h, i: (0, h))],
        out_specs=pl.BlockSpec((tm, LANES), blk),
        out_shape=jax.ShapeDtypeStruct((t, width), BF16),
        scratch_shapes=[pltpu.VMEM((LANES, LANES), F32)],
        compiler_params=_params("parallel", "parallel", "arbitrary"),
        name="retention",
    )(log_gamma, qa, ka, va, ga, gn_gain.reshape(1, width))


def _moba_select_kernel(q_ref, km_ref, qe_ref, qo_ref):
    ts = q_ref.shape[0]
    nb = km_ref.shape[0]
    qf = q_ref[...]
    km = km_ref[...]
    lane = lax.broadcasted_iota(jnp.int32, (ts, LANES), 1)
    even_data = lane < HEAD_DIM
    nt = (((1,), (1,)), ((), ()))
    blk = lax.broadcasted_iota(jnp.int32, (nb, ts), 0).astype(F32)
    col = lax.broadcasted_iota(jnp.int32, (nb, ts), 1)
    own = ((pl.program_id(2) * ts + col) // MOBA_BLOCK).astype(F32)
    past = blk < own
    place_row = lax.broadcasted_iota(jnp.int32, (nb, LANES), 0)
    place_lane = lax.broadcasted_iota(jnp.int32, (nb, LANES), 1)

    for h, out_ref in enumerate((qe_ref, qo_ref)):
        data = even_data if h == 0 else jnp.logical_not(even_data)
        qh = jnp.where(data, qf, 0.0)
        gate = lax.dot_general(km, qh, nt, preferred_element_type=F32,
                               precision=lax.Precision.HIGHEST)
        g = jnp.where(past, gate, NEG_INF)
        sel = jnp.zeros((nb, ts), F32)
        for _ in range(min(MOBA_TOPK, nb)):
            m = jnp.max(g, axis=0, keepdims=True)
            first = jnp.min(jnp.where(g == m, blk, float(nb)), axis=0, keepdims=True)
            pick = blk == first
            sel = jnp.where(pick, 1.0, sel)
            g = jnp.where(pick, NEG_INF, g)
        sel = jnp.where(past, sel, 0.0)
        sel = jnp.where(blk == own, 1.0, sel)
        spare_base = HEAD_DIM if h == 0 else 0
        place = (place_lane == place_row + spare_base).astype(BF16)
        placed = lax.dot_general(sel.astype(BF16), place, (((0,), (0,)), ((), ())),
                                 preferred_element_type=F32)
        bias = (1.0 - placed) * MASK_BIAS
        out_ref[...] = jnp.where(data, qh * (HEAD_DIM ** -0.5 * LOG2_E), bias).astype(out_ref.dtype)


def _moba_select(qb, kmean, batch, ts=1024):
    t, width = qb.shape
    s = t // batch
    nb = s // MOBA_BLOCK
    ts = min(ts, s)
    assert nb <= HEAD_DIM and ts % MOBA_BLOCK == 0
    km = kmean.reshape(batch, nb, width)
    qblk = lambda b, h, i: (b * (s // ts) + i, h)
    return pl.pallas_call(
        _moba_select_kernel,
        grid=(batch, width // LANES, s // ts),
        in_specs=[pl.BlockSpec((ts, LANES), qblk),
                  pl.BlockSpec((None, nb, LANES), lambda b, h, i: (b, 0, h))],
        out_specs=[pl.BlockSpec((ts, LANES), qblk)] * 2,
        out_shape=[jax.ShapeDtypeStruct((t, width), BF16)] * 2,
        compiler_params=_params("parallel", "parallel", "parallel"),
        name="moba_select",
    )(qb, km)


def _moba_kernel(qe_ref, qo_ref, ke_ref, ko_ref, ve_ref, vo_ref, o_ref, s_ref, m_ref, acc_ref):
    qi = pl.program_id(2)
    tq = qe_ref.shape[0]
    slab = MOBA_GROUP * MOBA_BLOCK
    nt = (((1,), (1,)), ((), ()))
    last = qi // MOBA_GROUP
    heads = ((qe_ref, ke_ref, ve_ref), (qo_ref, ko_ref, vo_ref))
    r = lax.broadcasted_iota(jnp.int32, (tq, slab), 0)
    c = lax.broadcasted_iota(jnp.int32, (tq, slab), 1)
    causal = c - r <= (qi - last * MOBA_GROUP) * MOBA_BLOCK

    def scores(groups, masked):
        for h, (q_ref, k_ref, _) in enumerate(heads):
            mx = None if masked else m_ref[h]
            for t in groups:
                off = pl.multiple_of(t * slab, slab)
                sb = lax.dot_general(q_ref[...], k_ref[pl.ds(off, slab), :], nt,
                                     preferred_element_type=F32)
                if masked:
                    sb = jnp.where(causal, sb, NEG_INF)
                s_ref[h, t] = sb
                for u in range(slab // LANES):
                    part = sb[:, u * LANES:(u + 1) * LANES]
                    mx = part if mx is None else jnp.maximum(mx, part)
            m_ref[h] = mx

    def weighted_values(groups, first):
        for h, (_, _, v_ref) in enumerate(heads):
            acc = None if first else acc_ref[h]
            for t in groups:
                off = pl.multiple_of(t * slab, slab)
                p = jnp.exp2(s_ref[h, t] - row_max[h]).astype(BF16)
                pv = jnp.dot(p, v_ref[pl.ds(off, slab), :], preferred_element_type=F32)
                acc = pv if acc is None else acc + pv
            acc_ref[h] = acc

    def over_past_groups(fn):
        lax.fori_loop(0, last // 2, lambda i, carry: (fn((2 * i, 2 * i + 1)), carry)[1], 0)

        @pl.when(last % 2 == 1)
        def _():
            fn((last - 1,))

    scores((last,), True)
    over_past_groups(lambda groups: scores(groups, False))
    row_max = [jnp.max(m_ref[h], axis=-1, keepdims=True) for h in range(2)]
    weighted_values((last,), True)
    over_past_groups(lambda groups: weighted_values(groups, False))

    lane = lax.broadcasted_iota(jnp.int32, (tq, LANES), 1)
    a0 = acc_ref[0]
    a1 = acc_ref[1]
    o0 = a0 / pltpu.roll(a0, HEAD_DIM, 1)
    o1 = a1 / pltpu.roll(a1, HEAD_DIM, 1)
    o_ref[...] = jnp.where(lane < HEAD_DIM, o0, o1).astype(o_ref.dtype)


def _moba(qbe, qbo, kbe, kbo, vbe, vbo, batch):
    t, width = qbe.shape
    s = t // batch
    tq = MOBA_BLOCK
    nq = s // tq
    nb = s // MOBA_BLOCK
    assert nb % MOBA_GROUP == 0
    qblk = lambda b, h, i: (b * nq + i, h)
    kvblk = lambda b, h, i: (b, h)
    return pl.pallas_call(
        _moba_kernel,
        grid=(batch, width // LANES, nq),
        in_specs=[pl.BlockSpec((tq, LANES), qblk)] * 2 + [pl.BlockSpec((s, LANES), kvblk)] * 4,
        out_specs=pl.BlockSpec((tq, LANES), qblk),
        out_shape=jax.ShapeDtypeStruct((t, width), BF16),
        scratch_shapes=[pltpu.VMEM((2, nb // MOBA_GROUP, tq, MOBA_GROUP * MOBA_BLOCK), F32),
                        pltpu.VMEM((2, tq, LANES), F32),
                        pltpu.VMEM((2, tq, LANES), F32)],
        compiler_params=_params("parallel", "parallel", "arbitrary"),
        name="moba",
    )(qbe, qbo, kbe, kbo, vbe, vbo)


def _out_proj_kernel(*refs, n_in):
    a_refs = refs[:n_in]
    w_ref, x_ref, g_ref, b_ref, o_ref = refs[n_in:]
    y = None
    off = 0
    for a_ref in a_refs:
        kdim = a_ref.shape[1]
        part = jnp.dot(a_ref[...], w_ref[off:off + kdim, :], preferred_element_type=F32)
        y = part if y is None else y + part
        off += kdim
    z = DEEPNORM_ALPHA * x_ref[...] + y
    o_ref[...] = _layer_norm_rows(z, g_ref[...], b_ref[...])


def _out_proj_ln(acts, w_bf16, x2d, gain, bias, tm=512):
    t, d = x2d.shape
    row = lambda i: (i, 0)
    const = lambda i: (0, 0)
    return pl.pallas_call(
        functools.partial(_out_proj_kernel, n_in=len(acts)),
        grid=(t // tm,),
        in_specs=[pl.BlockSpec((tm, a.shape[1]), row) for a in acts]
        + [pl.BlockSpec(w_bf16.shape, const),
           pl.BlockSpec((tm, d), row),
           pl.BlockSpec((1, d), const),
           pl.BlockSpec((1, d), const)],
        out_specs=pl.BlockSpec((tm, d), row),
        out_shape=jax.ShapeDtypeStruct((t, d), F32),
        compiler_params=_params("parallel"),
        name="out_proj_ln",
    )(*acts, w_bf16, x2d, gain.reshape(1, d), bias.reshape(1, d))


def _ffn_kernel(x_ref, w1_ref, w2_ref, g_ref, b_ref, o_ref, xb_ref, acc_ref):
    j = pl.program_id(1)

    @pl.when(j == 0)
    def _():
        xb_ref[...] = x_ref[...].astype(BF16)
        acc_ref[...] = jnp.zeros_like(acc_ref)

    h = jnp.dot(xb_ref[...], w1_ref[...], preferred_element_type=F32)
    h = jnp.square(jnp.maximum(h, 0.0)).astype(BF16)
    acc_ref[...] += jnp.dot(h, w2_ref[...], preferred_element_type=F32)

    @pl.when(j == pl.num_programs(1) - 1)
    def _():
        z = DEEPNORM_ALPHA * x_ref[...] + acc_ref[...]
        o_ref[...] = _layer_norm_rows(z, g_ref[...], b_ref[...])


def _ffn_ln(x2d, w1_bf16, w2_bf16, gain, bias, tm=1024, th=512):
    t, d = x2d.shape
    hdim = w1_bf16.shape[1]
    return pl.pallas_call(
        _ffn_kernel,
        grid=(t // tm, hdim // th),
        in_specs=[pl.BlockSpec((tm, d), lambda i, j: (i, 0)),
                  pl.BlockSpec((d, th), lambda i, j: (0, j)),
                  pl.BlockSpec((th, d), lambda i, j: (j, 0)),
                  pl.BlockSpec((1, d), lambda i, j: (0, 0)),
                  pl.BlockSpec((1, d), lambda i, j: (0, 0))],
        out_specs=pl.BlockSpec((tm, d), lambda i, j: (i, 0)),
        out_shape=jax.ShapeDtypeStruct((t, d), F32),
        scratch_shapes=[pltpu.VMEM((tm, d), BF16), pltpu.VMEM((tm, d), F32)],
        compiler_params=_params("parallel", "arbitrary"),
        name="ffn_ln",
    )(x2d, w1_bf16, w2_bf16, gain.reshape(1, d), bias.reshape(1, d))


def _gmlp_kernel(x_ref, w_ref, b_ref, lng_ref, lnb_ref, ws_ref, bst_ref, o_ref, v_ref):
    tm = x_ref.shape[0]
    gw = o_ref.shape[1]
    c = GMLP_CHUNK
    cw = gw // GMLP_GROUPS
    xb = x_ref[...].astype(BF16)
    v = jnp.dot(xb, w_ref[:, gw:], preferred_element_type=F32) + b_ref[:, gw:]
    v_ref[...] = _layer_norm_rows(jax.nn.gelu(v), lng_ref[...], lnb_ref[...]).astype(BF16)
    r = lax.broadcasted_iota(jnp.int32, (c, c), 0)
    cc = lax.broadcasted_iota(jnp.int32, (c, c), 1)
    causal = cc <= r
    bst = bst_ref[...]
    for g in range(GMLP_GROUPS):
        cols = slice(g * cw, (g + 1) * cw)
        ws = jnp.where(causal, ws_ref[g], 0.0).astype(BF16)
        u = jax.nn.gelu(jnp.dot(xb, w_ref[:, cols], preferred_element_type=F32) + b_ref[:, cols])
        for ci in range(tm // c):
            rows = slice(ci * c, (ci + 1) * c)
            sv = jnp.dot(ws, v_ref[rows, cols], preferred_element_type=F32) + bst[:, g:g + 1]
            o_ref[rows, cols] = (u[rows] * sv).astype(o_ref.dtype)


def _gmlp_gate(x2d, w_uv_bf16, b_uv, ln_g, ln_b, w_s, b_s, tm=512):
    t, d = x2d.shape
    n = w_uv_bf16.shape[1]
    gw = n // 2
    row = lambda i: (i, 0)
    const = lambda i: (0, 0)
    return pl.pallas_call(
        _gmlp_kernel,
        grid=(t // tm,),
        in_specs=[pl.BlockSpec((tm, d), row),
                  pl.BlockSpec((d, n), const),
                  pl.BlockSpec((1, n), const),
                  pl.BlockSpec((1, gw), const),
                  pl.BlockSpec((1, gw), const),
                  pl.BlockSpec(w_s.shape, lambda i: (0, 0, 0)),
                  pl.BlockSpec((GMLP_CHUNK, GMLP_GROUPS), const)],
        out_specs=pl.BlockSpec((tm, gw), row),
        out_shape=jax.ShapeDtypeStruct((t, gw), BF16),
        scratch_shapes=[pltpu.VMEM((tm, gw), BF16)],
        compiler_params=_params("parallel"),
        name="gmlp_gate",
    )(x2d, w_uv_bf16, b_uv.reshape(1, n), ln_g.reshape(1, gw), ln_b.reshape(1, gw), w_s, b_s.T)


def kernel(x, positions, ln_gain, ln_bias, mix_w_in, ret_gn_gain, mix_w_out,
           gmlp_w_uv, gmlp_b_uv, gmlp_ln_gain, gmlp_ln_bias, gmlp_w_s, gmlp_b_s,
           gmlp_w_out, ffn_w_in, ffn_w_out):
    batch, seq, d = x.shape
    depth = ln_gain.shape[0]
    h = x.reshape(batch * seq, d)
    cos_t, sin_t = _rope_tables(positions)
    log_gamma = jnp.log1p(-jnp.exp2(-5.0 - jnp.arange(N_RET_HEADS, dtype=F32)))
    for layer in range(depth):
        i = layer // 2
        if layer % 2 == 0:
            qa, ka, va, ga, qb, kbe, kbo, vbe, vbo, kmean = _in_proj(
                h, mix_w_in[i].astype(BF16), cos_t, sin_t, seq // MOBA_BLOCK)
            ya = _retention(qa, ka, va, ga, ret_gn_gain[i], log_gamma, batch)
            qbe, qbo = _moba_select(qb, kmean, batch)
            yb = _moba(qbe, qbo, kbe, kbo, vbe, vbo, batch)
            h = _out_proj_ln([ya, yb], mix_w_out[i].astype(BF16), h,
                             ln_gain[layer, 0], ln_bias[layer, 0])
        else:
            gated = _gmlp_gate(h, gmlp_w_uv[i].astype(BF16), gmlp_b_uv[i], gmlp_ln_gain[i],
                               gmlp_ln_bias[i], gmlp_w_s[i], gmlp_b_s[i])
            h = _out_proj_ln([gated], gmlp_w_out[i].astype(BF16), h,
                             ln_gain[layer, 0], ln_bias[layer, 0])
        h = _ffn_ln(h, ffn_w_in[layer].astype(BF16), ffn_w_out[layer].astype(BF16),
                    ln_gain[layer, 1], ln_bias[layer, 1])
    return h.reshape(batch, seq, d)
```

```python
import functools

import jax
import jax.numpy as jnp
from jax import lax
from jax.experimental import pallas as pl
from jax.experimental.pallas import tpu as pltpu

HEAD_DIM = 64
N_RET_HEADS = 8
N_MOBA_HEADS = 8
RET_WIDTH = N_RET_HEADS * HEAD_DIM
MOBA_WIDTH = N_MOBA_HEADS * HEAD_DIM
RET_CHUNK = 128
MOBA_BLOCK = 256
MOBA_TOPK = 3
GMLP_CHUNK = 128
GMLP_GROUPS = 8
ROPE_THETA = 10000.0
LN_EPS = 1e-5
DEPTH = 4
DEEPNORM_ALPHA = (2 * DEPTH) ** 0.25

LANES = 128
VMEM_LIMIT_BYTES = 52 * 1024 * 1024

BF16 = jnp.bfloat16
F32 = jnp.float32
NEG_INF = float("-inf")
MOBA_GROUP = 4
MASK_BIAS = -1e30
LOG2_E = 1.4426950408889634


def _params(*sem):
    return pltpu.CompilerParams(dimension_semantics=sem, vmem_limit_bytes=VMEM_LIMIT_BYTES)


def _layer_norm_rows(z, g, b):
    mu = jnp.mean(z, axis=-1, keepdims=True)
    d = z - mu
    var = jnp.mean(d * d, axis=-1, keepdims=True)
    return d * lax.rsqrt(var + LN_EPS) * g + b


def _rope_table_kernel(pos_ref, freq_ref, cos_ref, sin_ref):
    ang = pos_ref[...] * freq_ref[...]
    lane = lax.broadcasted_iota(jnp.int32, ang.shape, 1)
    first_half = (lane % HEAD_DIM) < (HEAD_DIM // 2)
    cos_ref[...] = jnp.cos(ang)
    sin_ref[...] = jnp.where(first_half, -jnp.sin(ang), jnp.sin(ang))


def _rope_tables(positions):
    t = positions.size
    tm = min(t, 2048)
    pos = positions.reshape(t, 1).astype(F32)
    inv_freq = ROPE_THETA ** (-jnp.arange(0, HEAD_DIM, 2, dtype=F32) / HEAD_DIM)
    freq = jnp.tile(inv_freq, LANES // (HEAD_DIM // 2)).reshape(1, LANES)
    return pl.pallas_call(
        _rope_table_kernel,
        grid=(t // tm,),
        in_specs=[pl.BlockSpec((tm, 1), lambda i: (i, 0)),
                  pl.BlockSpec((1, LANES), lambda i: (0, 0))],
        out_specs=[pl.BlockSpec((tm, LANES), lambda i: (i, 0))] * 2,
        out_shape=[jax.ShapeDtypeStruct((t, LANES), F32)] * 2,
        compiler_params=_params("parallel"),
        name="rope_tables",
    )(pos, freq)


def _in_proj_kernel(x_ref, w_ref, cos_ref, sin_ref,
                    qa_ref, ka_ref, va_ref, ga_ref, qb_ref, kbe_ref, kbo_ref, vbe_ref, vbo_ref,
                    km_ref, *, n_blocks):
    tm = x_ref.shape[0]
    width = RET_WIDTH
    xb = x_ref[...].astype(BF16)
    reps = width // LANES
    cos = jnp.concatenate([cos_ref[...]] * reps, axis=1)
    sin = jnp.concatenate([sin_ref[...]] * reps, axis=1)
    lane = lax.broadcasted_iota(jnp.int32, (tm, width), 1)
    first_half = (lane % HEAD_DIM) < (HEAD_DIM // 2)

    def proj(g):
        return jnp.dot(xb, w_ref[:, g * width:(g + 1) * width], preferred_element_type=F32)

    def rope(t):
        partner = jnp.where(first_half,
                            pltpu.roll(t, width - HEAD_DIM // 2, 1),
                            pltpu.roll(t, HEAD_DIM // 2, 1))
        return t * cos + partner * sin

    qa_ref[...] = rope(proj(0)).astype(qa_ref.dtype)
    ka_ref[...] = rope(proj(1)).astype(ka_ref.dtype)
    va_ref[...] = proj(2).astype(va_ref.dtype)
    ga_ref[...] = proj(3).astype(ga_ref.dtype)
    qb_ref[...] = rope(proj(4)).astype(qb_ref.dtype)
    kb = rope(proj(5))
    vb = proj(6)
    pair_lane = lane % LANES
    even_data = pair_lane < HEAD_DIM
    row = lax.broadcasted_iota(jnp.int32, (tm, width), 0)
    block = ((pl.program_id(0) * tm + row) // MOBA_BLOCK) % n_blocks
    onehot_e = (pair_lane - HEAD_DIM == block).astype(F32)
    onehot_o = (pair_lane == block).astype(F32)
    kbe_ref[...] = jnp.where(even_data, kb, onehot_e).astype(kbe_ref.dtype)
    kbo_ref[...] = jnp.where(even_data, onehot_o, kb).astype(kbo_ref.dtype)
    vbe_ref[...] = jnp.where(even_data, vb, 1.0).astype(vbe_ref.dtype)
    vbo_ref[...] = jnp.where(even_data, 1.0, vb).astype(vbo_ref.dtype)
    for blk in range(tm // MOBA_BLOCK):
        km_ref[blk] = jnp.mean(kb[blk * MOBA_BLOCK:(blk + 1) * MOBA_BLOCK], axis=0, keepdims=True)


def _in_proj(x2d, w_bf16, cos_t, sin_t, n_blocks, tm=512):
    t, d = x2d.shape
    n = w_bf16.shape[1]
    width = RET_WIDTH
    assert n_blocks <= HEAD_DIM and tm % MOBA_BLOCK == 0
    row = lambda i: (i, 0)
    out_dtypes = [BF16, F32, BF16, F32, F32, BF16, BF16, BF16, BF16]
    return pl.pallas_call(
        functools.partial(_in_proj_kernel, n_blocks=n_blocks),
        grid=(t // tm,),
        in_specs=[pl.BlockSpec((tm, d), row),
                  pl.BlockSpec((d, n), lambda i: (0, 0)),
                  pl.BlockSpec((tm, LANES), row),
                  pl.BlockSpec((tm, LANES), row)],
        out_specs=[pl.BlockSpec((tm, width), row)] * len(out_dtypes)
        + [pl.BlockSpec((tm // MOBA_BLOCK, 1, width), lambda i: (i, 0, 0))],
        out_shape=[jax.ShapeDtypeStruct((t, width), dt) for dt in out_dtypes]
        + [jax.ShapeDtypeStruct((t // MOBA_BLOCK, 1, width), F32)],
        compiler_params=_params("parallel"),
        name="in_proj",
    )(x2d, w_bf16, cos_t, sin_t)


def _retention_kernel(lg_ref, q_ref, k_ref, v_ref, g_ref, gain_ref, o_ref, state_ref, decay_ref):
    hp = pl.program_id(1)
    c = RET_CHUNK
    tm = q_ref.shape[0]
    lane = lax.broadcasted_iota(jnp.int32, (c, LANES), 1)
    row = lax.broadcasted_iota(jnp.int32, (c, LANES), 0)
    head0 = lane < HEAD_DIM

    @pl.when(pl.program_id(2) == 0)
    def _():
        state_ref[...] = jnp.zeros_like(state_ref)
        lg0 = lg_ref[2 * hp]
        lg1 = lg_ref[2 * hp + 1]
        lg_lane = jnp.where(head0, lg0, lg1)
        idx = row.astype(F32)
        diff = (row - lane).astype(F32)
        dpos = jnp.maximum(diff, 0.0)
        decay_ref[0] = jnp.exp(lg_lane * (idx + 1.0))
        decay_ref[1] = jnp.exp(lg_lane * (c - 1.0 - idx))
        decay_ref[2] = jnp.exp(lg_lane * float(c))
        decay_ref[3] = jnp.where(diff >= 0, jnp.exp(lg0 * dpos), 0.0)
        decay_ref[4] = jnp.where(diff >= 0, jnp.exp(lg1 * dpos), 0.0)

    xi = decay_ref[0]
    zeta = decay_ref[1]
    chunk_decay = decay_ref[2]
    decay0 = decay_ref[3]
    decay1 = decay_ref[4]
    same_head = (row < HEAD_DIM) == head0
    gain = gain_ref[...]
    nt = (((1,), (1,)), ((), ()))
    tn = (((0,), (0,)), ((), ()))

    for ci in range(tm // c):
        sl = slice(ci * c, (ci + 1) * c)
        q = q_ref[sl, :]
        kf = k_ref[sl, :] * (HEAD_DIM ** -0.5)
        v = v_ref[sl, :]
        kb = kf.astype(BF16)
        zero = jnp.zeros_like(q)
        q0 = jnp.where(head0, q, zero)
        q1 = jnp.where(head0, zero, q)
        s0 = lax.dot_general(q0, kb, nt, preferred_element_type=F32) * decay0
        s1 = lax.dot_general(q1, kb, nt, preferred_element_type=F32) * decay1
        inner = jnp.where(head0,
                          jnp.dot(s0.astype(BF16), v, preferred_element_type=F32),
                          jnp.dot(s1.astype(BF16), v, preferred_element_type=F32))
        state = state_ref[...]
        cross = jnp.dot(q, state.astype(BF16), preferred_element_type=F32) * xi
        kv = lax.dot_general((kf * zeta).astype(BF16), v, tn, preferred_element_type=F32)
        state_ref[...] = state * chunk_decay + jnp.where(same_head, kv, 0.0)

        y = inner + cross
        inv = 1.0 / HEAD_DIM
        mu = jnp.where(head0,
                       jnp.sum(jnp.where(head0, y, 0.0), axis=-1, keepdims=True),
                       jnp.sum(jnp.where(head0, 0.0, y), axis=-1, keepdims=True)) * inv
        d = y - mu
        dd = d * d
        var = jnp.where(head0,
                        jnp.sum(jnp.where(head0, dd, 0.0), axis=-1, keepdims=True),
                        jnp.sum(jnp.where(head0, 0.0, dd), axis=-1, keepdims=True)) * inv
        yn = d * lax.rsqrt(var + LN_EPS) * gain
        o_ref[sl, :] = (yn * jax.nn.silu(g_ref[sl, :])).astype(o_ref.dtype)


def _retention(qa, ka, va, ga, gn_gain, log_gamma, batch, tm=512):
    t, width = qa.shape
    s = t // batch
    nt = s // tm
    blk = lambda b, h, i: (b * nt + i, h)
    return pl.pallas_call(
        _retention_kernel,
        grid=(batch, width // LANES, nt),
        in_specs=[pl.BlockSpec(memory_space=pltpu.SMEM),
                  pl.BlockSpec((tm, LANES), blk),
                  pl.BlockSpec((tm, LANES), blk),
                  pl.BlockSpec((tm, LANES), blk),
                  pl.BlockSpec((tm, LANES), blk),
                  pl.BlockSpec((1, LANES), lambda b, h, i: (0, h))],
        out_specs=pl.BlockSpec((tm, LANES), blk),
        out_shape=jax.ShapeDtypeStruct((t, width), BF16),
        scratch_shapes=[pltpu.VMEM((LANES, LANES), F32),
                        pltpu.VMEM((5, RET_CHUNK, LANES), F32)],
        compiler_params=_params("parallel", "parallel", "arbitrary"),
        name="retention",
    )(log_gamma, qa, ka, va, ga, gn_gain.reshape(1, width))


def _moba_select_kernel(q_ref, km_ref, qe_ref, qo_ref):
    ts = q_ref.shape[0]
    nb = km_ref.shape[0]
    qf = q_ref[...]
    km = km_ref[...]
    lane = lax.broadcasted_iota(jnp.int32, (ts, LANES), 1)
    even_data = lane < HEAD_DIM
    nt = (((1,), (1,)), ((), ()))
    blk = lax.broadcasted_iota(jnp.int32, (nb, ts), 0).astype(F32)
    col = lax.broadcasted_iota(jnp.int32, (nb, ts), 1)
    own = ((pl.program_id(2) * ts + col) // MOBA_BLOCK).astype(F32)
    past = blk < own
    place_row = lax.broadcasted_iota(jnp.int32, (nb, LANES), 0)
    place_lane = lax.broadcasted_iota(jnp.int32, (nb, LANES), 1)

    for h, out_ref in enumerate((qe_ref, qo_ref)):
        data = even_data if h == 0 else jnp.logical_not(even_data)
        qh = jnp.where(data, qf, 0.0)
        gate = lax.dot_general(km, qh, nt, preferred_element_type=F32,
                               precision=lax.Precision.HIGHEST)
        g = jnp.where(past, gate, NEG_INF)
        sel = jnp.zeros((nb, ts), F32)
        for _ in range(min(MOBA_TOPK, nb)):
            m = jnp.max(g, axis=0, keepdims=True)
            first = jnp.min(jnp.where(g == m, blk, float(nb)), axis=0, keepdims=True)
            pick = blk == first
            sel = jnp.where(pick, 1.0, sel)
            g = jnp.where(pick, NEG_INF, g)
        sel = jnp.where(past, sel, 0.0)
        sel = jnp.where(blk == own, 1.0, sel)
        spare_base = HEAD_DIM if h == 0 else 0
        place = (place_lane == place_row + spare_base).astype(BF16)
        placed = lax.dot_general(sel.astype(BF16), place, (((0,), (0,)), ((), ())),
                                 preferred_element_type=F32)
        bias = (1.0 - placed) * MASK_BIAS
        out_ref[...] = jnp.where(data, qh * (HEAD_DIM ** -0.5 * LOG2_E), bias).astype(out_ref.dtype)


def _moba_select(qb, kmean, batch, ts=1024):
    t, width = qb.shape
    s = t // batch
    nb = s // MOBA_BLOCK
    ts = min(ts, s)
    assert nb <= HEAD_DIM and ts % MOBA_BLOCK == 0
    km = kmean.reshape(batch, nb, width)
    qblk = lambda b, h, i: (b * (s // ts) + i, h)
    return pl.pallas_call(
        _moba_select_kernel,
        grid=(batch, width // LANES, s // ts),
        in_specs=[pl.BlockSpec((ts, LANES), qblk),
                  pl.BlockSpec((None, nb, LANES), lambda b, h, i: (b, 0, h))],
        out_specs=[pl.BlockSpec((ts, LANES), qblk)] * 2,
        out_shape=[jax.ShapeDtypeStruct((t, width), BF16)] * 2,
        compiler_params=_params("parallel", "parallel", "parallel"),
        name="moba_select",
    )(qb, km)


def _moba_kernel(qe_ref, qo_ref, ke_ref, ko_ref, ve_ref, vo_ref, o_ref, s_ref, m_ref, acc_ref):
    qi = pl.program_id(2)
    tq = qe_ref.shape[0]
    slab = MOBA_GROUP * MOBA_BLOCK
    nt = (((1,), (1,)), ((), ()))
    last = qi // MOBA_GROUP
    heads = ((qe_ref, ke_ref, ve_ref), (qo_ref, ko_ref, vo_ref))
    r = lax.broadcasted_iota(jnp.int32, (tq, slab), 0)
    c = lax.broadcasted_iota(jnp.int32, (tq, slab), 1)
    causal = c - r <= (qi - last * MOBA_GROUP) * MOBA_BLOCK

    def scores(groups, masked):
        for h, (q_ref, k_ref, _) in enumerate(heads):
            mx = None if masked else m_ref[h]
            for t in groups:
                off = pl.multiple_of(t * slab, slab)
                sb = lax.dot_general(q_ref[...], k_ref[pl.ds(off, slab), :], nt,
                                     preferred_element_type=F32)
                if masked:
                    sb = jnp.where(causal, sb, NEG_INF)
                s_ref[h, t] = sb
                for u in range(slab // LANES):
                    part = sb[:, u * LANES:(u + 1) * LANES]
                    mx = part if mx is None else jnp.maximum(mx, part)
            m_ref[h] = mx

    def weighted_values(groups, first):
        for h, (_, _, v_ref) in enumerate(heads):
            acc = None if first else acc_ref[h]
            for t in groups:
                off = pl.multiple_of(t * slab, slab)
                p = jnp.exp2(s_ref[h, t] - row_max[h]).astype(BF16)
                pv = jnp.dot(p, v_ref[pl.ds(off, slab), :], preferred_element_type=F32)
                acc = pv if acc is None else acc + pv
            acc_ref[h] = acc

    def over_past_groups(fn):
        lax.fori_loop(0, last // 2, lambda i, carry: (fn((2 * i, 2 * i + 1)), carry)[1], 0)

        @pl.when(last % 2 == 1)
        def _():
            fn((last - 1,))

    scores((last,), True)
    over_past_groups(lambda groups: scores(groups, False))
    row_max = [jnp.max(m_ref[h], axis=-1, keepdims=True) for h in range(2)]
    weighted_values((last,), True)
    over_past_groups(lambda groups: weighted_values(groups, False))

    lane = lax.broadcasted_iota(jnp.int32, (tq, LANES), 1)
    a0 = acc_ref[0]
    a1 = acc_ref[1]
    o0 = a0 / pltpu.roll(a0, HEAD_DIM, 1)
    o1 = a1 / pltpu.roll(a1, HEAD_DIM, 1)
    o_ref[...] = jnp.where(lane < HEAD_DIM, o0, o1).astype(o_ref.dtype)


def _moba(qbe, qbo, kbe, kbo, vbe, vbo, batch):
    t, width = qbe.shape
    s = t // batch
    tq = MOBA_BLOCK
    nq = s // tq
    nb = s // MOBA_BLOCK
    assert nb % MOBA_GROUP == 0
    qblk = lambda b, h, i: (b * nq + i, h)
    kvblk = lambda b, h, i: (b, h)
    return pl.pallas_call(
        _moba_kernel,
        grid=(batch, width // LANES, nq),
        in_specs=[pl.BlockSpec((tq, LANES), qblk)] * 2 + [pl.BlockSpec((s, LANES), kvblk)] * 4,
        out_specs=pl.BlockSpec((tq, LANES), qblk),
        out_shape=jax.ShapeDtypeStruct((t, width), BF16),
        scratch_shapes=[pltpu.VMEM((2, nb // MOBA_GROUP, tq, MOBA_GROUP * MOBA_BLOCK), F32),
                        pltpu.VMEM((2, tq, LANES), F32),
                        pltpu.VMEM((2, tq, LANES), F32)],
        compiler_params=_params("parallel", "parallel", "arbitrary"),
        name="moba",
    )(qbe, qbo, kbe, kbo, vbe, vbo)


def _out_proj_kernel(*refs, n_in):
    a_refs = refs[:n_in]
    w_ref, x_ref, g_ref, b_ref, o_ref = refs[n_in:]
    y = None
    off = 0
    for a_ref in a_refs:
        kdim = a_ref.shape[1]
        part = jnp.dot(a_ref[...], w_ref[off:off + kdim, :], preferred_element_type=F32)
        y = part if y is None else y + part
        off += kdim
    z = DEEPNORM_ALPHA * x_ref[...] + y
    o_ref[...] = _layer_norm_rows(z, g_ref[...], b_ref[...])


def _out_proj_ln(acts, w_bf16, x2d, gain, bias, tm=512):
    t, d = x2d.shape
    row = lambda i: (i, 0)
    const = lambda i: (0, 0)
    return pl.pallas_call(
        functools.partial(_out_proj_kernel, n_in=len(acts)),
        grid=(t // tm,),
        in_specs=[pl.BlockSpec((tm, a.shape[1]), row) for a in acts]
        + [pl.BlockSpec(w_bf16.shape, const),
           pl.BlockSpec((tm, d), row),
           pl.BlockSpec((1, d), const),
           pl.BlockSpec((1, d), const)],
        out_specs=pl.BlockSpec((tm, d), row),
        out_shape=jax.ShapeDtypeStruct((t, d), F32),
        compiler_params=_params("parallel"),
        name="out_proj_ln",
    )(*acts, w_bf16, x2d, gain.reshape(1, d), bias.reshape(1, d))


def _ffn_kernel(x_ref, w1_ref, w2_ref, g_ref, b_ref, o_ref, xb_ref, acc_ref):
    j = pl.program_id(1)

    @pl.when(j == 0)
    def _():
        xb_ref[...] = x_ref[...].astype(BF16)
        acc_ref[...] = jnp.zeros_like(acc_ref)

    h = jnp.dot(xb_ref[...], w1_ref[...], preferred_element_type=F32)
    h = jnp.square(jnp.maximum(h, 0.0)).astype(BF16)
    acc_ref[...] += jnp.dot(h, w2_ref[...], preferred_element_type=F32)

    @pl.when(j == pl.num_programs(1) - 1)
    def _():
        z = DEEPNORM_ALPHA * x_ref[...] + acc_ref[...]
        o_ref[...] = _layer_norm_rows(z, g_ref[...], b_ref[...])


def _ffn_ln(x2d, w1_bf16, w2_bf16, gain, bias, tm=1024, th=1024):
    t, d = x2d.shape
    hdim = w1_bf16.shape[1]
    return pl.pallas_call(
        _ffn_kernel,
        grid=(t // tm, hdim // th),
        in_specs=[pl.BlockSpec((tm, d), lambda i, j: (i, 0)),
                  pl.BlockSpec((d, th), lambda i, j: (0, j)),
                  pl.BlockSpec((th, d), lambda i, j: (j, 0)),
                  pl.BlockSpec((1, d), lambda i, j: (0, 0)),
                  pl.BlockSpec((1, d), lambda i, j: (0, 0))],
        out_specs=pl.BlockSpec((tm, d), lambda i, j: (i, 0)),
        out_shape=jax.ShapeDtypeStruct((t, d), F32),
        scratch_shapes=[pltpu.VMEM((tm, d), BF16), pltpu.VMEM((tm, d), F32)],
        compiler_params=_params("parallel", "arbitrary"),
        name="ffn_ln",
    )(x2d, w1_bf16, w2_bf16, gain.reshape(1, d), bias.reshape(1, d))


def _gmlp_kernel(x_ref, w_ref, b_ref, lng_ref, lnb_ref, ws_ref, bst_ref, wo_ref, g_ref, beta_ref,
                 o_ref, v_ref, gated_ref):
    tm = x_ref.shape[0]
    gw = v_ref.shape[1]
    c = GMLP_CHUNK
    cw = gw // GMLP_GROUPS
    xb = x_ref[...].astype(BF16)
    v = jnp.dot(xb, w_ref[:, gw:], preferred_element_type=F32) + b_ref[:, gw:]
    v_ref[...] = _layer_norm_rows(jax.nn.gelu(v), lng_ref[...], lnb_ref[...]).astype(BF16)
    r = lax.broadcasted_iota(jnp.int32, (c, c), 0)
    cc = lax.broadcasted_iota(jnp.int32, (c, c), 1)
    causal = cc <= r
    bst = bst_ref[...]
    for g in range(GMLP_GROUPS):
        cols = slice(g * cw, (g + 1) * cw)
        ws = jnp.where(causal, ws_ref[g], 0.0).astype(BF16)
        u = jax.nn.gelu(jnp.dot(xb, w_ref[:, cols], preferred_element_type=F32) + b_ref[:, cols])
        for ci in range(tm // c):
            rows = slice(ci * c, (ci + 1) * c)
            sv = jnp.dot(ws, v_ref[rows, cols], preferred_element_type=F32) + bst[:, g:g + 1]
            gated_ref[rows, cols] = (u[rows] * sv).astype(BF16)
    y = jnp.dot(gated_ref[...], wo_ref[...], preferred_element_type=F32)
    z = DEEPNORM_ALPHA * x_ref[...] + y
    o_ref[...] = _layer_norm_rows(z, g_ref[...], beta_ref[...])


def _gmlp_mixer_ln(x2d, w_uv_bf16, b_uv, ln_g, ln_b, w_s, b_s, w_out_bf16, gain, bias, tm=512):
    t, d = x2d.shape
    n = w_uv_bf16.shape[1]
    gw = n // 2
    row = lambda i: (i, 0)
    const = lambda i: (0, 0)
    return pl.pallas_call(
        _gmlp_kernel,
        grid=(t // tm,),
        in_specs=[pl.BlockSpec((tm, d), row),
                  pl.BlockSpec((d, n), const),
                  pl.BlockSpec((1, n), const),
                  pl.BlockSpec((1, gw), const),
                  pl.BlockSpec((1, gw), const),
                  pl.BlockSpec(w_s.shape, lambda i: (0, 0, 0)),
                  pl.BlockSpec((GMLP_CHUNK, GMLP_GROUPS), const),
                  pl.BlockSpec((gw, d), const),
                  pl.BlockSpec((1, d), const),
                  pl.BlockSpec((1, d), const)],
        out_specs=pl.BlockSpec((tm, d), row),
        out_shape=jax.ShapeDtypeStruct((t, d), F32),
        scratch_shapes=[pltpu.VMEM((tm, gw), BF16), pltpu.VMEM((tm, gw), BF16)],
        compiler_params=_params("parallel"),
        name="gmlp_mixer_ln",
    )(x2d, w_uv_bf16, b_uv.reshape(1, n), ln_g.reshape(1, gw), ln_b.reshape(1, gw), w_s, b_s.T,
      w_out_bf16, gain.reshape(1, d), bias.reshape(1, d))


def kernel(x, positions, ln_gain, ln_bias, mix_w_in, ret_gn_gain, mix_w_out,
           gmlp_w_uv, gmlp_b_uv, gmlp_ln_gain, gmlp_ln_bias, gmlp_w_s, gmlp_b_s,
           gmlp_w_out, ffn_w_in, ffn_w_out):
    batch, seq, d = x.shape
    depth = ln_gain.shape[0]
    h = x.reshape(batch * seq, d)
    cos_t, sin_t = _rope_tables(positions)
    log_gamma = jnp.log1p(-jnp.exp2(-5.0 - jnp.arange(N_RET_HEADS, dtype=F32)))
    for layer in range(depth):
        i = layer // 2
        if layer % 2 == 0:
            qa, ka, va, ga, qb, kbe, kbo, vbe, vbo, kmean = _in_proj(
                h, mix_w_in[i].astype(BF16), cos_t, sin_t, seq // MOBA_BLOCK)
            ya = _retention(qa, ka, va, ga, ret_gn_gain[i], log_gamma, batch)
            qbe, qbo = _moba_select(qb, kmean, batch)
            yb = _moba(qbe, qbo, kbe, kbo, vbe, vbo, batch)
            h = _out_proj_ln([ya, yb], mix_w_out[i].astype(BF16), h,
                             ln_gain[layer, 0], ln_bias[layer, 0])
        else:
            h = _gmlp_mixer_ln(h, gmlp_w_uv[i].astype(BF16), gmlp_b_uv[i], gmlp_ln_gain[i],
                               gmlp_ln_bias[i], gmlp_w_s[i], gmlp_b_s[i],
                               gmlp_w_out[i].astype(BF16), ln_gain[layer, 0], ln_bias[layer, 0])
        h = _ffn_ln(h, ffn_w_in[layer].astype(BF16), ffn_w_out[layer].astype(BF16),
                    ln_gain[layer, 1], ln_bias[layer, 1])
    return h.reshape(batch, seq, d)
```

```python
import functools

import jax
import jax.numpy as jnp
from jax import lax
from jax.experimental import pallas as pl
from jax.experimental.pallas import tpu as pltpu

HEAD_DIM = 64
N_RET_HEADS = 8
N_MOBA_HEADS = 8
RET_WIDTH = N_RET_HEADS * HEAD_DIM
MOBA_WIDTH = N_MOBA_HEADS * HEAD_DIM
RET_CHUNK = 128
MOBA_BLOCK = 256
MOBA_TOPK = 3
GMLP_CHUNK = 128
GMLP_GROUPS = 8
ROPE_THETA = 10000.0
LN_EPS = 1e-5
DEPTH = 4
DEEPNORM_ALPHA = (2 * DEPTH) ** 0.25

LANES = 128
VMEM_LIMIT_BYTES = 52 * 1024 * 1024

BF16 = jnp.bfloat16
F32 = jnp.float32
NEG_INF = float("-inf")
MOBA_GROUP = 4
MASK_BIAS = -1e30
LOG2_E = 1.4426950408889634


def _params(*sem):
    return pltpu.CompilerParams(dimension_semantics=sem, vmem_limit_bytes=VMEM_LIMIT_BYTES)


def _layer_norm_rows(z, g, b):
    mu = jnp.mean(z, axis=-1, keepdims=True)
    d = z - mu
    var = jnp.mean(d * d, axis=-1, keepdims=True)
    return d * lax.rsqrt(var + LN_EPS) * g + b


def _rope_table_kernel(pos_ref, freq_ref, cos_ref, sin_ref):
    ang = pos_ref[...] * freq_ref[...]
    lane = lax.broadcasted_iota(jnp.int32, ang.shape, 1)
    first_half = (lane % HEAD_DIM) < (HEAD_DIM // 2)
    cos_ref[...] = jnp.cos(ang)
    sin_ref[...] = jnp.where(first_half, -jnp.sin(ang), jnp.sin(ang))


def _rope_tables(positions):
    t = positions.size
    tm = min(t, 2048)
    pos = positions.reshape(t, 1).astype(F32)
    inv_freq = ROPE_THETA ** (-jnp.arange(0, HEAD_DIM, 2, dtype=F32) / HEAD_DIM)
    freq = jnp.tile(inv_freq, LANES // (HEAD_DIM // 2)).reshape(1, LANES)
    return pl.pallas_call(
        _rope_table_kernel,
        grid=(t // tm,),
        in_specs=[pl.BlockSpec((tm, 1), lambda i: (i, 0)),
                  pl.BlockSpec((1, LANES), lambda i: (0, 0))],
        out_specs=[pl.BlockSpec((tm, LANES), lambda i: (i, 0))] * 2,
        out_shape=[jax.ShapeDtypeStruct((t, LANES), F32)] * 2,
        compiler_params=_params("parallel"),
        name="rope_tables",
    )(pos, freq)


def _in_proj_kernel(x_ref, w_ref, cos_ref, sin_ref,
                    qa_ref, ka_ref, va_ref, ga_ref, qb_ref, kbe_ref, kbo_ref, vbe_ref, vbo_ref,
                    km_ref, *, n_blocks):
    tm = x_ref.shape[0]
    width = RET_WIDTH
    xb = x_ref[...].astype(BF16)
    reps = width // LANES
    cos = jnp.concatenate([cos_ref[...]] * reps, axis=1)
    sin = jnp.concatenate([sin_ref[...]] * reps, axis=1)
    lane = lax.broadcasted_iota(jnp.int32, (tm, width), 1)
    first_half = (lane % HEAD_DIM) < (HEAD_DIM // 2)

    def proj(g):
        return jnp.dot(xb, w_ref[:, g * width:(g + 1) * width], preferred_element_type=F32)

    def rope(t):
        partner = jnp.where(first_half,
                            pltpu.roll(t, width - HEAD_DIM // 2, 1),
                            pltpu.roll(t, HEAD_DIM // 2, 1))
        return t * cos + partner * sin

    qa_ref[...] = rope(proj(0)).astype(qa_ref.dtype)
    ka_ref[...] = rope(proj(1)).astype(ka_ref.dtype)
    va_ref[...] = proj(2).astype(va_ref.dtype)
    ga_ref[...] = proj(3).astype(ga_ref.dtype)
    qb_ref[...] = rope(proj(4)).astype(qb_ref.dtype)
    kb = rope(proj(5))
    vb = proj(6)
    pair_lane = lane % LANES
    even_data = pair_lane < HEAD_DIM
    row = lax.broadcasted_iota(jnp.int32, (tm, width), 0)
    block = ((pl.program_id(0) * tm + row) // MOBA_BLOCK) % n_blocks
    onehot_e = (pair_lane - HEAD_DIM == block).astype(F32)
    onehot_o = (pair_lane == block).astype(F32)
    kbe_ref[...] = jnp.where(even_data, kb, onehot_e).astype(kbe_ref.dtype)
    kbo_ref[...] = jnp.where(even_data, onehot_o, kb).astype(kbo_ref.dtype)
    vbe_ref[...] = jnp.where(even_data, vb, 1.0).astype(vbe_ref.dtype)
    vbo_ref[...] = jnp.where(even_data, 1.0, vb).astype(vbo_ref.dtype)
    for blk in range(tm // MOBA_BLOCK):
        km_ref[blk] = jnp.mean(kb[blk * MOBA_BLOCK:(blk + 1) * MOBA_BLOCK], axis=0, keepdims=True)


def _in_proj(x2d, w_bf16, cos_t, sin_t, n_blocks, tm=512):
    t, d = x2d.shape
    n = w_bf16.shape[1]
    width = RET_WIDTH
    assert n_blocks <= HEAD_DIM and tm % MOBA_BLOCK == 0
    row = lambda i: (i, 0)
    out_dtypes = [BF16, F32, BF16, F32, F32, BF16, BF16, BF16, BF16]
    return pl.pallas_call(
        functools.partial(_in_proj_kernel, n_blocks=n_blocks),
        grid=(t // tm,),
        in_specs=[pl.BlockSpec((tm, d), row),
                  pl.BlockSpec((d, n), lambda i: (0, 0)),
                  pl.BlockSpec((tm, LANES), row),
                  pl.BlockSpec((tm, LANES), row)],
        out_specs=[pl.BlockSpec((tm, width), row)] * len(out_dtypes)
        + [pl.BlockSpec((tm // MOBA_BLOCK, 1, width), lambda i: (i, 0, 0))],
        out_shape=[jax.ShapeDtypeStruct((t, width), dt) for dt in out_dtypes]
        + [jax.ShapeDtypeStruct((t // MOBA_BLOCK, 1, width), F32)],
        compiler_params=_params("parallel"),
        name="in_proj",
    )(x2d, w_bf16, cos_t, sin_t)


def _retention_kernel(lg_ref, q_ref, k_ref, v_ref, g_ref, gain_ref, o_ref, state_ref, decay_ref):
    hp = pl.program_id(1)
    c = RET_CHUNK
    tm = q_ref.shape[0]
    lane = lax.broadcasted_iota(jnp.int32, (c, LANES), 1)
    row = lax.broadcasted_iota(jnp.int32, (c, LANES), 0)
    head0 = lane < HEAD_DIM

    @pl.when(pl.program_id(2) == 0)
    def _():
        state_ref[...] = jnp.zeros_like(state_ref)
        lg0 = lg_ref[2 * hp]
        lg1 = lg_ref[2 * hp + 1]
        lg_lane = jnp.where(head0, lg0, lg1)
        idx = row.astype(F32)
        diff = (row - lane).astype(F32)
        dpos = jnp.maximum(diff, 0.0)
        decay_ref[0] = jnp.exp(lg_lane * (idx + 1.0))
        decay_ref[1] = jnp.exp(lg_lane * (c - 1.0 - idx))
        decay_ref[2] = jnp.exp(lg_lane * float(c))
        decay_ref[3] = jnp.where(diff >= 0, jnp.exp(lg0 * dpos), 0.0)
        decay_ref[4] = jnp.where(diff >= 0, jnp.exp(lg1 * dpos), 0.0)

    xi = decay_ref[0]
    zeta = decay_ref[1]
    chunk_decay = decay_ref[2]
    decay0 = decay_ref[3]
    decay1 = decay_ref[4]
    same_head = (row < HEAD_DIM) == head0
    gain = gain_ref[...]
    nt = (((1,), (1,)), ((), ()))
    tn = (((0,), (0,)), ((), ()))

    for ci in range(tm // c):
        sl = slice(ci * c, (ci + 1) * c)
        q = q_ref[sl, :]
        kf = k_ref[sl, :] * (HEAD_DIM ** -0.5)
        v = v_ref[sl, :]
        kb = kf.astype(BF16)
        zero = jnp.zeros_like(q)
        q0 = jnp.where(head0, q, zero)
        q1 = jnp.where(head0, zero, q)
        s0 = lax.dot_general(q0, kb, nt, preferred_element_type=F32) * decay0
        s1 = lax.dot_general(q1, kb, nt, preferred_element_type=F32) * decay1
        inner = jnp.where(head0,
                          jnp.dot(s0.astype(BF16), v, preferred_element_type=F32),
                          jnp.dot(s1.astype(BF16), v, preferred_element_type=F32))
        state = state_ref[...]
        cross = jnp.dot(q, state.astype(BF16), preferred_element_type=F32) * xi
        kv = lax.dot_general((kf * zeta).astype(BF16), v, tn, preferred_element_type=F32)
        state_ref[...] = state * chunk_decay + jnp.where(same_head, kv, 0.0)

        y = inner + cross
        inv = 1.0 / HEAD_DIM
        mu = jnp.where(head0,
                       jnp.sum(jnp.where(head0, y, 0.0), axis=-1, keepdims=True),
                       jnp.sum(jnp.where(head0, 0.0, y), axis=-1, keepdims=True)) * inv
        d = y - mu
        dd = d * d
        var = jnp.where(head0,
                        jnp.sum(jnp.where(head0, dd, 0.0), axis=-1, keepdims=True),
                        jnp.sum(jnp.where(head0, 0.0, dd), axis=-1, keepdims=True)) * inv
        yn = d * lax.rsqrt(var + LN_EPS) * gain
        o_ref[sl, :] = (yn * jax.nn.silu(g_ref[sl, :])).astype(o_ref.dtype)


def _retention(qa, ka, va, ga, gn_gain, log_gamma, batch, tm=512):
    t, width = qa.shape
    s = t // batch
    nt = s // tm
    blk = lambda b, h, i: (b * nt + i, h)
    return pl.pallas_call(
        _retention_kernel,
        grid=(batch, width // LANES, nt),
        in_specs=[pl.BlockSpec(memory_space=pltpu.SMEM),
                  pl.BlockSpec((tm, LANES), blk),
                  pl.BlockSpec((tm, LANES), blk),
                  pl.BlockSpec((tm, LANES), blk),
                  pl.BlockSpec((tm, LANES), blk),
                  pl.BlockSpec((1, LANES), lambda b, h, i: (0, h))],
        out_specs=pl.BlockSpec((tm, LANES), blk),
        out_shape=jax.ShapeDtypeStruct((t, width), BF16),
        scratch_shapes=[pltpu.VMEM((LANES, LANES), F32),
                        pltpu.VMEM((5, RET_CHUNK, LANES), F32)],
        compiler_params=_params("parallel", "parallel", "arbitrary"),
        name="retention",
    )(log_gamma, qa, ka, va, ga, gn_gain.reshape(1, width))


def _moba_select_kernel(q_ref, km_ref, qe_ref, qo_ref):
    ts = q_ref.shape[0]
    nb = km_ref.shape[0]
    qf = q_ref[...]
    km = km_ref[...]
    lane = lax.broadcasted_iota(jnp.int32, (ts, LANES), 1)
    even_data = lane < HEAD_DIM
    nt = (((1,), (1,)), ((), ()))
    blk = lax.broadcasted_iota(jnp.int32, (nb, ts), 0).astype(F32)
    col = lax.broadcasted_iota(jnp.int32, (nb, ts), 1)
    own = ((pl.program_id(2) * ts + col) // MOBA_BLOCK).astype(F32)
    past = blk < own
    place_row = lax.broadcasted_iota(jnp.int32, (nb, LANES), 0)
    place_lane = lax.broadcasted_iota(jnp.int32, (nb, LANES), 1)

    for h, out_ref in enumerate((qe_ref, qo_ref)):
        data = even_data if h == 0 else jnp.logical_not(even_data)
        qh = jnp.where(data, qf, 0.0)
        gate = lax.dot_general(km, qh, nt, preferred_element_type=F32,
                               precision=lax.Precision.HIGHEST)
        g = jnp.where(past, gate, NEG_INF)
        sel = jnp.zeros((nb, ts), F32)
        for _ in range(min(MOBA_TOPK, nb)):
            m = jnp.max(g, axis=0, keepdims=True)
            first = jnp.min(jnp.where(g == m, blk, float(nb)), axis=0, keepdims=True)
            pick = blk == first
            sel = jnp.where(pick, 1.0, sel)
            g = jnp.where(pick, NEG_INF, g)
        sel = jnp.where(past, sel, 0.0)
        sel = jnp.where(blk == own, 1.0, sel)
        spare_base = HEAD_DIM if h == 0 else 0
        place = (place_lane == place_row + spare_base).astype(BF16)
        placed = lax.dot_general(sel.astype(BF16), place, (((0,), (0,)), ((), ())),
                                 preferred_element_type=F32)
        bias = (1.0 - placed) * MASK_BIAS
        out_ref[...] = jnp.where(data, qh * (HEAD_DIM ** -0.5 * LOG2_E), bias).astype(out_ref.dtype)


def _moba_select(qb, kmean, batch, ts=1024):
    t, width = qb.shape
    s = t // batch
    nb = s // MOBA_BLOCK
    ts = min(ts, s)
    assert nb <= HEAD_DIM and ts % MOBA_BLOCK == 0
    km = kmean.reshape(batch, nb, width)
    qblk = lambda b, h, i: (b * (s // ts) + i, h)
    return pl.pallas_call(
        _moba_select_kernel,
        grid=(batch, width // LANES, s // ts),
        in_specs=[pl.BlockSpec((ts, LANES), qblk),
                  pl.BlockSpec((None, nb, LANES), lambda b, h, i: (b, 0, h))],
        out_specs=[pl.BlockSpec((ts, LANES), qblk)] * 2,
        out_shape=[jax.ShapeDtypeStruct((t, width), BF16)] * 2,
        compiler_params=_params("parallel", "parallel", "parallel"),
        name="moba_select",
    )(qb, km)


def _moba_kernel(qe_ref, qo_ref, ke_ref, ko_ref, ve_ref, vo_ref, o_ref, s_ref, m_ref, acc_ref):
    qi = pl.program_id(2)
    tq = qe_ref.shape[0]
    slab = MOBA_GROUP * MOBA_BLOCK
    nt = (((1,), (1,)), ((), ()))
    last = qi // MOBA_GROUP
    heads = ((qe_ref, ke_ref, ve_ref), (qo_ref, ko_ref, vo_ref))
    r = lax.broadcasted_iota(jnp.int32, (tq, slab), 0)
    c = lax.broadcasted_iota(jnp.int32, (tq, slab), 1)
    causal = c - r <= (qi - last * MOBA_GROUP) * MOBA_BLOCK

    def scores(groups, masked):
        for h, (q_ref, k_ref, _) in enumerate(heads):
            mx = None if masked else m_ref[h]
            for t in groups:
                off = t * slab
                sb = lax.dot_general(q_ref[...], k_ref[pl.ds(off, slab), :], nt,
                                     preferred_element_type=F32)
                if masked:
                    sb = jnp.where(causal, sb, NEG_INF)
                s_ref[h, t] = sb
                for u in range(slab // LANES):
                    part = sb[:, u * LANES:(u + 1) * LANES]
                    mx = part if mx is None else jnp.maximum(mx, part)
            m_ref[h] = mx

    def weighted_values(groups, first):
        for h, (_, _, v_ref) in enumerate(heads):
            acc = None if first else acc_ref[h]
            for t in groups:
                off = t * slab
                p = jnp.exp2(s_ref[h, t] - row_max[h]).astype(BF16)
                pv = jnp.dot(p, v_ref[pl.ds(off, slab), :], preferred_element_type=F32)
                acc = pv if acc is None else acc + pv
            acc_ref[h] = acc

    row_max = [None, None]
    for n_past in range(s_ref.shape[1]):
        @pl.when(last == n_past)
        def _(n_past=n_past):
            scores((n_past,), True)
            if n_past:
                scores(tuple(range(n_past)), False)
            for h in range(2):
                row_max[h] = jnp.max(m_ref[h], axis=-1, keepdims=True)
            weighted_values(tuple(range(n_past + 1)), True)

    lane = lax.broadcasted_iota(jnp.int32, (tq, LANES), 1)
    a0 = acc_ref[0]
    a1 = acc_ref[1]
    o0 = a0 / pltpu.roll(a0, HEAD_DIM, 1)
    o1 = a1 / pltpu.roll(a1, HEAD_DIM, 1)
    o_ref[...] = jnp.where(lane < HEAD_DIM, o0, o1).astype(o_ref.dtype)


def _moba(qbe, qbo, kbe, kbo, vbe, vbo, batch):
    t, width = qbe.shape
    s = t // batch
    tq = MOBA_BLOCK
    nq = s // tq
    nb = s // MOBA_BLOCK
    assert nb % MOBA_GROUP == 0
    qblk = lambda b, h, i: (b * nq + i, h)
    kvblk = lambda b, h, i: (b, h)
    return pl.pallas_call(
        _moba_kernel,
        grid=(batch, width // LANES, nq),
        in_specs=[pl.BlockSpec((tq, LANES), qblk)] * 2 + [pl.BlockSpec((s, LANES), kvblk)] * 4,
        out_specs=pl.BlockSpec((tq, LANES), qblk),
        out_shape=jax.ShapeDtypeStruct((t, width), BF16),
        scratch_shapes=[pltpu.VMEM((2, nb // MOBA_GROUP, tq, MOBA_GROUP * MOBA_BLOCK), F32),
                        pltpu.VMEM((2, tq, LANES), F32),
                        pltpu.VMEM((2, tq, LANES), F32)],
        compiler_params=_params("parallel", "parallel", "arbitrary"),
        name="moba",
    )(qbe, qbo, kbe, kbo, vbe, vbo)


def _out_proj_kernel(*refs, n_in):
    a_refs = refs[:n_in]
    w_ref, x_ref, g_ref, b_ref, o_ref = refs[n_in:]
    y = None
    off = 0
    for a_ref in a_refs:
        kdim = a_ref.shape[1]
        part = jnp.dot(a_ref[...], w_ref[off:off + kdim, :], preferred_element_type=F32)
        y = part if y is None else y + part
        off += kdim
    z = DEEPNORM_ALPHA * x_ref[...] + y
    o_ref[...] = _layer_norm_rows(z, g_ref[...], b_ref[...])


def _out_proj_ln(acts, w_bf16, x2d, gain, bias, tm=512):
    t, d = x2d.shape
    row = lambda i: (i, 0)
    const = lambda i: (0, 0)
    return pl.pallas_call(
        functools.partial(_out_proj_kernel, n_in=len(acts)),
        grid=(t // tm,),
        in_specs=[pl.BlockSpec((tm, a.shape[1]), row) for a in acts]
        + [pl.BlockSpec(w_bf16.shape, const),
           pl.BlockSpec((tm, d), row),
           pl.BlockSpec((1, d), const),
           pl.BlockSpec((1, d), const)],
        out_specs=pl.BlockSpec((tm, d), row),
        out_shape=jax.ShapeDtypeStruct((t, d), F32),
        compiler_params=_params("parallel"),
        name="out_proj_ln",
    )(*acts, w_bf16, x2d, gain.reshape(1, d), bias.reshape(1, d))


def _ffn_kernel(x_ref, w1_ref, w2_ref, g_ref, b_ref, o_ref, xb_ref, acc_ref):
    j = pl.program_id(1)

    @pl.when(j == 0)
    def _():
        xb_ref[...] = x_ref[...].astype(BF16)
        acc_ref[...] = jnp.zeros_like(acc_ref)

    h = jnp.dot(xb_ref[...], w1_ref[...], preferred_element_type=F32)
    h = jnp.square(jnp.maximum(h, 0.0)).astype(BF16)
    acc_ref[...] += jnp.dot(h, w2_ref[...], preferred_element_type=F32)

    @pl.when(j == pl.num_programs(1) - 1)
    def _():
        z = DEEPNORM_ALPHA * x_ref[...] + acc_ref[...]
        o_ref[...] = _layer_norm_rows(z, g_ref[...], b_ref[...])


def _ffn_ln(x2d, w1_bf16, w2_bf16, gain, bias, tm=1024, th=1024):
    t, d = x2d.shape
    hdim = w1_bf16.shape[1]
    return pl.pallas_call(
        _ffn_kernel,
        grid=(t // tm, hdim // th),
        in_specs=[pl.BlockSpec((tm, d), lambda i, j: (i, 0)),
                  pl.BlockSpec((d, th), lambda i, j: (0, j)),
                  pl.BlockSpec((th, d), lambda i, j: (j, 0)),
                  pl.BlockSpec((1, d), lambda i, j: (0, 0)),
                  pl.BlockSpec((1, d), lambda i, j: (0, 0))],
        out_specs=pl.BlockSpec((tm, d), lambda i, j: (i, 0)),
        out_shape=jax.ShapeDtypeStruct((t, d), F32),
        scratch_shapes=[pltpu.VMEM((tm, d), BF16), pltpu.VMEM((tm, d), F32)],
        compiler_params=_params("parallel", "arbitrary"),
        name="ffn_ln",
    )(x2d, w1_bf16, w2_bf16, gain.reshape(1, d), bias.reshape(1, d))


def _gmlp_kernel(x_ref, w_ref, b_ref, lng_ref, lnb_ref, ws_ref, bst_ref, wo_ref, g_ref, beta_ref,
                 o_ref, v_ref, gated_ref):
    tm = x_ref.shape[0]
    gw = v_ref.shape[1]
    c = GMLP_CHUNK
    cw = gw // GMLP_GROUPS
    xb = x_ref[...].astype(BF16)
    v = jnp.dot(xb, w_ref[:, gw:], preferred_element_type=F32) + b_ref[:, gw:]
    v_ref[...] = _layer_norm_rows(jax.nn.gelu(v), lng_ref[...], lnb_ref[...]).astype(BF16)
    r = lax.broadcasted_iota(jnp.int32, (c, c), 0)
    cc = lax.broadcasted_iota(jnp.int32, (c, c), 1)
    causal = cc <= r
    bst = bst_ref[...]
    for g in range(GMLP_GROUPS):
        cols = slice(g * cw, (g + 1) * cw)
        ws = jnp.where(causal, ws_ref[g], 0.0).astype(BF16)
        u = jax.nn.gelu(jnp.dot(xb, w_ref[:, cols], preferred_element_type=F32) + b_ref[:, cols])
        for ci in range(tm // c):
            rows = slice(ci * c, (ci + 1) * c)
            sv = jnp.dot(ws, v_ref[rows, cols], preferred_element_type=F32) + bst[:, g:g + 1]
            gated_ref[rows, cols] = (u[rows] * sv).astype(BF16)
    y = jnp.dot(gated_ref[...], wo_ref[...], preferred_element_type=F32)
    z = DEEPNORM_ALPHA * x_ref[...] + y
    o_ref[...] = _layer_norm_rows(z, g_ref[...], beta_ref[...])


def _gmlp_mixer_ln(x2d, w_uv_bf16, b_uv, ln_g, ln_b, w_s, b_s, w_out_bf16, gain, bias, tm=512):
    t, d = x2d.shape
    n = w_uv_bf16.shape[1]
    gw = n // 2
    row = lambda i: (i, 0)
    const = lambda i: (0, 0)
    return pl.pallas_call(
        _gmlp_kernel,
        grid=(t // tm,),
        in_specs=[pl.BlockSpec((tm, d), row),
                  pl.BlockSpec((d, n), const),
                  pl.BlockSpec((1, n), const),
                  pl.BlockSpec((1, gw), const),
                  pl.BlockSpec((1, gw), const),
                  pl.BlockSpec(w_s.shape, lambda i: (0, 0, 0)),
                  pl.BlockSpec((GMLP_CHUNK, GMLP_GROUPS), const),
                  pl.BlockSpec((gw, d), const),
                  pl.BlockSpec((1, d), const),
                  pl.BlockSpec((1, d), const)],
        out_specs=pl.BlockSpec((tm, d), row),
        out_shape=jax.ShapeDtypeStruct((t, d), F32),
        scratch_shapes=[pltpu.VMEM((tm, gw), BF16), pltpu.VMEM((tm, gw), BF16)],
        compiler_params=_params("parallel"),
        name="gmlp_mixer_ln",
    )(x2d, w_uv_bf16, b_uv.reshape(1, n), ln_g.reshape(1, gw), ln_b.reshape(1, gw), w_s, b_s.T,
      w_out_bf16, gain.reshape(1, d), bias.reshape(1, d))


def kernel(x, positions, ln_gain, ln_bias, mix_w_in, ret_gn_gain, mix_w_out,
           gmlp_w_uv, gmlp_b_uv, gmlp_ln_gain, gmlp_ln_bias, gmlp_w_s, gmlp_b_s,
           gmlp_w_out, ffn_w_in, ffn_w_out):
    batch, seq, d = x.shape
    depth = ln_gain.shape[0]
    h = x.reshape(batch * seq, d)
    cos_t, sin_t = _rope_tables(positions)
    log_gamma = jnp.log1p(-jnp.exp2(-5.0 - jnp.arange(N_RET_HEADS, dtype=F32)))
    for layer in range(depth):
        i = layer // 2
        if layer % 2 == 0:
            qa, ka, va, ga, qb, kbe, kbo, vbe, vbo, kmean = _in_proj(
                h, mix_w_in[i].astype(BF16), cos_t, sin_t, seq // MOBA_BLOCK)
            ya = _retention(qa, ka, va, ga, ret_gn_gain[i], log_gamma, batch)
            qbe, qbo = _moba_select(qb, kmean, batch)
            yb = _moba(qbe, qbo, kbe, kbo, vbe, vbo, batch)
            h = _out_proj_ln([ya, yb], mix_w_out[i].astype(BF16), h,
                             ln_gain[layer, 0], ln_bias[layer, 0])
        else:
            h = _gmlp_mixer_ln(h, gmlp_w_uv[i].astype(BF16), gmlp_b_uv[i], gmlp_ln_gain[i],
                               gmlp_ln_bias[i], gmlp_w_s[i], gmlp_b_s[i],
                               gmlp_w_out[i].astype(BF16), ln_gain[layer, 0], ln_bias[layer, 0])
        h = _ffn_ln(h, ffn_w_in[layer].astype(BF16), ffn_w_out[layer].astype(BF16),
                    ln_gain[layer, 1], ln_bias[layer, 1])
    return h.reshape(batch, seq, d)
```

```python
import functools

import jax
import jax.numpy as jnp
from jax import lax
from jax.experimental import pallas as pl
from jax.experimental.pallas import tpu as pltpu

HEAD_DIM = 64
N_RET_HEADS = 8
N_MOBA_HEADS = 8
RET_WIDTH = N_RET_HEADS * HEAD_DIM
MOBA_WIDTH = N_MOBA_HEADS * HEAD_DIM
RET_CHUNK = 128
MOBA_BLOCK = 256
MOBA_TOPK = 3
GMLP_CHUNK = 128
GMLP_GROUPS = 8
ROPE_THETA = 10000.0
LN_EPS = 1e-5
DEPTH = 4
DEEPNORM_ALPHA = (2 * DEPTH) ** 0.25

LANES = 128
VMEM_LIMIT_BYTES = 52 * 1024 * 1024

BF16 = jnp.bfloat16
F32 = jnp.float32
NEG_INF = float("-inf")
MOBA_GROUP = 4
MASK_BIAS = -1e30
LOG2_E = 1.4426950408889634


def _params(*sem):
    return pltpu.CompilerParams(dimension_semantics=sem, vmem_limit_bytes=VMEM_LIMIT_BYTES)


def _layer_norm_rows(z, g, b):
    mu = jnp.mean(z, axis=-1, keepdims=True)
    d = z - mu
    var = jnp.mean(d * d, axis=-1, keepdims=True)
    return d * lax.rsqrt(var + LN_EPS) * g + b


def _rope_table_kernel(pos_ref, freq_ref, cos_ref, sin_ref):
    ang = pos_ref[...] * freq_ref[...]
    lane = lax.broadcasted_iota(jnp.int32, ang.shape, 1)
    first_half = (lane % HEAD_DIM) < (HEAD_DIM // 2)
    cos_ref[...] = jnp.cos(ang)
    sin_ref[...] = jnp.where(first_half, -jnp.sin(ang), jnp.sin(ang))


def _rope_tables(positions):
    t = positions.size
    tm = min(t, 2048)
    pos = positions.reshape(t, 1).astype(F32)
    inv_freq = ROPE_THETA ** (-jnp.arange(0, HEAD_DIM, 2, dtype=F32) / HEAD_DIM)
    freq = jnp.tile(inv_freq, LANES // (HEAD_DIM // 2)).reshape(1, LANES)
    return pl.pallas_call(
        _rope_table_kernel,
        grid=(t // tm,),
        in_specs=[pl.BlockSpec((tm, 1), lambda i: (i, 0)),
                  pl.BlockSpec((1, LANES), lambda i: (0, 0))],
        out_specs=[pl.BlockSpec((tm, LANES), lambda i: (i, 0))] * 2,
        out_shape=[jax.ShapeDtypeStruct((t, LANES), F32)] * 2,
        compiler_params=_params("parallel"),
        name="rope_tables",
    )(pos, freq)


def _in_proj_kernel(x_ref, w_ref, cos_ref, sin_ref,
                    qa_ref, ka_ref, va_ref, ga_ref, qb_ref, kbe_ref, kbo_ref, vbe_ref, vbo_ref,
                    km_ref, *, n_blocks):
    tm = x_ref.shape[0]
    width = RET_WIDTH
    xb = x_ref[...].astype(BF16)
    reps = width // LANES
    cos = jnp.concatenate([cos_ref[...]] * reps, axis=1)
    sin = jnp.concatenate([sin_ref[...]] * reps, axis=1)
    lane = lax.broadcasted_iota(jnp.int32, (tm, width), 1)
    first_half = (lane % HEAD_DIM) < (HEAD_DIM // 2)

    def proj(g):
        return jnp.dot(xb, w_ref[:, g * width:(g + 1) * width], preferred_element_type=F32)

    def rope(t):
        partner = jnp.where(first_half,
                            pltpu.roll(t, width - HEAD_DIM // 2, 1),
                            pltpu.roll(t, HEAD_DIM // 2, 1))
        return t * cos + partner * sin

    qa_ref[...] = rope(proj(0)).astype(qa_ref.dtype)
    ka_ref[...] = rope(proj(1)).astype(ka_ref.dtype)
    va_ref[...] = proj(2).astype(va_ref.dtype)
    ga_ref[...] = proj(3).astype(ga_ref.dtype)
    qb_ref[...] = rope(proj(4)).astype(qb_ref.dtype)
    kb = rope(proj(5))
    vb = proj(6)
    pair_lane = lane % LANES
    even_data = pair_lane < HEAD_DIM
    row = lax.broadcasted_iota(jnp.int32, (tm, width), 0)
    block = ((pl.program_id(0) * tm + row) // MOBA_BLOCK) % n_blocks
    onehot_e = (pair_lane - HEAD_DIM == block).astype(F32)
    onehot_o = (pair_lane == block).astype(F32)
    kbe_ref[...] = jnp.where(even_data, kb, onehot_e).astype(kbe_ref.dtype)
    kbo_ref[...] = jnp.where(even_data, onehot_o, kb).astype(kbo_ref.dtype)
    vbe_ref[...] = jnp.where(even_data, vb, 1.0).astype(vbe_ref.dtype)
    vbo_ref[...] = jnp.where(even_data, 1.0, vb).astype(vbo_ref.dtype)
    for blk in range(tm // MOBA_BLOCK):
        km_ref[blk] = jnp.mean(kb[blk * MOBA_BLOCK:(blk + 1) * MOBA_BLOCK], axis=0, keepdims=True)


def _in_proj(x2d, w_bf16, cos_t, sin_t, n_blocks, tm=512):
    t, d = x2d.shape
    n = w_bf16.shape[1]
    width = RET_WIDTH
    assert n_blocks <= HEAD_DIM and tm % MOBA_BLOCK == 0
    row = lambda i: (i, 0)
    out_dtypes = [BF16, F32, BF16, F32, F32, BF16, BF16, BF16, BF16]
    return pl.pallas_call(
        functools.partial(_in_proj_kernel, n_blocks=n_blocks),
        grid=(t // tm,),
        in_specs=[pl.BlockSpec((tm, d), row),
                  pl.BlockSpec((d, n), lambda i: (0, 0)),
                  pl.BlockSpec((tm, LANES), row),
                  pl.BlockSpec((tm, LANES), row)],
        out_specs=[pl.BlockSpec((tm, width), row)] * len(out_dtypes)
        + [pl.BlockSpec((tm // MOBA_BLOCK, 1, width), lambda i: (i, 0, 0))],
        out_shape=[jax.ShapeDtypeStruct((t, width), dt) for dt in out_dtypes]
        + [jax.ShapeDtypeStruct((t // MOBA_BLOCK, 1, width), F32)],
        compiler_params=_params("parallel"),
        name="in_proj",
    )(x2d, w_bf16, cos_t, sin_t)


def _retention_kernel(lg_ref, q_ref, k_ref, v_ref, g_ref, gain_ref, o_ref, state_ref, decay_ref):
    hp = pl.program_id(1)
    c = RET_CHUNK
    tm = q_ref.shape[0]
    lane = lax.broadcasted_iota(jnp.int32, (c, LANES), 1)
    row = lax.broadcasted_iota(jnp.int32, (c, LANES), 0)
    head0 = lane < HEAD_DIM

    @pl.when(pl.program_id(2) == 0)
    def _():
        state_ref[...] = jnp.zeros_like(state_ref)
        lg0 = lg_ref[2 * hp]
        lg1 = lg_ref[2 * hp + 1]
        lg_lane = jnp.where(head0, lg0, lg1)
        idx = row.astype(F32)
        diff = (row - lane).astype(F32)
        dpos = jnp.maximum(diff, 0.0)
        decay_ref[0] = jnp.exp(lg_lane * (idx + 1.0))
        decay_ref[1] = jnp.exp(lg_lane * (c - 1.0 - idx))
        decay_ref[2] = jnp.exp(lg_lane * float(c))
        decay_ref[3] = jnp.where(diff >= 0, jnp.exp(lg0 * dpos), 0.0)
        decay_ref[4] = jnp.where(diff >= 0, jnp.exp(lg1 * dpos), 0.0)

    xi = decay_ref[0]
    zeta = decay_ref[1]
    chunk_decay = decay_ref[2]
    decay0 = decay_ref[3]
    decay1 = decay_ref[4]
    same_head = (row < HEAD_DIM) == head0
    gain = gain_ref[...]
    nt = (((1,), (1,)), ((), ()))
    tn = (((0,), (0,)), ((), ()))

    for ci in range(tm // c):
        sl = slice(ci * c, (ci + 1) * c)
        q = q_ref[sl, :]
        kf = k_ref[sl, :] * (HEAD_DIM ** -0.5)
        v = v_ref[sl, :]
        kb = kf.astype(BF16)
        zero = jnp.zeros_like(q)
        q0 = jnp.where(head0, q, zero)
        q1 = jnp.where(head0, zero, q)
        s0 = lax.dot_general(q0, kb, nt, preferred_element_type=F32) * decay0
        s1 = lax.dot_general(q1, kb, nt, preferred_element_type=F32) * decay1
        inner = jnp.where(head0,
                          jnp.dot(s0.astype(BF16), v, preferred_element_type=F32),
                          jnp.dot(s1.astype(BF16), v, preferred_element_type=F32))
        state = state_ref[...]
        cross = jnp.dot(q, state.astype(BF16), preferred_element_type=F32) * xi
        kv = lax.dot_general((kf * zeta).astype(BF16), v, tn, preferred_element_type=F32)
        state_ref[...] = state * chunk_decay + jnp.where(same_head, kv, 0.0)

        y = inner + cross
        inv = 1.0 / HEAD_DIM
        mu = jnp.where(head0,
                       jnp.sum(jnp.where(head0, y, 0.0), axis=-1, keepdims=True),
                       jnp.sum(jnp.where(head0, 0.0, y), axis=-1, keepdims=True)) * inv
        d = y - mu
        dd = d * d
        var = jnp.where(head0,
                        jnp.sum(jnp.where(head0, dd, 0.0), axis=-1, keepdims=True),
                        jnp.sum(jnp.where(head0, 0.0, dd), axis=-1, keepdims=True)) * inv
        yn = d * lax.rsqrt(var + LN_EPS) * gain
        o_ref[sl, :] = (yn * jax.nn.silu(g_ref[sl, :])).astype(o_ref.dtype)


def _retention(qa, ka, va, ga, gn_gain, log_gamma, batch, tm=512):
    t, width = qa.shape
    s = t // batch
    nt = s // tm
    blk = lambda b, h, i: (b * nt + i, h)
    return pl.pallas_call(
        _retention_kernel,
        grid=(batch, width // LANES, nt),
        in_specs=[pl.BlockSpec(memory_space=pltpu.SMEM),
                  pl.BlockSpec((tm, LANES), blk),
                  pl.BlockSpec((tm, LANES), blk),
                  pl.BlockSpec((tm, LANES), blk),
                  pl.BlockSpec((tm, LANES), blk),
                  pl.BlockSpec((1, LANES), lambda b, h, i: (0, h))],
        out_specs=pl.BlockSpec((tm, LANES), blk),
        out_shape=jax.ShapeDtypeStruct((t, width), BF16),
        scratch_shapes=[pltpu.VMEM((LANES, LANES), F32),
                        pltpu.VMEM((5, RET_CHUNK, LANES), F32)],
        compiler_params=_params("parallel", "parallel", "arbitrary"),
        name="retention",
    )(log_gamma, qa, ka, va, ga, gn_gain.reshape(1, width))


def _moba_select_kernel(q_ref, km_ref, qe_ref, qo_ref):
    ts = q_ref.shape[0]
    nb = km_ref.shape[0]
    qf = q_ref[...]
    km = km_ref[...]
    lane = lax.broadcasted_iota(jnp.int32, (ts, LANES), 1)
    even_data = lane < HEAD_DIM
    nt = (((1,), (1,)), ((), ()))
    blk = lax.broadcasted_iota(jnp.int32, (nb, ts), 0).astype(F32)
    col = lax.broadcasted_iota(jnp.int32, (nb, ts), 1)
    own = ((pl.program_id(2) * ts + col) // MOBA_BLOCK).astype(F32)
    past = blk < own
    place_row = lax.broadcasted_iota(jnp.int32, (nb, LANES), 0)
    place_lane = lax.broadcasted_iota(jnp.int32, (nb, LANES), 1)

    for h, out_ref in enumerate((qe_ref, qo_ref)):
        data = even_data if h == 0 else jnp.logical_not(even_data)
        qh = jnp.where(data, qf, 0.0)
        gate = lax.dot_general(km, qh, nt, preferred_element_type=F32,
                               precision=lax.Precision.HIGHEST)
        g = jnp.where(past, gate, NEG_INF)
        sel = jnp.zeros((nb, ts), F32)
        for _ in range(min(MOBA_TOPK, nb)):
            m = jnp.max(g, axis=0, keepdims=True)
            first = jnp.min(jnp.where(g == m, blk, float(nb)), axis=0, keepdims=True)
            pick = blk == first
            sel = jnp.where(pick, 1.0, sel)
            g = jnp.where(pick, NEG_INF, g)
        sel = jnp.where(past, sel, 0.0)
        sel = jnp.where(blk == own, 1.0, sel)
        spare_base = HEAD_DIM if h == 0 else 0
        place = (place_lane == place_row + spare_base).astype(BF16)
        placed = lax.dot_general(sel.astype(BF16), place, (((0,), (0,)), ((), ())),
                                 preferred_element_type=F32)
        bias = (1.0 - placed) * MASK_BIAS
        out_ref[...] = jnp.where(data, qh * (HEAD_DIM ** -0.5 * LOG2_E), bias).astype(out_ref.dtype)


def _moba_select(qb, kmean, batch, ts=1024):
    t, width = qb.shape
    s = t // batch
    nb = s // MOBA_BLOCK
    ts = min(ts, s)
    assert nb <= HEAD_DIM and ts % MOBA_BLOCK == 0
    km = kmean.reshape(batch, nb, width)
    qblk = lambda b, h, i: (b * (s // ts) + i, h)
    return pl.pallas_call(
        _moba_select_kernel,
        grid=(batch, width // LANES, s // ts),
        in_specs=[pl.BlockSpec((ts, LANES), qblk),
                  pl.BlockSpec((None, nb, LANES), lambda b, h, i: (b, 0, h))],
        out_specs=[pl.BlockSpec((ts, LANES), qblk)] * 2,
        out_shape=[jax.ShapeDtypeStruct((t, width), BF16)] * 2,
        compiler_params=_params("parallel", "parallel", "parallel"),
        name="moba_select",
    )(qb, km)


def _moba_kernel(qe_ref, qo_ref, ke_ref, ko_ref, ve_ref, vo_ref, o_ref, s_ref, m_ref, acc_ref):
    qi = pl.program_id(2)
    tq = qe_ref.shape[0]
    slab = MOBA_GROUP * MOBA_BLOCK
    nt = (((1,), (1,)), ((), ()))
    last = qi // MOBA_GROUP
    heads = ((qe_ref, ke_ref, ve_ref), (qo_ref, ko_ref, vo_ref))
    r = lax.broadcasted_iota(jnp.int32, (tq, slab), 0)
    c = lax.broadcasted_iota(jnp.int32, (tq, slab), 1)
    causal = c - r <= (qi - last * MOBA_GROUP) * MOBA_BLOCK

    def scores(groups, masked):
        for h, (q_ref, k_ref, _) in enumerate(heads):
            mx = None if masked else m_ref[h]
            for t in groups:
                off = t * slab
                sb = lax.dot_general(q_ref[...], k_ref[pl.ds(off, slab), :], nt,
                                     preferred_element_type=F32)
                if masked:
                    sb = jnp.where(causal, sb, NEG_INF)
                s_ref[h, t] = sb
                for u in range(slab // LANES):
                    part = sb[:, u * LANES:(u + 1) * LANES]
                    mx = part if mx is None else jnp.maximum(mx, part)
            m_ref[h] = mx

    def weighted_values(groups, first):
        for h, (_, _, v_ref) in enumerate(heads):
            acc = None if first else acc_ref[h]
            for t in groups:
                off = t * slab
                p = jnp.exp2(s_ref[h, t] - row_max[h]).astype(BF16)
                pv = jnp.dot(p, v_ref[pl.ds(off, slab), :], preferred_element_type=F32)
                acc = pv if acc is None else acc + pv
            acc_ref[h] = acc

    row_max = [None, None]
    for n_past in range(s_ref.shape[1]):
        @pl.when(last == n_past)
        def _(n_past=n_past):
            scores((n_past,), True)
            if n_past:
                scores(tuple(range(n_past)), False)
            for h in range(2):
                row_max[h] = jnp.max(m_ref[h], axis=-1, keepdims=True)
            weighted_values(tuple(range(n_past + 1)), True)

    lane = lax.broadcasted_iota(jnp.int32, (tq, LANES), 1)
    a0 = acc_ref[0]
    a1 = acc_ref[1]
    o0 = a0 / pltpu.roll(a0, HEAD_DIM, 1)
    o1 = a1 / pltpu.roll(a1, HEAD_DIM, 1)
    o_ref[...] = jnp.where(lane < HEAD_DIM, o0, o1).astype(o_ref.dtype)


def _moba(qbe, qbo, kbe, kbo, vbe, vbo, batch):
    t, width = qbe.shape
    s = t // batch
    tq = MOBA_BLOCK
    nq = s // tq
    nb = s // MOBA_BLOCK
    assert nb % MOBA_GROUP == 0
    qblk = lambda b, h, i: (b * nq + i, h)
    kvblk = lambda b, h, i: (b, h)
    return pl.pallas_call(
        _moba_kernel,
        grid=(batch, width // LANES, nq),
        in_specs=[pl.BlockSpec((tq, LANES), qblk)] * 2 + [pl.BlockSpec((s, LANES), kvblk)] * 4,
        out_specs=pl.BlockSpec((tq, LANES), qblk),
        out_shape=jax.ShapeDtypeStruct((t, width), BF16),
        scratch_shapes=[pltpu.VMEM((2, nb // MOBA_GROUP, tq, MOBA_GROUP * MOBA_BLOCK), F32),
                        pltpu.VMEM((2, tq, LANES), F32),
                        pltpu.VMEM((2, tq, LANES), F32)],
        compiler_params=_params("parallel", "parallel", "arbitrary"),
        name="moba",
    )(qbe, qbo, kbe, kbo, vbe, vbo)


def _out_proj_kernel(*refs, n_in):
    a_refs = refs[:n_in]
    w_ref, x_ref, g_ref, b_ref, o_ref = refs[n_in:]
    y = None
    off = 0
    for a_ref in a_refs:
        kdim = a_ref.shape[1]
        part = jnp.dot(a_ref[...], w_ref[off:off + kdim, :], preferred_element_type=F32)
        y = part if y is None else y + part
        off += kdim
    z = DEEPNORM_ALPHA * x_ref[...] + y
    o_ref[...] = _layer_norm_rows(z, g_ref[...], b_ref[...])


def _out_proj_ln(acts, w_bf16, x2d, gain, bias, tm=512):
    t, d = x2d.shape
    row = lambda i: (i, 0)
    const = lambda i: (0, 0)
    return pl.pallas_call(
        functools.partial(_out_proj_kernel, n_in=len(acts)),
        grid=(t // tm,),
        in_specs=[pl.BlockSpec((tm, a.shape[1]), row) for a in acts]
        + [pl.BlockSpec(w_bf16.shape, const),
           pl.BlockSpec((tm, d), row),
           pl.BlockSpec((1, d), const),
           pl.BlockSpec((1, d), const)],
        out_specs=pl.BlockSpec((tm, d), row),
        out_shape=jax.ShapeDtypeStruct((t, d), F32),
        compiler_params=_params("parallel"),
        name="out_proj_ln",
    )(*acts, w_bf16, x2d, gain.reshape(1, d), bias.reshape(1, d))


def _ffn_kernel(x_ref, w1_ref, w2_ref, g_ref, b_ref, o_ref, *, row_split, th):
    tm = x_ref.shape[0]
    hdim = w1_ref.shape[1]
    rows = tm // row_split
    for part in range(row_split):
        sl = slice(part * rows, (part + 1) * rows)
        x = x_ref[sl, :]
        xb = x.astype(BF16)
        y = None
        for c in range(hdim // th):
            h = jnp.dot(xb, w1_ref[:, c * th:(c + 1) * th], preferred_element_type=F32)
            h = jnp.square(jnp.maximum(h, 0.0)).astype(BF16)
            yc = jnp.dot(h, w2_ref[c * th:(c + 1) * th, :], preferred_element_type=F32)
            y = yc if y is None else y + yc
        z = DEEPNORM_ALPHA * x + y
        o_ref[sl, :] = _layer_norm_rows(z, g_ref[...], b_ref[...])


def _ffn_ln(x2d, w1_bf16, w2_bf16, gain, bias, tm=1024, th=1024, row_split=2):
    t, d = x2d.shape
    hdim = w1_bf16.shape[1]
    row = lambda i: (i, 0)
    const = lambda i: (0, 0)
    resident = pl.Buffered(1)
    return pl.pallas_call(
        functools.partial(_ffn_kernel, row_split=row_split, th=th),
        grid=(t // tm,),
        in_specs=[pl.BlockSpec((tm, d), row),
                  pl.BlockSpec((d, hdim), const, pipeline_mode=resident),
                  pl.BlockSpec((hdim, d), const, pipeline_mode=resident),
                  pl.BlockSpec((1, d), const),
                  pl.BlockSpec((1, d), const)],
        out_specs=pl.BlockSpec((tm, d), row),
        out_shape=jax.ShapeDtypeStruct((t, d), F32),
        compiler_params=_params("parallel"),
        name="ffn_ln",
    )(x2d, w1_bf16, w2_bf16, gain.reshape(1, d), bias.reshape(1, d))


def _gmlp_kernel(x_ref, w_ref, b_ref, lng_ref, lnb_ref, ws_ref, bst_ref, wo_ref, g_ref, beta_ref,
                 o_ref, v_ref, gated_ref):
    tm = x_ref.shape[0]
    gw = v_ref.shape[1]
    c = GMLP_CHUNK
    cw = gw // GMLP_GROUPS
    xb = x_ref[...].astype(BF16)
    v = jnp.dot(xb, w_ref[:, gw:], preferred_element_type=F32) + b_ref[:, gw:]
    v_ref[...] = _layer_norm_rows(jax.nn.gelu(v), lng_ref[...], lnb_ref[...]).astype(BF16)
    r = lax.broadcasted_iota(jnp.int32, (c, c), 0)
    cc = lax.broadcasted_iota(jnp.int32, (c, c), 1)
    causal = cc <= r
    bst = bst_ref[...]
    for g in range(GMLP_GROUPS):
        cols = slice(g * cw, (g + 1) * cw)
        ws = jnp.where(causal, ws_ref[g], 0.0).astype(BF16)
        u = jax.nn.gelu(jnp.dot(xb, w_ref[:, cols], preferred_element_type=F32) + b_ref[:, cols])
        for ci in range(tm // c):
            rows = slice(ci * c, (ci + 1) * c)
            sv = jnp.dot(ws, v_ref[rows, cols], preferred_element_type=F32) + bst[:, g:g + 1]
            gated_ref[rows, cols] = (u[rows] * sv).astype(BF16)
    y = jnp.dot(gated_ref[...], wo_ref[...], preferred_element_type=F32)
    z = DEEPNORM_ALPHA * x_ref[...] + y
    o_ref[...] = _layer_norm_rows(z, g_ref[...], beta_ref[...])


def _gmlp_mixer_ln(x2d, w_uv_bf16, b_uv, ln_g, ln_b, w_s, b_s, w_out_bf16, gain, bias, tm=512):
    t, d = x2d.shape
    n = w_uv_bf16.shape[1]
    gw = n // 2
    row = lambda i: (i, 0)
    const = lambda i: (0, 0)
    return pl.pallas_call(
        _gmlp_kernel,
        grid=(t // tm,),
        in_specs=[pl.BlockSpec((tm, d), row),
                  pl.BlockSpec((d, n), const),
                  pl.BlockSpec((1, n), const),
                  pl.BlockSpec((1, gw), const),
                  pl.BlockSpec((1, gw), const),
                  pl.BlockSpec(w_s.shape, lambda i: (0, 0, 0)),
                  pl.BlockSpec((GMLP_CHUNK, GMLP_GROUPS), const),
                  pl.BlockSpec((gw, d), const),
                  pl.BlockSpec((1, d), const),
                  pl.BlockSpec((1, d), const)],
        out_specs=pl.BlockSpec((tm, d), row),
        out_shape=jax.ShapeDtypeStruct((t, d), F32),
        scratch_shapes=[pltpu.VMEM((tm, gw), BF16), pltpu.VMEM((tm, gw), BF16)],
        compiler_params=_params("parallel"),
        name="gmlp_mixer_ln",
    )(x2d, w_uv_bf16, b_uv.reshape(1, n), ln_g.reshape(1, gw), ln_b.reshape(1, gw), w_s, b_s.T,
      w_out_bf16, gain.reshape(1, d), bias.reshape(1, d))


def kernel(x, positions, ln_gain, ln_bias, mix_w_in, ret_gn_gain, mix_w_out,
           gmlp_w_uv, gmlp_b_uv, gmlp_ln_gain, gmlp_ln_bias, gmlp_w_s, gmlp_b_s,
           gmlp_w_out, ffn_w_in, ffn_w_out):
    batch, seq, d = x.shape
    depth = ln_gain.shape[0]
    h = x.reshape(batch * seq, d)
    cos_t, sin_t = _rope_tables(positions)
    log_gamma = jnp.log1p(-jnp.exp2(-5.0 - jnp.arange(N_RET_HEADS, dtype=F32)))
    for layer in range(depth):
        i = layer // 2
        if layer % 2 == 0:
            qa, ka, va, ga, qb, kbe, kbo, vbe, vbo, kmean = _in_proj(
                h, mix_w_in[i].astype(BF16), cos_t, sin_t, seq // MOBA_BLOCK)
            ya = _retention(qa, ka, va, ga, ret_gn_gain[i], log_gamma, batch)
            qbe, qbo = _moba_select(qb, kmean, batch)
            yb = _moba(qbe, qbo, kbe, kbo, vbe, vbo, batch)
            h = _out_proj_ln([ya, yb], mix_w_out[i].astype(BF16), h,
                             ln_gain[layer, 0], ln_bias[layer, 0])
        else:
            h = _gmlp_mixer_ln(h, gmlp_w_uv[i].astype(BF16), gmlp_b_uv[i], gmlp_ln_gain[i],
                               gmlp_ln_bias[i], gmlp_w_s[i], gmlp_b_s[i],
                               gmlp_w_out[i].astype(BF16), ln_gain[layer, 0], ln_bias[layer, 0])
        h = _ffn_ln(h, ffn_w_in[layer].astype(BF16), ffn_w_out[layer].astype(BF16),
                    ln_gain[layer, 1], ln_bias[layer, 1])
    return h.reshape(batch, seq, d)
```

```python
import functools

import jax
import jax.numpy as jnp
from jax import lax
from jax.experimental import pallas as pl
from jax.experimental.pallas import tpu as pltpu

HEAD_DIM = 64
N_RET_HEADS = 8
N_MOBA_HEADS = 8
RET_WIDTH = N_RET_HEADS * HEAD_DIM
MOBA_WIDTH = N_MOBA_HEADS * HEAD_DIM
RET_CHUNK = 128
MOBA_BLOCK = 256
MOBA_TOPK = 3
GMLP_CHUNK = 128
GMLP_GROUPS = 8
ROPE_THETA = 10000.0
LN_EPS = 1e-5
DEPTH = 4
DEEPNORM_ALPHA = (2 * DEPTH) ** 0.25

LANES = 128
VMEM_LIMIT_BYTES = 52 * 1024 * 1024

BF16 = jnp.bfloat16
F32 = jnp.float32
NEG_INF = float("-inf")
MOBA_GROUP = 4
MASK_BIAS = -1e30
LOG2_E = 1.4426950408889634


def _params(*sem):
    return pltpu.CompilerParams(dimension_semantics=sem, vmem_limit_bytes=VMEM_LIMIT_BYTES)


def _layer_norm_rows(z, g, b):
    mu = jnp.mean(z, axis=-1, keepdims=True)
    d = z - mu
    var = jnp.mean(d * d, axis=-1, keepdims=True)
    return d * lax.rsqrt(var + LN_EPS) * g + b


def _rope_table_kernel(pos_ref, freq_ref, cos_ref, sin_ref):
    ang = pos_ref[...] * freq_ref[...]
    lane = lax.broadcasted_iota(jnp.int32, ang.shape, 1)
    first_half = (lane % HEAD_DIM) < (HEAD_DIM // 2)
    cos_ref[...] = jnp.cos(ang)
    sin_ref[...] = jnp.where(first_half, -jnp.sin(ang), jnp.sin(ang))


def _rope_tables(positions):
    t = positions.size
    tm = min(t, 2048)
    pos = positions.reshape(t, 1).astype(F32)
    inv_freq = ROPE_THETA ** (-jnp.arange(0, HEAD_DIM, 2, dtype=F32) / HEAD_DIM)
    freq = jnp.tile(inv_freq, LANES // (HEAD_DIM // 2)).reshape(1, LANES)
    return pl.pallas_call(
        _rope_table_kernel,
        grid=(t // tm,),
        in_specs=[pl.BlockSpec((tm, 1), lambda i: (i, 0)),
                  pl.BlockSpec((1, LANES), lambda i: (0, 0))],
        out_specs=[pl.BlockSpec((tm, LANES), lambda i: (i, 0))] * 2,
        out_shape=[jax.ShapeDtypeStruct((t, LANES), F32)] * 2,
        compiler_params=_params("parallel"),
        name="rope_tables",
    )(pos, freq)


def _in_proj_kernel(x_ref, w_ref, cos_ref, sin_ref,
                    qa_ref, ka_ref, va_ref, ga_ref, qb_ref, kbe_ref, kbo_ref, vbe_ref, vbo_ref,
                    km_ref, *, n_blocks):
    tm = x_ref.shape[0]
    width = RET_WIDTH
    xb = x_ref[...].astype(BF16)
    reps = width // LANES
    cos = jnp.concatenate([cos_ref[...]] * reps, axis=1)
    sin = jnp.concatenate([sin_ref[...]] * reps, axis=1)
    lane = lax.broadcasted_iota(jnp.int32, (tm, width), 1)
    first_half = (lane % HEAD_DIM) < (HEAD_DIM // 2)

    def proj(g):
        return jnp.dot(xb, w_ref[:, g * width:(g + 1) * width], preferred_element_type=F32)

    def rope(t):
        partner = jnp.where(first_half,
                            pltpu.roll(t, width - HEAD_DIM // 2, 1),
                            pltpu.roll(t, HEAD_DIM // 2, 1))
        return t * cos + partner * sin

    qa_ref[...] = rope(proj(0)).astype(qa_ref.dtype)
    ka_ref[...] = rope(proj(1)).astype(ka_ref.dtype)
    va_ref[...] = proj(2).astype(va_ref.dtype)
    ga_ref[...] = proj(3).astype(ga_ref.dtype)
    def store_pairs(ref, val):
        for p in range(width // LANES):
            ref[p] = val[:, p * LANES:(p + 1) * LANES].astype(ref.dtype)

    store_pairs(qb_ref, rope(proj(4)))
    kb = rope(proj(5))
    vb = proj(6)
    pair_lane = lane % LANES
    even_data = pair_lane < HEAD_DIM
    row = lax.broadcasted_iota(jnp.int32, (tm, width), 0)
    block = ((pl.program_id(0) * tm + row) // MOBA_BLOCK) % n_blocks
    onehot_e = (pair_lane - HEAD_DIM == block).astype(F32)
    onehot_o = (pair_lane == block).astype(F32)
    store_pairs(kbe_ref, jnp.where(even_data, kb, onehot_e))
    store_pairs(kbo_ref, jnp.where(even_data, onehot_o, kb))
    store_pairs(vbe_ref, jnp.where(even_data, vb, 1.0))
    store_pairs(vbo_ref, jnp.where(even_data, 1.0, vb))
    for blk in range(tm // MOBA_BLOCK):
        km_ref[blk] = jnp.mean(kb[blk * MOBA_BLOCK:(blk + 1) * MOBA_BLOCK], axis=0, keepdims=True)


def _in_proj(x2d, w_bf16, cos_t, sin_t, n_blocks, tm=512):
    t, d = x2d.shape
    n = w_bf16.shape[1]
    width = RET_WIDTH
    assert n_blocks <= HEAD_DIM and tm % MOBA_BLOCK == 0
    row = lambda i: (i, 0)
    pairs = width // LANES
    row_dtypes = [BF16, F32, BF16, F32]
    pair_dtypes = [F32, BF16, BF16, BF16, BF16]
    return pl.pallas_call(
        functools.partial(_in_proj_kernel, n_blocks=n_blocks),
        grid=(t // tm,),
        in_specs=[pl.BlockSpec((tm, d), row),
                  pl.BlockSpec((d, n), lambda i: (0, 0)),
                  pl.BlockSpec((tm, LANES), row),
                  pl.BlockSpec((tm, LANES), row)],
        out_specs=[pl.BlockSpec((tm, width), row)] * len(row_dtypes)
        + [pl.BlockSpec((pairs, tm, LANES), lambda i: (0, i, 0))] * len(pair_dtypes)
        + [pl.BlockSpec((tm // MOBA_BLOCK, 1, width), lambda i: (i, 0, 0))],
        out_shape=[jax.ShapeDtypeStruct((t, width), dt) for dt in row_dtypes]
        + [jax.ShapeDtypeStruct((pairs, t, LANES), dt) for dt in pair_dtypes]
        + [jax.ShapeDtypeStruct((t // MOBA_BLOCK, 1, width), F32)],
        compiler_params=_params("parallel"),
        name="in_proj",
    )(x2d, w_bf16, cos_t, sin_t)


def _retention_kernel(lg_ref, q_ref, k_ref, v_ref, g_ref, gain_ref, o_ref, state_ref, decay_ref):
    c = RET_CHUNK
    tm, width = q_ref.shape
    lane = lax.broadcasted_iota(jnp.int32, (c, LANES), 1)
    row = lax.broadcasted_iota(jnp.int32, (c, LANES), 0)
    head0 = lane < HEAD_DIM
    same_head = (row < HEAD_DIM) == head0
    nt = (((1,), (1,)), ((), ()))
    tn = (((0,), (0,)), ((), ()))

    @pl.when(pl.program_id(1) == 0)
    def _():
        state_ref[...] = jnp.zeros_like(state_ref)
        idx = row.astype(F32)
        diff = (row - lane).astype(F32)
        dpos = jnp.maximum(diff, 0.0)
        for hp in range(width // LANES):
            lg0 = lg_ref[2 * hp]
            lg1 = lg_ref[2 * hp + 1]
            lg_lane = jnp.where(head0, lg0, lg1)
            decay_ref[hp, 0] = jnp.exp(lg_lane * (idx + 1.0))
            decay_ref[hp, 1] = jnp.exp(lg_lane * (c - 1.0 - idx))
            decay_ref[hp, 2] = jnp.exp(lg_lane * float(c))
            decay_ref[hp, 3] = jnp.where(diff >= 0, jnp.exp(lg0 * dpos), 0.0)
            decay_ref[hp, 4] = jnp.where(diff >= 0, jnp.exp(lg1 * dpos), 0.0)

    for hp in range(width // LANES):
        cols = slice(hp * LANES, (hp + 1) * LANES)
        xi = decay_ref[hp, 0]
        zeta = decay_ref[hp, 1]
        chunk_decay = decay_ref[hp, 2]
        decay0 = decay_ref[hp, 3]
        decay1 = decay_ref[hp, 4]
        gain = gain_ref[:, cols]
        for ci in range(tm // c):
            sl = slice(ci * c, (ci + 1) * c)
            q = q_ref[sl, cols]
            kf = k_ref[sl, cols] * (HEAD_DIM ** -0.5)
            v = v_ref[sl, cols]
            kb = kf.astype(BF16)
            zero = jnp.zeros_like(q)
            q0 = jnp.where(head0, q, zero)
            q1 = jnp.where(head0, zero, q)
            s0 = lax.dot_general(q0, kb, nt, preferred_element_type=F32) * decay0
            s1 = lax.dot_general(q1, kb, nt, preferred_element_type=F32) * decay1
            inner = jnp.where(head0,
                              jnp.dot(s0.astype(BF16), v, preferred_element_type=F32),
                              jnp.dot(s1.astype(BF16), v, preferred_element_type=F32))
            state = state_ref[hp]
            cross = jnp.dot(q, state.astype(BF16), preferred_element_type=F32) * xi
            kv = lax.dot_general((kf * zeta).astype(BF16), v, tn, preferred_element_type=F32)
            state_ref[hp] = state * chunk_decay + jnp.where(same_head, kv, 0.0)

            y = inner + cross
            inv = 1.0 / HEAD_DIM
            mu = jnp.where(head0,
                           jnp.sum(jnp.where(head0, y, 0.0), axis=-1, keepdims=True),
                           jnp.sum(jnp.where(head0, 0.0, y), axis=-1, keepdims=True)) * inv
            d = y - mu
            dd = d * d
            var = jnp.where(head0,
                            jnp.sum(jnp.where(head0, dd, 0.0), axis=-1, keepdims=True),
                            jnp.sum(jnp.where(head0, 0.0, dd), axis=-1, keepdims=True)) * inv
            yn = d * lax.rsqrt(var + LN_EPS) * gain
            o_ref[sl, cols] = (yn * jax.nn.silu(g_ref[sl, cols])).astype(o_ref.dtype)


def _retention(qa, ka, va, ga, gn_gain, log_gamma, batch, tm=512):
    t, width = qa.shape
    s = t // batch
    nt = s // tm
    pairs = width // LANES
    blk = lambda b, i: (b * nt + i, 0)
    return pl.pallas_call(
        _retention_kernel,
        grid=(batch, nt),
        in_specs=[pl.BlockSpec(memory_space=pltpu.SMEM)]
        + [pl.BlockSpec((tm, width), blk)] * 4
        + [pl.BlockSpec((1, width), lambda b, i: (0, 0))],
        out_specs=pl.BlockSpec((tm, width), blk),
        out_shape=jax.ShapeDtypeStruct((t, width), BF16),
        scratch_shapes=[pltpu.VMEM((pairs, LANES, LANES), F32),
                        pltpu.VMEM((pairs, 5, RET_CHUNK, LANES), F32)],
        compiler_params=_params("parallel", "arbitrary"),
        name="retention",
    )(log_gamma, qa, ka, va, ga, gn_gain.reshape(1, width))


def _moba_select_kernel(q_ref, km_ref, qe_ref, qo_ref):
    ts = q_ref.shape[0]
    nb = km_ref.shape[0]
    qf = q_ref[...]
    km = km_ref[...]
    lane = lax.broadcasted_iota(jnp.int32, (ts, LANES), 1)
    even_data = lane < HEAD_DIM
    nt = (((1,), (1,)), ((), ()))
    blk = lax.broadcasted_iota(jnp.int32, (nb, ts), 0).astype(F32)
    col = lax.broadcasted_iota(jnp.int32, (nb, ts), 1)
    own = ((pl.program_id(2) * ts + col) // MOBA_BLOCK).astype(F32)
    past = blk < own
    place_row = lax.broadcasted_iota(jnp.int32, (nb, LANES), 0)
    place_lane = lax.broadcasted_iota(jnp.int32, (nb, LANES), 1)

    for h, out_ref in enumerate((qe_ref, qo_ref)):
        data = even_data if h == 0 else jnp.logical_not(even_data)
        qh = jnp.where(data, qf, 0.0)
        gate = lax.dot_general(km, qh, nt, preferred_element_type=F32,
                               precision=lax.Precision.HIGHEST)
        g = jnp.where(past, gate, NEG_INF)
        sel = jnp.zeros((nb, ts), F32)
        for _ in range(min(MOBA_TOPK, nb)):
            m = jnp.max(g, axis=0, keepdims=True)
            first = jnp.min(jnp.where(g == m, blk, float(nb)), axis=0, keepdims=True)
            pick = blk == first
            sel = jnp.where(pick, 1.0, sel)
            g = jnp.where(pick, NEG_INF, g)
        sel = jnp.where(past, sel, 0.0)
        sel = jnp.where(blk == own, 1.0, sel)
        spare_base = HEAD_DIM if h == 0 else 0
        place = (place_lane == place_row + spare_base).astype(BF16)
        placed = lax.dot_general(sel.astype(BF16), place, (((0,), (0,)), ((), ())),
                                 preferred_element_type=F32)
        bias = (1.0 - placed) * MASK_BIAS
        out_ref[...] = jnp.where(data, qh * (HEAD_DIM ** -0.5 * LOG2_E), bias).astype(out_ref.dtype)


def _moba_select(qb, kmean, batch, ts=1024):
    pairs, t, _ = qb.shape
    s = t // batch
    nb = s // MOBA_BLOCK
    ts = min(ts, s)
    assert nb <= HEAD_DIM and ts % MOBA_BLOCK == 0
    km = kmean.reshape(batch, nb, pairs * LANES)
    qblk = lambda b, h, i: (h, b * (s // ts) + i, 0)
    return pl.pallas_call(
        _moba_select_kernel,
        grid=(batch, pairs, s // ts),
        in_specs=[pl.BlockSpec((None, ts, LANES), qblk),
                  pl.BlockSpec((None, nb, LANES), lambda b, h, i: (b, 0, h))],
        out_specs=[pl.BlockSpec((None, ts, LANES), qblk)] * 2,
        out_shape=[jax.ShapeDtypeStruct((pairs, t, LANES), BF16)] * 2,
        compiler_params=_params("parallel", "parallel", "parallel"),
        name="moba_select",
    )(qb, km)


def _moba_kernel(qe_ref, qo_ref, ke_ref, ko_ref, ve_ref, vo_ref, o_ref, s_ref, m_ref, acc_ref):
    qi = pl.program_id(2)
    tq = qe_ref.shape[0]
    slab = MOBA_GROUP * MOBA_BLOCK
    nt = (((1,), (1,)), ((), ()))
    last = qi // MOBA_GROUP
    heads = ((qe_ref, ke_ref, ve_ref), (qo_ref, ko_ref, vo_ref))
    r = lax.broadcasted_iota(jnp.int32, (tq, slab), 0)
    c = lax.broadcasted_iota(jnp.int32, (tq, slab), 1)
    causal = c - r <= (qi - last * MOBA_GROUP) * MOBA_BLOCK

    def scores(groups, masked):
        for h, (q_ref, k_ref, _) in enumerate(heads):
            mx = None if masked else m_ref[h]
            for t in groups:
                off = t * slab
                sb = lax.dot_general(q_ref[...], k_ref[pl.ds(off, slab), :], nt,
                                     preferred_element_type=F32)
                if masked:
                    sb = jnp.where(causal, sb, NEG_INF)
                s_ref[h, t] = sb
                for u in range(slab // LANES):
                    part = sb[:, u * LANES:(u + 1) * LANES]
                    mx = part if mx is None else jnp.maximum(mx, part)
            m_ref[h] = mx

    def weighted_values(groups, first):
        for h, (_, _, v_ref) in enumerate(heads):
            acc = None if first else acc_ref[h]
            for t in groups:
                off = t * slab
                p = jnp.exp2(s_ref[h, t] - row_max[h]).astype(BF16)
                pv = jnp.dot(p, v_ref[pl.ds(off, slab), :], preferred_element_type=F32)
                acc = pv if acc is None else acc + pv
            acc_ref[h] = acc

    row_max = [None, None]
    for n_past in range(s_ref.shape[1]):
        @pl.when(last == n_past)
        def _(n_past=n_past):
            scores((n_past,), True)
            if n_past:
                scores(tuple(range(n_past)), False)
            for h in range(2):
                row_max[h] = jnp.max(m_ref[h], axis=-1, keepdims=True)
            weighted_values(tuple(range(n_past + 1)), True)

    lane = lax.broadcasted_iota(jnp.int32, (tq, LANES), 1)
    a0 = acc_ref[0]
    a1 = acc_ref[1]
    o0 = a0 / pltpu.roll(a0, HEAD_DIM, 1)
    o1 = a1 / pltpu.roll(a1, HEAD_DIM, 1)
    o_ref[...] = jnp.where(lane < HEAD_DIM, o0, o1).astype(o_ref.dtype)


def _moba(qbe, qbo, kbe, kbo, vbe, vbo, batch):
    pairs, t, _ = qbe.shape
    s = t // batch
    tq = MOBA_BLOCK
    nq = s // tq
    nb = s // MOBA_BLOCK
    assert nb % MOBA_GROUP == 0
    qblk = lambda b, h, i: (h, b * nq + i, 0)
    kvblk = lambda b, h, i: (h, b, 0)
    return pl.pallas_call(
        _moba_kernel,
        grid=(batch, pairs, nq),
        in_specs=[pl.BlockSpec((None, tq, LANES), qblk)] * 2
        + [pl.BlockSpec((None, s, LANES), kvblk)] * 4,
        out_specs=pl.BlockSpec((None, tq, LANES), qblk),
        out_shape=jax.ShapeDtypeStruct((pairs, t, LANES), BF16),
        scratch_shapes=[pltpu.VMEM((2, nb // MOBA_GROUP, tq, MOBA_GROUP * MOBA_BLOCK), F32),
                        pltpu.VMEM((2, tq, LANES), F32),
                        pltpu.VMEM((2, tq, LANES), F32)],
        compiler_params=_params("parallel", "parallel", "arbitrary"),
        name="moba",
    )(qbe, qbo, kbe, kbo, vbe, vbo)


def _out_proj_kernel(ya_ref, yb_ref, w_ref, x_ref, g_ref, b_ref, o_ref):
    ka = ya_ref.shape[1]
    yb = jnp.concatenate([yb_ref[p] for p in range(yb_ref.shape[0])], axis=1)
    y = (jnp.dot(ya_ref[...], w_ref[:ka, :], preferred_element_type=F32)
         + jnp.dot(yb, w_ref[ka:, :], preferred_element_type=F32))
    z = DEEPNORM_ALPHA * x_ref[...] + y
    o_ref[...] = _layer_norm_rows(z, g_ref[...], b_ref[...])


def _out_proj_ln(ya, yb, w_bf16, x2d, gain, bias, tm=512):
    t, d = x2d.shape
    pairs = yb.shape[0]
    row = lambda i: (i, 0)
    const = lambda i: (0, 0)
    return pl.pallas_call(
        _out_proj_kernel,
        grid=(t // tm,),
        in_specs=[pl.BlockSpec((tm, ya.shape[1]), row),
                  pl.BlockSpec((pairs, tm, LANES), lambda i: (0, i, 0)),
                  pl.BlockSpec(w_bf16.shape, const),
                  pl.BlockSpec((tm, d), row),
                  pl.BlockSpec((1, d), const),
                  pl.BlockSpec((1, d), const)],
        out_specs=pl.BlockSpec((tm, d), row),
        out_shape=jax.ShapeDtypeStruct((t, d), F32),
        compiler_params=_params("parallel"),
        name="out_proj_ln",
    )(ya, yb, w_bf16, x2d, gain.reshape(1, d), bias.reshape(1, d))


def _ffn_kernel(x_ref, w1_ref, w2_ref, g_ref, b_ref, o_ref, *, row_split, th):
    tm = x_ref.shape[0]
    hdim = w1_ref.shape[1]
    rows = tm // row_split
    for part in range(row_split):
        sl = slice(part * rows, (part + 1) * rows)
        x = x_ref[sl, :]
        xb = x.astype(BF16)
        y = None
        for c in range(hdim // th):
            h = jnp.dot(xb, w1_ref[:, c * th:(c + 1) * th], preferred_element_type=F32)
            h = jnp.square(jnp.maximum(h, 0.0)).astype(BF16)
            yc = jnp.dot(h, w2_ref[c * th:(c + 1) * th, :], preferred_element_type=F32)
            y = yc if y is None else y + yc
        z = DEEPNORM_ALPHA * x + y
        o_ref[sl, :] = _layer_norm_rows(z, g_ref[...], b_ref[...])


def _ffn_ln(x2d, w1_bf16, w2_bf16, gain, bias, tm=1024, th=1024, row_split=2):
    t, d = x2d.shape
    hdim = w1_bf16.shape[1]
    row = lambda i: (i, 0)
    const = lambda i: (0, 0)
    resident = pl.Buffered(1)
    return pl.pallas_call(
        functools.partial(_ffn_kernel, row_split=row_split, th=th),
        grid=(t // tm,),
        in_specs=[pl.BlockSpec((tm, d), row),
                  pl.BlockSpec((d, hdim), const, pipeline_mode=resident),
                  pl.BlockSpec((hdim, d), const, pipeline_mode=resident),
                  pl.BlockSpec((1, d), const),
                  pl.BlockSpec((1, d), const)],
        out_specs=pl.BlockSpec((tm, d), row),
        out_shape=jax.ShapeDtypeStruct((t, d), F32),
        compiler_params=_params("parallel"),
        name="ffn_ln",
    )(x2d, w1_bf16, w2_bf16, gain.reshape(1, d), bias.reshape(1, d))


def _gmlp_kernel(x_ref, w_ref, b_ref, lng_ref, lnb_ref, ws_ref, bst_ref, wo_ref, g_ref, beta_ref,
                 o_ref, v_ref, gated_ref, *, row_split):
    tm = x_ref.shape[0]
    gw = v_ref.shape[1]
    c = GMLP_CHUNK
    cw = gw // GMLP_GROUPS
    part_rows = tm // row_split
    r = lax.broadcasted_iota(jnp.int32, (c, c), 0)
    cc = lax.broadcasted_iota(jnp.int32, (c, c), 1)
    causal = cc <= r
    bst = bst_ref[...]
    ws = [jnp.where(causal, ws_ref[g], 0.0).astype(BF16) for g in range(GMLP_GROUPS)]
    for part in range(row_split):
        base = part * part_rows
        prow = slice(base, base + part_rows)
        x = x_ref[prow, :]
        xb = x.astype(BF16)
        v = jnp.dot(xb, w_ref[:, gw:], preferred_element_type=F32) + b_ref[:, gw:]
        v_ref[prow, :] = _layer_norm_rows(jax.nn.gelu(v), lng_ref[...], lnb_ref[...]).astype(BF16)
        for g in range(GMLP_GROUPS):
            cols = slice(g * cw, (g + 1) * cw)
            u = jax.nn.gelu(jnp.dot(xb, w_ref[:, cols], preferred_element_type=F32) + b_ref[:, cols])
            for ci in range(part_rows // c):
                rows = slice(base + ci * c, base + (ci + 1) * c)
                sv = jnp.dot(ws[g], v_ref[rows, cols], preferred_element_type=F32) + bst[:, g:g + 1]
                gated_ref[rows, cols] = (u[ci * c:(ci + 1) * c] * sv).astype(BF16)
        y = jnp.dot(gated_ref[prow, :], wo_ref[...], preferred_element_type=F32)
        z = DEEPNORM_ALPHA * x + y
        o_ref[prow, :] = _layer_norm_rows(z, g_ref[...], beta_ref[...])


def _gmlp_mixer_ln(x2d, w_uv_bf16, b_uv, ln_g, ln_b, w_s, b_s, w_out_bf16, gain, bias,
                   tm=1024, row_split=2):
    t, d = x2d.shape
    n = w_uv_bf16.shape[1]
    gw = n // 2
    row = lambda i: (i, 0)
    const = lambda i: (0, 0)
    resident = pl.Buffered(1)
    return pl.pallas_call(
        functools.partial(_gmlp_kernel, row_split=row_split),
        grid=(t // tm,),
        in_specs=[pl.BlockSpec((tm, d), row),
                  pl.BlockSpec((d, n), const, pipeline_mode=resident),
                  pl.BlockSpec((1, n), const),
                  pl.BlockSpec((1, gw), const),
                  pl.BlockSpec((1, gw), const),
                  pl.BlockSpec(w_s.shape, lambda i: (0, 0, 0)),
                  pl.BlockSpec((GMLP_CHUNK, GMLP_GROUPS), const),
                  pl.BlockSpec((gw, d), const, pipeline_mode=resident),
                  pl.BlockSpec((1, d), const),
                  pl.BlockSpec((1, d), const)],
        out_specs=pl.BlockSpec((tm, d), row),
        out_shape=jax.ShapeDtypeStruct((t, d), F32),
        scratch_shapes=[pltpu.VMEM((tm, gw), BF16), pltpu.VMEM((tm, gw), BF16)],
        compiler_params=_params("parallel"),
        name="gmlp_mixer_ln",
    )(x2d, w_uv_bf16, b_uv.reshape(1, n), ln_g.reshape(1, gw), ln_b.reshape(1, gw), w_s, b_s.T,
      w_out_bf16, gain.reshape(1, d), bias.reshape(1, d))


def kernel(x, positions, ln_gain, ln_bias, mix_w_in, ret_gn_gain, mix_w_out,
           gmlp_w_uv, gmlp_b_uv, gmlp_ln_gain, gmlp_ln_bias, gmlp_w_s, gmlp_b_s,
           gmlp_w_out, ffn_w_in, ffn_w_out):
    batch, seq, d = x.shape
    depth = ln_gain.shape[0]
    h = x.reshape(batch * seq, d)
    cos_t, sin_t = _rope_tables(positions)
    log_gamma = jnp.log1p(-jnp.exp2(-5.0 - jnp.arange(N_RET_HEADS, dtype=F32)))
    for layer in range(depth):
        i = layer // 2
        if layer % 2 == 0:
            qa, ka, va, ga, qb, kbe, kbo, vbe, vbo, kmean = _in_proj(
                h, mix_w_in[i].astype(BF16), cos_t, sin_t, seq // MOBA_BLOCK)
            ya = _retention(qa, ka, va, ga, ret_gn_gain[i], log_gamma, batch)
            qbe, qbo = _moba_select(qb, kmean, batch)
            yb = _moba(qbe, qbo, kbe, kbo, vbe, vbo, batch)
            h = _out_proj_ln(ya, yb, mix_w_out[i].astype(BF16), h,
                             ln_gain[layer, 0], ln_bias[layer, 0])
        else:
            h = _gmlp_mixer_ln(h, gmlp_w_uv[i].astype(BF16), gmlp_b_uv[i], gmlp_ln_gain[i],
                               gmlp_ln_bias[i], gmlp_w_s[i], gmlp_b_s[i],
                               gmlp_w_out[i].astype(BF16), ln_gain[layer, 0], ln_bias[layer, 0])
        h = _ffn_ln(h, ffn_w_in[layer].astype(BF16), ffn_w_out[layer].astype(BF16),
                    ln_gain[layer, 1], ln_bias[layer, 1])
    return h.reshape(batch, seq, d)
```

```python
import functools

import jax
import jax.numpy as jnp
from jax import lax
from jax.experimental import pallas as pl
from jax.experimental.pallas import tpu as pltpu

HEAD_DIM = 64
N_RET_HEADS = 8
N_MOBA_HEADS = 8
RET_WIDTH = N_RET_HEADS * HEAD_DIM
MOBA_WIDTH = N_MOBA_HEADS * HEAD_DIM
RET_CHUNK = 128
MOBA_BLOCK = 256
MOBA_TOPK = 3
GMLP_CHUNK = 128
GMLP_GROUPS = 8
ROPE_THETA = 10000.0
LN_EPS = 1e-5
DEPTH = 4
DEEPNORM_ALPHA = (2 * DEPTH) ** 0.25

LANES = 128
VMEM_LIMIT_BYTES = 52 * 1024 * 1024

BF16 = jnp.bfloat16
F32 = jnp.float32
NEG_INF = float("-inf")
MOBA_GROUP = 2
MASK_BIAS = -1e30
LOG2_E = 1.4426950408889634


def _params(*sem):
    return pltpu.CompilerParams(dimension_semantics=sem, vmem_limit_bytes=VMEM_LIMIT_BYTES)


def _layer_norm_rows(z, g, b):
    mu = jnp.mean(z, axis=-1, keepdims=True)
    d = z - mu
    var = jnp.mean(d * d, axis=-1, keepdims=True)
    return d * lax.rsqrt(var + LN_EPS) * g + b


def _rope_table_kernel(pos_ref, freq_ref, cos_ref, sin_ref):
    ang = pos_ref[...] * freq_ref[...]
    lane = lax.broadcasted_iota(jnp.int32, ang.shape, 1)
    first_half = (lane % HEAD_DIM) < (HEAD_DIM // 2)
    cos_ref[...] = jnp.cos(ang)
    sin_ref[...] = jnp.where(first_half, -jnp.sin(ang), jnp.sin(ang))


def _rope_tables(positions):
    t = positions.size
    tm = min(t, 2048)
    pos = positions.reshape(t, 1).astype(F32)
    inv_freq = ROPE_THETA ** (-jnp.arange(0, HEAD_DIM, 2, dtype=F32) / HEAD_DIM)
    freq = jnp.tile(inv_freq, LANES // (HEAD_DIM // 2)).reshape(1, LANES)
    return pl.pallas_call(
        _rope_table_kernel,
        grid=(t // tm,),
        in_specs=[pl.BlockSpec((tm, 1), lambda i: (i, 0)),
                  pl.BlockSpec((1, LANES), lambda i: (0, 0))],
        out_specs=[pl.BlockSpec((tm, LANES), lambda i: (i, 0))] * 2,
        out_shape=[jax.ShapeDtypeStruct((t, LANES), F32)] * 2,
        compiler_params=_params("parallel"),
        name="rope_tables",
    )(pos, freq)


def _in_proj_kernel(x_ref, w_ref, cos_ref, sin_ref,
                    qa_ref, ka_ref, va_ref, ga_ref, qb_ref, kbe_ref, kbo_ref, vbe_ref, vbo_ref,
                    km_ref, *, n_blocks):
    tm = x_ref.shape[0]
    width = RET_WIDTH
    xb = x_ref[...].astype(BF16)
    reps = width // LANES
    cos = jnp.concatenate([cos_ref[...]] * reps, axis=1)
    sin = jnp.concatenate([sin_ref[...]] * reps, axis=1)
    lane = lax.broadcasted_iota(jnp.int32, (tm, width), 1)
    first_half = (lane % HEAD_DIM) < (HEAD_DIM // 2)

    def proj(g):
        return jnp.dot(xb, w_ref[:, g * width:(g + 1) * width], preferred_element_type=F32)

    def rope(t):
        partner = jnp.where(first_half,
                            pltpu.roll(t, width - HEAD_DIM // 2, 1),
                            pltpu.roll(t, HEAD_DIM // 2, 1))
        return t * cos + partner * sin

    qa_ref[...] = rope(proj(0)).astype(qa_ref.dtype)
    ka_ref[...] = rope(proj(1)).astype(ka_ref.dtype)
    va_ref[...] = proj(2).astype(va_ref.dtype)
    ga_ref[...] = proj(3).astype(ga_ref.dtype)
    def store_pairs(ref, val):
        for p in range(width // LANES):
            ref[p] = val[:, p * LANES:(p + 1) * LANES].astype(ref.dtype)

    store_pairs(qb_ref, rope(proj(4)))
    kb = rope(proj(5))
    vb = proj(6)
    pair_lane = lane % LANES
    even_data = pair_lane < HEAD_DIM
    row = lax.broadcasted_iota(jnp.int32, (tm, width), 0)
    block = ((pl.program_id(0) * tm + row) // MOBA_BLOCK) % n_blocks
    onehot_e = (pair_lane - HEAD_DIM == block).astype(F32)
    onehot_o = (pair_lane == block).astype(F32)
    store_pairs(kbe_ref, jnp.where(even_data, kb, onehot_e))
    store_pairs(kbo_ref, jnp.where(even_data, onehot_o, kb))
    store_pairs(vbe_ref, jnp.where(even_data, vb, 1.0))
    store_pairs(vbo_ref, jnp.where(even_data, 1.0, vb))
    for blk in range(tm // MOBA_BLOCK):
        km_ref[blk] = jnp.mean(kb[blk * MOBA_BLOCK:(blk + 1) * MOBA_BLOCK], axis=0, keepdims=True)


def _in_proj(x2d, w_bf16, cos_t, sin_t, n_blocks, tm=512):
    t, d = x2d.shape
    n = w_bf16.shape[1]
    width = RET_WIDTH
    assert n_blocks <= HEAD_DIM and tm % MOBA_BLOCK == 0
    row = lambda i: (i, 0)
    pairs = width // LANES
    row_dtypes = [BF16, F32, BF16, F32]
    pair_dtypes = [F32, BF16, BF16, BF16, BF16]
    return pl.pallas_call(
        functools.partial(_in_proj_kernel, n_blocks=n_blocks),
        grid=(t // tm,),
        in_specs=[pl.BlockSpec((tm, d), row),
                  pl.BlockSpec((d, n), lambda i: (0, 0)),
                  pl.BlockSpec((tm, LANES), row),
                  pl.BlockSpec((tm, LANES), row)],
        out_specs=[pl.BlockSpec((tm, width), row)] * len(row_dtypes)
        + [pl.BlockSpec((pairs, tm, LANES), lambda i: (0, i, 0))] * len(pair_dtypes)
        + [pl.BlockSpec((tm // MOBA_BLOCK, 1, width), lambda i: (i, 0, 0))],
        out_shape=[jax.ShapeDtypeStruct((t, width), dt) for dt in row_dtypes]
        + [jax.ShapeDtypeStruct((pairs, t, LANES), dt) for dt in pair_dtypes]
        + [jax.ShapeDtypeStruct((t // MOBA_BLOCK, 1, width), F32)],
        compiler_params=_params("parallel"),
        name="in_proj",
    )(x2d, w_bf16, cos_t, sin_t)


def _retention_kernel(lg_ref, q_ref, k_ref, v_ref, g_ref, gain_ref, o_ref, state_ref, decay_ref):
    c = RET_CHUNK
    tm, width = q_ref.shape
    lane = lax.broadcasted_iota(jnp.int32, (c, LANES), 1)
    row = lax.broadcasted_iota(jnp.int32, (c, LANES), 0)
    head0 = lane < HEAD_DIM
    same_head = (row < HEAD_DIM) == head0
    nt = (((1,), (1,)), ((), ()))
    tn = (((0,), (0,)), ((), ()))

    @pl.when(pl.program_id(1) == 0)
    def _():
        state_ref[...] = jnp.zeros_like(state_ref)
        idx = row.astype(F32)
        diff = (row - lane).astype(F32)
        dpos = jnp.maximum(diff, 0.0)
        for hp in range(width // LANES):
            lg0 = lg_ref[2 * hp]
            lg1 = lg_ref[2 * hp + 1]
            lg_lane = jnp.where(head0, lg0, lg1)
            decay_ref[hp, 0] = jnp.exp(lg_lane * (idx + 1.0))
            decay_ref[hp, 1] = jnp.exp(lg_lane * (c - 1.0 - idx))
            decay_ref[hp, 2] = jnp.exp(lg_lane * float(c))
            decay_ref[hp, 3] = jnp.where(diff >= 0, jnp.exp(lg0 * dpos), 0.0)
            decay_ref[hp, 4] = jnp.where(diff >= 0, jnp.exp(lg1 * dpos), 0.0)

    for hp in range(width // LANES):
        cols = slice(hp * LANES, (hp + 1) * LANES)
        xi = decay_ref[hp, 0]
        zeta = decay_ref[hp, 1]
        chunk_decay = decay_ref[hp, 2]
        decay0 = decay_ref[hp, 3]
        decay1 = decay_ref[hp, 4]
        gain = gain_ref[:, cols]
        for ci in range(tm // c):
            sl = slice(ci * c, (ci + 1) * c)
            q = q_ref[sl, cols]
            kf = k_ref[sl, cols] * (HEAD_DIM ** -0.5)
            v = v_ref[sl, cols]
            kb = kf.astype(BF16)
            zero = jnp.zeros_like(q)
            q0 = jnp.where(head0, q, zero)
            q1 = jnp.where(head0, zero, q)
            s0 = lax.dot_general(q0, kb, nt, preferred_element_type=F32) * decay0
            s1 = lax.dot_general(q1, kb, nt, preferred_element_type=F32) * decay1
            inner = jnp.where(head0,
                              jnp.dot(s0.astype(BF16), v, preferred_element_type=F32),
                              jnp.dot(s1.astype(BF16), v, preferred_element_type=F32))
            state = state_ref[hp]
            cross = jnp.dot(q, state.astype(BF16), preferred_element_type=F32) * xi
            kv = lax.dot_general((kf * zeta).astype(BF16), v, tn, preferred_element_type=F32)
            state_ref[hp] = state * chunk_decay + jnp.where(same_head, kv, 0.0)

            y = inner + cross
            inv = 1.0 / HEAD_DIM
            mu = jnp.where(head0,
                           jnp.sum(jnp.where(head0, y, 0.0), axis=-1, keepdims=True),
                           jnp.sum(jnp.where(head0, 0.0, y), axis=-1, keepdims=True)) * inv
            d = y - mu
            dd = d * d
            var = jnp.where(head0,
                            jnp.sum(jnp.where(head0, dd, 0.0), axis=-1, keepdims=True),
                            jnp.sum(jnp.where(head0, 0.0, dd), axis=-1, keepdims=True)) * inv
            yn = d * lax.rsqrt(var + LN_EPS) * gain
            o_ref[sl, cols] = (yn * jax.nn.silu(g_ref[sl, cols])).astype(o_ref.dtype)


def _retention(qa, ka, va, ga, gn_gain, log_gamma, batch, tm=512):
    t, width = qa.shape
    s = t // batch
    nt = s // tm
    pairs = width // LANES
    blk = lambda b, i: (b * nt + i, 0)
    return pl.pallas_call(
        _retention_kernel,
        grid=(batch, nt),
        in_specs=[pl.BlockSpec(memory_space=pltpu.SMEM)]
        + [pl.BlockSpec((tm, width), blk)] * 4
        + [pl.BlockSpec((1, width), lambda b, i: (0, 0))],
        out_specs=pl.BlockSpec((tm, width), blk),
        out_shape=jax.ShapeDtypeStruct((t, width), BF16),
        scratch_shapes=[pltpu.VMEM((pairs, LANES, LANES), F32),
                        pltpu.VMEM((pairs, 5, RET_CHUNK, LANES), F32)],
        compiler_params=_params("parallel", "arbitrary"),
        name="retention",
    )(log_gamma, qa, ka, va, ga, gn_gain.reshape(1, width))


def _moba_select_kernel(q_ref, km_ref, qe_ref, qo_ref):
    ts = q_ref.shape[0]
    nb = km_ref.shape[0]
    qf = q_ref[...]
    km = km_ref[...]
    lane = lax.broadcasted_iota(jnp.int32, (ts, LANES), 1)
    even_data = lane < HEAD_DIM
    nt = (((1,), (1,)), ((), ()))
    blk = lax.broadcasted_iota(jnp.int32, (nb, ts), 0).astype(F32)
    col = lax.broadcasted_iota(jnp.int32, (nb, ts), 1)
    own = ((pl.program_id(2) * ts + col) // MOBA_BLOCK).astype(F32)
    past = blk < own
    place_row = lax.broadcasted_iota(jnp.int32, (nb, LANES), 0)
    place_lane = lax.broadcasted_iota(jnp.int32, (nb, LANES), 1)

    for h, out_ref in enumerate((qe_ref, qo_ref)):
        data = even_data if h == 0 else jnp.logical_not(even_data)
        qh = jnp.where(data, qf, 0.0)
        gate = lax.dot_general(km, qh, nt, preferred_element_type=F32,
                               precision=lax.Precision.HIGHEST)
        g = jnp.where(past, gate, NEG_INF)
        sel = jnp.zeros((nb, ts), F32)
        for _ in range(min(MOBA_TOPK, nb)):
            m = jnp.max(g, axis=0, keepdims=True)
            first = jnp.min(jnp.where(g == m, blk, float(nb)), axis=0, keepdims=True)
            pick = blk == first
            sel = jnp.where(pick, 1.0, sel)
            g = jnp.where(pick, NEG_INF, g)
        sel = jnp.where(past, sel, 0.0)
        sel = jnp.where(blk == own, 1.0, sel)
        spare_base = HEAD_DIM if h == 0 else 0
        place = (place_lane == place_row + spare_base).astype(BF16)
        placed = lax.dot_general(sel.astype(BF16), place, (((0,), (0,)), ((), ())),
                                 preferred_element_type=F32)
        bias = (1.0 - placed) * MASK_BIAS
        out_ref[...] = jnp.where(data, qh * (HEAD_DIM ** -0.5 * LOG2_E), bias).astype(out_ref.dtype)


def _moba_select(qb, kmean, batch, ts=2048):
    pairs, t, _ = qb.shape
    s = t // batch
    nb = s // MOBA_BLOCK
    ts = min(ts, s)
    assert nb <= HEAD_DIM and ts % MOBA_BLOCK == 0
    km = kmean.reshape(batch, nb, pairs * LANES)
    qblk = lambda b, h, i: (h, b * (s // ts) + i, 0)
    return pl.pallas_call(
        _moba_select_kernel,
        grid=(batch, pairs, s // ts),
        in_specs=[pl.BlockSpec((None, ts, LANES), qblk),
                  pl.BlockSpec((None, nb, LANES), lambda b, h, i: (b, 0, h))],
        out_specs=[pl.BlockSpec((None, ts, LANES), qblk)] * 2,
        out_shape=[jax.ShapeDtypeStruct((pairs, t, LANES), BF16)] * 2,
        compiler_params=_params("parallel", "parallel", "parallel"),
        name="moba_select",
    )(qb, km)


def _moba_kernel(qe_ref, qo_ref, ke_ref, ko_ref, ve_ref, vo_ref, o_ref, s_ref, m_ref, acc_ref):
    qi = pl.program_id(2)
    tq = qe_ref.shape[0]
    slab = MOBA_GROUP * MOBA_BLOCK
    nt = (((1,), (1,)), ((), ()))
    last = qi // MOBA_GROUP
    heads = ((qe_ref, ke_ref, ve_ref), (qo_ref, ko_ref, vo_ref))
    r = lax.broadcasted_iota(jnp.int32, (tq, slab), 0)
    c = lax.broadcasted_iota(jnp.int32, (tq, slab), 1)
    causal = c - r <= (qi - last * MOBA_GROUP) * MOBA_BLOCK

    def scores(groups, masked):
        for h, (q_ref, k_ref, _) in enumerate(heads):
            mx = None if masked else m_ref[h]
            for t in groups:
                off = t * slab
                sb = lax.dot_general(q_ref[...], k_ref[pl.ds(off, slab), :], nt,
                                     preferred_element_type=F32)
                if masked:
                    sb = jnp.where(causal, sb, NEG_INF)
                s_ref[h, t] = sb
                for u in range(slab // LANES):
                    part = sb[:, u * LANES:(u + 1) * LANES]
                    mx = part if mx is None else jnp.maximum(mx, part)
            m_ref[h] = mx

    def weighted_values(groups, first):
        for h, (_, _, v_ref) in enumerate(heads):
            acc = None if first else acc_ref[h]
            for t in groups:
                off = t * slab
                p = jnp.exp2(s_ref[h, t] - row_max[h]).astype(BF16)
                pv = jnp.dot(p, v_ref[pl.ds(off, slab), :], preferred_element_type=F32)
                acc = pv if acc is None else acc + pv
            acc_ref[h] = acc

    row_max = [None, None]
    for n_past in range(s_ref.shape[1]):
        @pl.when(last == n_past)
        def _(n_past=n_past):
            scores((n_past,), True)
            if n_past:
                scores(tuple(range(n_past)), False)
            for h in range(2):
                row_max[h] = jnp.max(m_ref[h], axis=-1, keepdims=True)
            weighted_values(tuple(range(n_past + 1)), True)

    lane = lax.broadcasted_iota(jnp.int32, (tq, LANES), 1)
    a0 = acc_ref[0]
    a1 = acc_ref[1]
    o0 = a0 / pltpu.roll(a0, HEAD_DIM, 1)
    o1 = a1 / pltpu.roll(a1, HEAD_DIM, 1)
    o_ref[...] = jnp.where(lane < HEAD_DIM, o0, o1).astype(o_ref.dtype)


def _moba(qbe, qbo, kbe, kbo, vbe, vbo, batch):
    pairs, t, _ = qbe.shape
    s = t // batch
    tq = MOBA_BLOCK
    nq = s // tq
    nb = s // MOBA_BLOCK
    assert nb % MOBA_GROUP == 0
    qblk = lambda b, h, i: (h, b * nq + i, 0)
    kvblk = lambda b, h, i: (h, b, 0)
    return pl.pallas_call(
        _moba_kernel,
        grid=(batch, pairs, nq),
        in_specs=[pl.BlockSpec((None, tq, LANES), qblk)] * 2
        + [pl.BlockSpec((None, s, LANES), kvblk)] * 4,
        out_specs=pl.BlockSpec((None, tq, LANES), qblk),
        out_shape=jax.ShapeDtypeStruct((pairs, t, LANES), BF16),
        scratch_shapes=[pltpu.VMEM((2, nb // MOBA_GROUP, tq, MOBA_GROUP * MOBA_BLOCK), F32),
                        pltpu.VMEM((2, tq, LANES), F32),
                        pltpu.VMEM((2, tq, LANES), F32)],
        compiler_params=_params("parallel", "parallel", "arbitrary"),
        name="moba",
    )(qbe, qbo, kbe, kbo, vbe, vbo)


def _out_proj_kernel(ya_ref, yb_ref, w_ref, x_ref, g_ref, b_ref, o_ref):
    ka = ya_ref.shape[1]
    yb = jnp.concatenate([yb_ref[p] for p in range(yb_ref.shape[0])], axis=1)
    y = (jnp.dot(ya_ref[...], w_ref[:ka, :], preferred_element_type=F32)
         + jnp.dot(yb, w_ref[ka:, :], preferred_element_type=F32))
    z = DEEPNORM_ALPHA * x_ref[...] + y
    o_ref[...] = _layer_norm_rows(z, g_ref[...], b_ref[...])


def _out_proj_ln(ya, yb, w_bf16, x2d, gain, bias, tm=512):
    t, d = x2d.shape
    pairs = yb.shape[0]
    row = lambda i: (i, 0)
    const = lambda i: (0, 0)
    return pl.pallas_call(
        _out_proj_kernel,
        grid=(t // tm,),
        in_specs=[pl.BlockSpec((tm, ya.shape[1]), row),
                  pl.BlockSpec((pairs, tm, LANES), lambda i: (0, i, 0)),
                  pl.BlockSpec(w_bf16.shape, const),
                  pl.BlockSpec((tm, d), row),
                  pl.BlockSpec((1, d), const),
                  pl.BlockSpec((1, d), const)],
        out_specs=pl.BlockSpec((tm, d), row),
        out_shape=jax.ShapeDtypeStruct((t, d), F32),
        compiler_params=_params("parallel"),
        name="out_proj_ln",
    )(ya, yb, w_bf16, x2d, gain.reshape(1, d), bias.reshape(1, d))


def _ffn_kernel(x_ref, w1_ref, w2_ref, g_ref, b_ref, o_ref, *, row_split, th):
    tm = x_ref.shape[0]
    hdim = w1_ref.shape[1]
    rows = tm // row_split
    for part in range(row_split):
        sl = slice(part * rows, (part + 1) * rows)
        x = x_ref[sl, :]
        xb = x.astype(BF16)
        y = None
        for c in range(hdim // th):
            h = jnp.dot(xb, w1_ref[:, c * th:(c + 1) * th], preferred_element_type=F32)
            h = jnp.square(jnp.maximum(h, 0.0)).astype(BF16)
            yc = jnp.dot(h, w2_ref[c * th:(c + 1) * th, :], preferred_element_type=F32)
            y = yc if y is None else y + yc
        z = DEEPNORM_ALPHA * x + y
        o_ref[sl, :] = _layer_norm_rows(z, g_ref[...], b_ref[...])


def _ffn_ln(x2d, w1_bf16, w2_bf16, gain, bias, tm=1024, th=1024, row_split=2):
    t, d = x2d.shape
    hdim = w1_bf16.shape[1]
    row = lambda i: (i, 0)
    const = lambda i: (0, 0)
    resident = pl.Buffered(1)
    return pl.pallas_call(
        functools.partial(_ffn_kernel, row_split=row_split, th=th),
        grid=(t // tm,),
        in_specs=[pl.BlockSpec((tm, d), row),
                  pl.BlockSpec((d, hdim), const, pipeline_mode=resident),
                  pl.BlockSpec((hdim, d), const, pipeline_mode=resident),
                  pl.BlockSpec((1, d), const),
                  pl.BlockSpec((1, d), const)],
        out_specs=pl.BlockSpec((tm, d), row),
        out_shape=jax.ShapeDtypeStruct((t, d), F32),
        compiler_params=_params("parallel"),
        name="ffn_ln",
    )(x2d, w1_bf16, w2_bf16, gain.reshape(1, d), bias.reshape(1, d))


def _gmlp_kernel(x_ref, w_ref, b_ref, lng_ref, lnb_ref, ws_ref, bst_ref, wo_ref, g_ref, beta_ref,
                 o_ref, v_ref, gated_ref, *, row_split):
    tm = x_ref.shape[0]
    gw = v_ref.shape[1]
    c = GMLP_CHUNK
    cw = gw // GMLP_GROUPS
    part_rows = tm // row_split
    r = lax.broadcasted_iota(jnp.int32, (c, c), 0)
    cc = lax.broadcasted_iota(jnp.int32, (c, c), 1)
    causal = cc <= r
    bst = bst_ref[...]
    ws = [jnp.where(causal, ws_ref[g], 0.0).astype(BF16) for g in range(GMLP_GROUPS)]
    for part in range(row_split):
        base = part * part_rows
        prow = slice(base, base + part_rows)
        x = x_ref[prow, :]
        xb = x.astype(BF16)
        v = jnp.dot(xb, w_ref[:, gw:], preferred_element_type=F32) + b_ref[:, gw:]
        v_ref[prow, :] = _layer_norm_rows(jax.nn.gelu(v), lng_ref[...], lnb_ref[...]).astype(BF16)
        for g in range(GMLP_GROUPS):
            cols = slice(g * cw, (g + 1) * cw)
            u = jax.nn.gelu(jnp.dot(xb, w_ref[:, cols], preferred_element_type=F32) + b_ref[:, cols])
            for ci in range(part_rows // c):
                rows = slice(base + ci * c, base + (ci + 1) * c)
                sv = jnp.dot(ws[g], v_ref[rows, cols], preferred_element_type=F32) + bst[:, g:g + 1]
                gated_ref[rows, cols] = (u[ci * c:(ci + 1) * c] * sv).astype(BF16)
        y = jnp.dot(gated_ref[prow, :], wo_ref[...], preferred_element_type=F32)
        z = DEEPNORM_ALPHA * x + y
        o_ref[prow, :] = _layer_norm_rows(z, g_ref[...], beta_ref[...])


def _gmlp_mixer_ln(x2d, w_uv_bf16, b_uv, ln_g, ln_b, w_s, b_s, w_out_bf16, gain, bias,
                   tm=1024, row_split=2):
    t, d = x2d.shape
    n = w_uv_bf16.shape[1]
    gw = n // 2
    row = lambda i: (i, 0)
    const = lambda i: (0, 0)
    resident = pl.Buffered(1)
    return pl.pallas_call(
        functools.partial(_gmlp_kernel, row_split=row_split),
        grid=(t // tm,),
        in_specs=[pl.BlockSpec((tm, d), row),
                  pl.BlockSpec((d, n), const, pipeline_mode=resident),
                  pl.BlockSpec((1, n), const),
                  pl.BlockSpec((1, gw), const),
                  pl.BlockSpec((1, gw), const),
                  pl.BlockSpec(w_s.shape, lambda i: (0, 0, 0)),
                  pl.BlockSpec((GMLP_CHUNK, GMLP_GROUPS), const),
                  pl.BlockSpec((gw, d), const, pipeline_mode=resident),
                  pl.BlockSpec((1, d), const),
                  pl.BlockSpec((1, d), const)],
        out_specs=pl.BlockSpec((tm, d), row),
        out_shape=jax.ShapeDtypeStruct((t, d), F32),
        scratch_shapes=[pltpu.VMEM((tm, gw), BF16), pltpu.VMEM((tm, gw), BF16)],
        compiler_params=_params("parallel"),
        name="gmlp_mixer_ln",
    )(x2d, w_uv_bf16, b_uv.reshape(1, n), ln_g.reshape(1, gw), ln_b.reshape(1, gw), w_s, b_s.T,
      w_out_bf16, gain.reshape(1, d), bias.reshape(1, d))


def kernel(x, positions, ln_gain, ln_bias, mix_w_in, ret_gn_gain, mix_w_out,
           gmlp_w_uv, gmlp_b_uv, gmlp_ln_gain, gmlp_ln_bias, gmlp_w_s, gmlp_b_s,
           gmlp_w_out, ffn_w_in, ffn_w_out):
    batch, seq, d = x.shape
    depth = ln_gain.shape[0]
    h = x.reshape(batch * seq, d)
    cos_t, sin_t = _rope_tables(positions)
    log_gamma = jnp.log1p(-jnp.exp2(-5.0 - jnp.arange(N_RET_HEADS, dtype=F32)))
    for layer in range(depth):
        i = layer // 2
        if layer % 2 == 0:
            qa, ka, va, ga, qb, kbe, kbo, vbe, vbo, kmean = _in_proj(
                h, mix_w_in[i].astype(BF16), cos_t, sin_t, seq // MOBA_BLOCK)
            ya = _retention(qa, ka, va, ga, ret_gn_gain[i], log_gamma, batch)
            qbe, qbo = _moba_select(qb, kmean, batch)
            yb = _moba(qbe, qbo, kbe, kbo, vbe, vbo, batch)
            h = _out_proj_ln(ya, yb, mix_w_out[i].astype(BF16), h,
                             ln_gain[layer, 0], ln_bias[layer, 0])
        else:
            h = _gmlp_mixer_ln(h, gmlp_w_uv[i].astype(BF16), gmlp_b_uv[i], gmlp_ln_gain[i],
                               gmlp_ln_bias[i], gmlp_w_s[i], gmlp_b_s[i],
                               gmlp_w_out[i].astype(BF16), ln_gain[layer, 0], ln_bias[layer, 0])
        h = _ffn_ln(h, ffn_w_in[layer].astype(BF16), ffn_w_out[layer].astype(BF16),
                    ln_gain[layer, 1], ln_bias[layer, 1])
    return h.reshape(batch, seq, d)
```

```python
import functools

import jax
import jax.numpy as jnp
from jax import lax
from jax.experimental import pallas as pl
from jax.experimental.pallas import tpu as pltpu

HEAD_DIM = 64
N_RET_HEADS = 8
N_MOBA_HEADS = 8
RET_WIDTH = N_RET_HEADS * HEAD_DIM
MOBA_WIDTH = N_MOBA_HEADS * HEAD_DIM
RET_CHUNK = 128
MOBA_BLOCK = 256
MOBA_TOPK = 3
GMLP_CHUNK = 128
GMLP_GROUPS = 8
ROPE_THETA = 10000.0
LN_EPS = 1e-5
DEPTH = 4
DEEPNORM_ALPHA = (2 * DEPTH) ** 0.25

LANES = 128
VMEM_LIMIT_BYTES = 52 * 1024 * 1024

BF16 = jnp.bfloat16
F32 = jnp.float32
NEG_INF = float("-inf")
MOBA_GROUP = 8
MASK_BIAS = -1e30
LOG2_E = 1.4426950408889634


def _params(*sem):
    return pltpu.CompilerParams(dimension_semantics=sem, vmem_limit_bytes=VMEM_LIMIT_BYTES)


def _layer_norm_rows(z, g, b):
    mu = jnp.mean(z, axis=-1, keepdims=True)
    d = z - mu
    var = jnp.mean(d * d, axis=-1, keepdims=True)
    return d * lax.rsqrt(var + LN_EPS) * g + b


def _rope_table_kernel(pos_ref, freq_ref, cos_ref, sin_ref):
    ang = pos_ref[...] * freq_ref[...]
    lane = lax.broadcasted_iota(jnp.int32, ang.shape, 1)
    first_half = (lane % HEAD_DIM) < (HEAD_DIM // 2)
    cos_ref[...] = jnp.cos(ang)
    sin_ref[...] = jnp.where(first_half, -jnp.sin(ang), jnp.sin(ang))


def _rope_tables(positions):
    t = positions.size
    tm = min(t, 2048)
    pos = positions.reshape(t, 1).astype(F32)
    inv_freq = ROPE_THETA ** (-jnp.arange(0, HEAD_DIM, 2, dtype=F32) / HEAD_DIM)
    freq = jnp.tile(inv_freq, LANES // (HEAD_DIM // 2)).reshape(1, LANES)
    return pl.pallas_call(
        _rope_table_kernel,
        grid=(t // tm,),
        in_specs=[pl.BlockSpec((tm, 1), lambda i: (i, 0)),
                  pl.BlockSpec((1, LANES), lambda i: (0, 0))],
        out_specs=[pl.BlockSpec((tm, LANES), lambda i: (i, 0))] * 2,
        out_shape=[jax.ShapeDtypeStruct((t, LANES), F32)] * 2,
        compiler_params=_params("parallel"),
        name="rope_tables",
    )(pos, freq)


def _in_proj_kernel(x_ref, w_ref, cos_ref, sin_ref,
                    qa_ref, ka_ref, va_ref, ga_ref, qb_ref, kbe_ref, kbo_ref, vbe_ref, vbo_ref,
                    km_ref, *, n_blocks):
    tm = x_ref.shape[0]
    width = RET_WIDTH
    xb = x_ref[...].astype(BF16)
    reps = width // LANES
    cos = jnp.concatenate([cos_ref[...]] * reps, axis=1)
    sin = jnp.concatenate([sin_ref[...]] * reps, axis=1)
    lane = lax.broadcasted_iota(jnp.int32, (tm, width), 1)
    first_half = (lane % HEAD_DIM) < (HEAD_DIM // 2)

    def proj(g):
        return jnp.dot(xb, w_ref[:, g * width:(g + 1) * width], preferred_element_type=F32)

    def rope(t):
        partner = jnp.where(first_half,
                            pltpu.roll(t, width - HEAD_DIM // 2, 1),
                            pltpu.roll(t, HEAD_DIM // 2, 1))
        return t * cos + partner * sin

    qa_ref[...] = rope(proj(0)).astype(qa_ref.dtype)
    ka_ref[...] = rope(proj(1)).astype(ka_ref.dtype)
    va_ref[...] = proj(2).astype(va_ref.dtype)
    ga_ref[...] = proj(3).astype(ga_ref.dtype)
    def store_pairs(ref, val):
        for p in range(width // LANES):
            ref[p] = val[:, p * LANES:(p + 1) * LANES].astype(ref.dtype)

    store_pairs(qb_ref, rope(proj(4)))
    kb = rope(proj(5))
    vb = proj(6)
    pair_lane = lane % LANES
    even_data = pair_lane < HEAD_DIM
    row = lax.broadcasted_iota(jnp.int32, (tm, width), 0)
    block = ((pl.program_id(0) * tm + row) // MOBA_BLOCK) % n_blocks
    onehot_e = (pair_lane - HEAD_DIM == block).astype(F32)
    onehot_o = (pair_lane == block).astype(F32)
    store_pairs(kbe_ref, jnp.where(even_data, kb, onehot_e))
    store_pairs(kbo_ref, jnp.where(even_data, onehot_o, kb))
    store_pairs(vbe_ref, jnp.where(even_data, vb, 1.0))
    store_pairs(vbo_ref, jnp.where(even_data, 1.0, vb))
    for blk in range(tm // MOBA_BLOCK):
        km_ref[blk] = jnp.mean(kb[blk * MOBA_BLOCK:(blk + 1) * MOBA_BLOCK], axis=0, keepdims=True)


def _in_proj(x2d, w_bf16, cos_t, sin_t, n_blocks, tm=512):
    t, d = x2d.shape
    n = w_bf16.shape[1]
    width = RET_WIDTH
    assert n_blocks <= HEAD_DIM and tm % MOBA_BLOCK == 0
    row = lambda i: (i, 0)
    pairs = width // LANES
    row_dtypes = [BF16, F32, BF16, F32]
    pair_dtypes = [F32, BF16, BF16, BF16, BF16]
    return pl.pallas_call(
        functools.partial(_in_proj_kernel, n_blocks=n_blocks),
        grid=(t // tm,),
        in_specs=[pl.BlockSpec((tm, d), row),
                  pl.BlockSpec((d, n), lambda i: (0, 0)),
                  pl.BlockSpec((tm, LANES), row),
                  pl.BlockSpec((tm, LANES), row)],
        out_specs=[pl.BlockSpec((tm, width), row)] * len(row_dtypes)
        + [pl.BlockSpec((pairs, tm, LANES), lambda i: (0, i, 0))] * len(pair_dtypes)
        + [pl.BlockSpec((tm // MOBA_BLOCK, 1, width), lambda i: (i, 0, 0))],
        out_shape=[jax.ShapeDtypeStruct((t, width), dt) for dt in row_dtypes]
        + [jax.ShapeDtypeStruct((pairs, t, LANES), dt) for dt in pair_dtypes]
        + [jax.ShapeDtypeStruct((t // MOBA_BLOCK, 1, width), F32)],
        compiler_params=_params("parallel"),
        name="in_proj",
    )(x2d, w_bf16, cos_t, sin_t)


def _retention_kernel(lg_ref, q_ref, k_ref, v_ref, g_ref, gain_ref, o_ref, state_ref, decay_ref):
    c = RET_CHUNK
    tm, width = q_ref.shape
    lane = lax.broadcasted_iota(jnp.int32, (c, LANES), 1)
    row = lax.broadcasted_iota(jnp.int32, (c, LANES), 0)
    head0 = lane < HEAD_DIM
    same_head = (row < HEAD_DIM) == head0
    nt = (((1,), (1,)), ((), ()))
    tn = (((0,), (0,)), ((), ()))

    @pl.when(pl.program_id(1) == 0)
    def _():
        state_ref[...] = jnp.zeros_like(state_ref)
        idx = row.astype(F32)
        diff = (row - lane).astype(F32)
        dpos = jnp.maximum(diff, 0.0)
        for hp in range(width // LANES):
            lg0 = lg_ref[2 * hp]
            lg1 = lg_ref[2 * hp + 1]
            lg_lane = jnp.where(head0, lg0, lg1)
            decay_ref[hp, 0] = jnp.exp(lg_lane * (idx + 1.0))
            decay_ref[hp, 1] = jnp.exp(lg_lane * (c - 1.0 - idx))
            decay_ref[hp, 2] = jnp.exp(lg_lane * float(c))
            decay_ref[hp, 3] = jnp.where(diff >= 0, jnp.exp(lg0 * dpos), 0.0)
            decay_ref[hp, 4] = jnp.where(diff >= 0, jnp.exp(lg1 * dpos), 0.0)

    for hp in range(width // LANES):
        cols = slice(hp * LANES, (hp + 1) * LANES)
        xi = decay_ref[hp, 0]
        zeta = decay_ref[hp, 1]
        chunk_decay = decay_ref[hp, 2]
        decay0 = decay_ref[hp, 3]
        decay1 = decay_ref[hp, 4]
        gain = gain_ref[:, cols]
        for ci in range(tm // c):
            sl = slice(ci * c, (ci + 1) * c)
            q = q_ref[sl, cols]
            kf = k_ref[sl, cols] * (HEAD_DIM ** -0.5)
            v = v_ref[sl, cols]
            kb = kf.astype(BF16)
            zero = jnp.zeros_like(q)
            q0 = jnp.where(head0, q, zero)
            q1 = jnp.where(head0, zero, q)
            s0 = lax.dot_general(q0, kb, nt, preferred_element_type=F32) * decay0
            s1 = lax.dot_general(q1, kb, nt, preferred_element_type=F32) * decay1
            inner = jnp.where(head0,
                              jnp.dot(s0.astype(BF16), v, preferred_element_type=F32),
                              jnp.dot(s1.astype(BF16), v, preferred_element_type=F32))
            state = state_ref[hp]
            cross = jnp.dot(q, state.astype(BF16), preferred_element_type=F32) * xi
            kv = lax.dot_general((kf * zeta).astype(BF16), v, tn, preferred_element_type=F32)
            state_ref[hp] = state * chunk_decay + jnp.where(same_head, kv, 0.0)

            y = inner + cross
            inv = 1.0 / HEAD_DIM
            mu = jnp.where(head0,
                           jnp.sum(jnp.where(head0, y, 0.0), axis=-1, keepdims=True),
                           jnp.sum(jnp.where(head0, 0.0, y), axis=-1, keepdims=True)) * inv
            d = y - mu
            dd = d * d
            var = jnp.where(head0,
                            jnp.sum(jnp.where(head0, dd, 0.0), axis=-1, keepdims=True),
                            jnp.sum(jnp.where(head0, 0.0, dd), axis=-1, keepdims=True)) * inv
            yn = d * lax.rsqrt(var + LN_EPS) * gain
            o_ref[sl, cols] = (yn * jax.nn.silu(g_ref[sl, cols])).astype(o_ref.dtype)


def _retention(qa, ka, va, ga, gn_gain, log_gamma, batch, tm=512):
    t, width = qa.shape
    s = t // batch
    nt = s // tm
    pairs = width // LANES
    blk = lambda b, i: (b * nt + i, 0)
    return pl.pallas_call(
        _retention_kernel,
        grid=(batch, nt),
        in_specs=[pl.BlockSpec(memory_space=pltpu.SMEM)]
        + [pl.BlockSpec((tm, width), blk)] * 4
        + [pl.BlockSpec((1, width), lambda b, i: (0, 0))],
        out_specs=pl.BlockSpec((tm, width), blk),
        out_shape=jax.ShapeDtypeStruct((t, width), BF16),
        scratch_shapes=[pltpu.VMEM((pairs, LANES, LANES), F32),
                        pltpu.VMEM((pairs, 5, RET_CHUNK, LANES), F32)],
        compiler_params=_params("parallel", "arbitrary"),
        name="retention",
    )(log_gamma, qa, ka, va, ga, gn_gain.reshape(1, width))


def _moba_select_kernel(q_ref, km_ref, qe_ref, qo_ref):
    ts = q_ref.shape[0]
    nb = km_ref.shape[0]
    qf = q_ref[...]
    km = km_ref[...]
    lane = lax.broadcasted_iota(jnp.int32, (ts, LANES), 1)
    even_data = lane < HEAD_DIM
    nt = (((1,), (1,)), ((), ()))
    blk = lax.broadcasted_iota(jnp.int32, (nb, ts), 0).astype(F32)
    col = lax.broadcasted_iota(jnp.int32, (nb, ts), 1)
    own = ((pl.program_id(2) * ts + col) // MOBA_BLOCK).astype(F32)
    past = blk < own
    place_row = lax.broadcasted_iota(jnp.int32, (nb, LANES), 0)
    place_lane = lax.broadcasted_iota(jnp.int32, (nb, LANES), 1)

    for h, out_ref in enumerate((qe_ref, qo_ref)):
        data = even_data if h == 0 else jnp.logical_not(even_data)
        qh = jnp.where(data, qf, 0.0)
        gate = lax.dot_general(km, qh, nt, preferred_element_type=F32,
                               precision=lax.Precision.HIGHEST)
        g = jnp.where(past, gate, NEG_INF)
        sel = jnp.zeros((nb, ts), F32)
        for _ in range(min(MOBA_TOPK, nb)):
            m = jnp.max(g, axis=0, keepdims=True)
            first = jnp.min(jnp.where(g == m, blk, float(nb)), axis=0, keepdims=True)
            pick = blk == first
            sel = jnp.where(pick, 1.0, sel)
            g = jnp.where(pick, NEG_INF, g)
        sel = jnp.where(past, sel, 0.0)
        sel = jnp.where(blk == own, 1.0, sel)
        spare_base = HEAD_DIM if h == 0 else 0
        place = (place_lane == place_row + spare_base).astype(BF16)
        placed = lax.dot_general(sel.astype(BF16), place, (((0,), (0,)), ((), ())),
                                 preferred_element_type=F32)
        bias = (1.0 - placed) * MASK_BIAS
        out_ref[...] = jnp.where(data, qh * (HEAD_DIM ** -0.5 * LOG2_E), bias).astype(out_ref.dtype)


def _moba_select(qb, kmean, batch, ts=2048):
    pairs, t, _ = qb.shape
    s = t // batch
    nb = s // MOBA_BLOCK
    ts = min(ts, s)
    assert nb <= HEAD_DIM and ts % MOBA_BLOCK == 0
    km = kmean.reshape(batch, nb, pairs * LANES)
    qblk = lambda b, h, i: (h, b * (s // ts) + i, 0)
    return pl.pallas_call(
        _moba_select_kernel,
        grid=(batch, pairs, s // ts),
        in_specs=[pl.BlockSpec((None, ts, LANES), qblk),
                  pl.BlockSpec((None, nb, LANES), lambda b, h, i: (b, 0, h))],
        out_specs=[pl.BlockSpec((None, ts, LANES), qblk)] * 2,
        out_shape=[jax.ShapeDtypeStruct((pairs, t, LANES), BF16)] * 2,
        compiler_params=_params("parallel", "parallel", "parallel"),
        name="moba_select",
    )(qb, km)


def _moba_kernel(qe_ref, qo_ref, ke_ref, ko_ref, ve_ref, vo_ref, o_ref, s_ref, m_ref, acc_ref):
    qi = pl.program_id(2)
    tq = qe_ref.shape[0]
    slab = MOBA_GROUP * MOBA_BLOCK
    nt = (((1,), (1,)), ((), ()))
    last = qi // MOBA_GROUP
    heads = ((qe_ref, ke_ref, ve_ref), (qo_ref, ko_ref, vo_ref))
    r = lax.broadcasted_iota(jnp.int32, (tq, slab), 0)
    c = lax.broadcasted_iota(jnp.int32, (tq, slab), 1)
    causal = c - r <= (qi - last * MOBA_GROUP) * MOBA_BLOCK

    def scores(groups, masked):
        for h, (q_ref, k_ref, _) in enumerate(heads):
            mx = None if masked else m_ref[h]
            for t in groups:
                off = t * slab
                sb = lax.dot_general(q_ref[...], k_ref[pl.ds(off, slab), :], nt,
                                     preferred_element_type=F32)
                if masked:
                    sb = jnp.where(causal, sb, NEG_INF)
                s_ref[h, t] = sb
                for u in range(slab // LANES):
                    part = sb[:, u * LANES:(u + 1) * LANES]
                    mx = part if mx is None else jnp.maximum(mx, part)
            m_ref[h] = mx

    def weighted_values(groups, first):
        for h, (_, _, v_ref) in enumerate(heads):
            acc = None if first else acc_ref[h]
            for t in groups:
                off = t * slab
                p = jnp.exp2(s_ref[h, t] - row_max[h]).astype(BF16)
                pv = jnp.dot(p, v_ref[pl.ds(off, slab), :], preferred_element_type=F32)
                acc = pv if acc is None else acc + pv
            acc_ref[h] = acc

    row_max = [None, None]
    for n_past in range(s_ref.shape[1]):
        @pl.when(last == n_past)
        def _(n_past=n_past):
            scores((n_past,), True)
            if n_past:
                scores(tuple(range(n_past)), False)
            for h in range(2):
                row_max[h] = jnp.max(m_ref[h], axis=-1, keepdims=True)
            weighted_values(tuple(range(n_past + 1)), True)

    lane = lax.broadcasted_iota(jnp.int32, (tq, LANES), 1)
    a0 = acc_ref[0]
    a1 = acc_ref[1]
    o0 = a0 / pltpu.roll(a0, HEAD_DIM, 1)
    o1 = a1 / pltpu.roll(a1, HEAD_DIM, 1)
    o_ref[...] = jnp.where(lane < HEAD_DIM, o0, o1).astype(o_ref.dtype)


def _moba(qbe, qbo, kbe, kbo, vbe, vbo, batch):
    pairs, t, _ = qbe.shape
    s = t // batch
    tq = MOBA_BLOCK
    nq = s // tq
    nb = s // MOBA_BLOCK
    assert nb % MOBA_GROUP == 0
    qblk = lambda b, h, i: (h, b * nq + i, 0)
    kvblk = lambda b, h, i: (h, b, 0)
    return pl.pallas_call(
        _moba_kernel,
        grid=(batch, pairs, nq),
        in_specs=[pl.BlockSpec((None, tq, LANES), qblk)] * 2
        + [pl.BlockSpec((None, s, LANES), kvblk)] * 4,
        out_specs=pl.BlockSpec((None, tq, LANES), qblk),
        out_shape=jax.ShapeDtypeStruct((pairs, t, LANES), BF16),
        scratch_shapes=[pltpu.VMEM((2, nb // MOBA_GROUP, tq, MOBA_GROUP * MOBA_BLOCK), F32),
                        pltpu.VMEM((2, tq, LANES), F32),
                        pltpu.VMEM((2, tq, LANES), F32)],
        compiler_params=_params("parallel", "parallel", "arbitrary"),
        name="moba",
    )(qbe, qbo, kbe, kbo, vbe, vbo)


def _out_proj_kernel(ya_ref, yb_ref, w_ref, x_ref, g_ref, b_ref, o_ref):
    ka = ya_ref.shape[1]
    yb = jnp.concatenate([yb_ref[p] for p in range(yb_ref.shape[0])], axis=1)
    y = (jnp.dot(ya_ref[...], w_ref[:ka, :], preferred_element_type=F32)
         + jnp.dot(yb, w_ref[ka:, :], preferred_element_type=F32))
    z = DEEPNORM_ALPHA * x_ref[...] + y
    o_ref[...] = _layer_norm_rows(z, g_ref[...], b_ref[...])


def _out_proj_ln(ya, yb, w_bf16, x2d, gain, bias, tm=512):
    t, d = x2d.shape
    pairs = yb.shape[0]
    row = lambda i: (i, 0)
    const = lambda i: (0, 0)
    return pl.pallas_call(
        _out_proj_kernel,
        grid=(t // tm,),
        in_specs=[pl.BlockSpec((tm, ya.shape[1]), row),
                  pl.BlockSpec((pairs, tm, LANES), lambda i: (0, i, 0)),
                  pl.BlockSpec(w_bf16.shape, const),
                  pl.BlockSpec((tm, d), row),
                  pl.BlockSpec((1, d), const),
                  pl.BlockSpec((1, d), const)],
        out_specs=pl.BlockSpec((tm, d), row),
        out_shape=jax.ShapeDtypeStruct((t, d), F32),
        compiler_params=_params("parallel"),
        name="out_proj_ln",
    )(ya, yb, w_bf16, x2d, gain.reshape(1, d), bias.reshape(1, d))


def _ffn_kernel(x_ref, w1_ref, w2_ref, g_ref, b_ref, o_ref, *, row_split, th):
    tm = x_ref.shape[0]
    hdim = w1_ref.shape[1]
    rows = tm // row_split
    for part in range(row_split):
        sl = slice(part * rows, (part + 1) * rows)
        x = x_ref[sl, :]
        xb = x.astype(BF16)
        y = None
        for c in range(hdim // th):
            h = jnp.dot(xb, w1_ref[:, c * th:(c + 1) * th], preferred_element_type=F32)
            h = jnp.square(jnp.maximum(h, 0.0)).astype(BF16)
            yc = jnp.dot(h, w2_ref[c * th:(c + 1) * th, :], preferred_element_type=F32)
            y = yc if y is None else y + yc
        z = DEEPNORM_ALPHA * x + y
        o_ref[sl, :] = _layer_norm_rows(z, g_ref[...], b_ref[...])


def _ffn_ln(x2d, w1_bf16, w2_bf16, gain, bias, tm=1024, th=1024, row_split=2):
    t, d = x2d.shape
    hdim = w1_bf16.shape[1]
    row = lambda i: (i, 0)
    const = lambda i: (0, 0)
    resident = pl.Buffered(1)
    return pl.pallas_call(
        functools.partial(_ffn_kernel, row_split=row_split, th=th),
        grid=(t // tm,),
        in_specs=[pl.BlockSpec((tm, d), row),
                  pl.BlockSpec((d, hdim), const, pipeline_mode=resident),
                  pl.BlockSpec((hdim, d), const, pipeline_mode=resident),
                  pl.BlockSpec((1, d), const),
                  pl.BlockSpec((1, d), const)],
        out_specs=pl.BlockSpec((tm, d), row),
        out_shape=jax.ShapeDtypeStruct((t, d), F32),
        compiler_params=_params("parallel"),
        name="ffn_ln",
    )(x2d, w1_bf16, w2_bf16, gain.reshape(1, d), bias.reshape(1, d))


def _gmlp_kernel(x_ref, w_ref, b_ref, lng_ref, lnb_ref, ws_ref, bst_ref, wo_ref, g_ref, beta_ref,
                 o_ref, v_ref, gated_ref, *, row_split):
    tm = x_ref.shape[0]
    gw = v_ref.shape[1]
    c = GMLP_CHUNK
    cw = gw // GMLP_GROUPS
    part_rows = tm // row_split
    r = lax.broadcasted_iota(jnp.int32, (c, c), 0)
    cc = lax.broadcasted_iota(jnp.int32, (c, c), 1)
    causal = cc <= r
    bst = bst_ref[...]
    ws = [jnp.where(causal, ws_ref[g], 0.0).astype(BF16) for g in range(GMLP_GROUPS)]
    for part in range(row_split):
        base = part * part_rows
        prow = slice(base, base + part_rows)
        x = x_ref[prow, :]
        xb = x.astype(BF16)
        v = jnp.dot(xb, w_ref[:, gw:], preferred_element_type=F32) + b_ref[:, gw:]
        v_ref[prow, :] = _layer_norm_rows(jax.nn.gelu(v), lng_ref[...], lnb_ref[...]).astype(BF16)
        for g in range(GMLP_GROUPS):
            cols = slice(g * cw, (g + 1) * cw)
            u = jax.nn.gelu(jnp.dot(xb, w_ref[:, cols], preferred_element_type=F32) + b_ref[:, cols])
            for ci in range(part_rows // c):
                rows = slice(base + ci * c, base + (ci + 1) * c)
                sv = jnp.dot(ws[g], v_ref[rows, cols], preferred_element_type=F32) + bst[:, g:g + 1]
                gated_ref[rows, cols] = (u[ci * c:(ci + 1) * c] * sv).astype(BF16)
        y = jnp.dot(gated_ref[prow, :], wo_ref[...], preferred_element_type=F32)
        z = DEEPNORM_ALPHA * x + y
        o_ref[prow, :] = _layer_norm_rows(z, g_ref[...], beta_ref[...])


def _gmlp_mixer_ln(x2d, w_uv_bf16, b_uv, ln_g, ln_b, w_s, b_s, w_out_bf16, gain, bias,
                   tm=1024, row_split=2):
    t, d = x2d.shape
    n = w_uv_bf16.shape[1]
    gw = n // 2
    row = lambda i: (i, 0)
    const = lambda i: (0, 0)
    resident = pl.Buffered(1)
    return pl.pallas_call(
        functools.partial(_gmlp_kernel, row_split=row_split),
        grid=(t // tm,),
        in_specs=[pl.BlockSpec((tm, d), row),
                  pl.BlockSpec((d, n), const, pipeline_mode=resident),
                  pl.BlockSpec((1, n), const),
                  pl.BlockSpec((1, gw), const),
                  pl.BlockSpec((1, gw), const),
                  pl.BlockSpec(w_s.shape, lambda i: (0, 0, 0)),
                  pl.BlockSpec((GMLP_CHUNK, GMLP_GROUPS), const),
                  pl.BlockSpec((gw, d), const, pipeline_mode=resident),
                  pl.BlockSpec((1, d), const),
                  pl.BlockSpec((1, d), const)],
        out_specs=pl.BlockSpec((tm, d), row),
        out_shape=jax.ShapeDtypeStruct((t, d), F32),
        scratch_shapes=[pltpu.VMEM((tm, gw), BF16), pltpu.VMEM((tm, gw), BF16)],
        compiler_params=_params("parallel"),
        name="gmlp_mixer_ln",
    )(x2d, w_uv_bf16, b_uv.reshape(1, n), ln_g.reshape(1, gw), ln_b.reshape(1, gw), w_s, b_s.T,
      w_out_bf16, gain.reshape(1, d), bias.reshape(1, d))


def kernel(x, positions, ln_gain, ln_bias, mix_w_in, ret_gn_gain, mix_w_out,
           gmlp_w_uv, gmlp_b_uv, gmlp_ln_gain, gmlp_ln_bias, gmlp_w_s, gmlp_b_s,
           gmlp_w_out, ffn_w_in, ffn_w_out):
    batch, seq, d = x.shape
    depth = ln_gain.shape[0]
    h = x.reshape(batch * seq, d)
    cos_t, sin_t = _rope_tables(positions)
    log_gamma = jnp.log1p(-jnp.exp2(-5.0 - jnp.arange(N_RET_HEADS, dtype=F32)))
    for layer in range(depth):
        i = layer // 2
        if layer % 2 == 0:
            qa, ka, va, ga, qb, kbe, kbo, vbe, vbo, kmean = _in_proj(
                h, mix_w_in[i].astype(BF16), cos_t, sin_t, seq // MOBA_BLOCK)
            ya = _retention(qa, ka, va, ga, ret_gn_gain[i], log_gamma, batch)
            qbe, qbo = _moba_select(qb, kmean, batch)
            yb = _moba(qbe, qbo, kbe, kbo, vbe, vbo, batch)
            h = _out_proj_ln(ya, yb, mix_w_out[i].astype(BF16), h,
                             ln_gain[layer, 0], ln_bias[layer, 0])
        else:
            h = _gmlp_mixer_ln(h, gmlp_w_uv[i].astype(BF16), gmlp_b_uv[i], gmlp_ln_gain[i],
                               gmlp_ln_bias[i], gmlp_w_s[i], gmlp_b_s[i],
                               gmlp_w_out[i].astype(BF16), ln_gain[layer, 0], ln_bias[layer, 0])
        h = _ffn_ln(h, ffn_w_in[layer].astype(BF16), ffn_w_out[layer].astype(BF16),
                    ln_gain[layer, 1], ln_bias[layer, 1])
    return h.reshape(batch, seq, d)
```

```python
import functools

import jax
import jax.numpy as jnp
from jax import lax
from jax.experimental import pallas as pl
from jax.experimental.pallas import tpu as pltpu

HEAD_DIM = 64
N_RET_HEADS = 8
N_MOBA_HEADS = 8
RET_WIDTH = N_RET_HEADS * HEAD_DIM
MOBA_WIDTH = N_MOBA_HEADS * HEAD_DIM
RET_CHUNK = 128
MOBA_BLOCK = 256
MOBA_TOPK = 3
GMLP_CHUNK = 128
GMLP_GROUPS = 8
ROPE_THETA = 10000.0
LN_EPS = 1e-5
DEPTH = 4
DEEPNORM_ALPHA = (2 * DEPTH) ** 0.25

LANES = 128
VMEM_LIMIT_BYTES = 52 * 1024 * 1024

BF16 = jnp.bfloat16
F32 = jnp.float32
NEG_INF = float("-inf")
MOBA_GROUP = 4
MASK_BIAS = -1e30
LOG2_E = 1.4426950408889634


def _params(*sem):
    return pltpu.CompilerParams(dimension_semantics=sem, vmem_limit_bytes=VMEM_LIMIT_BYTES)


def _layer_norm_rows(z, g, b):
    mu = jnp.mean(z, axis=-1, keepdims=True)
    d = z - mu
    var = jnp.mean(d * d, axis=-1, keepdims=True)
    return d * lax.rsqrt(var + LN_EPS) * g + b


def _rope_table_kernel(pos_ref, freq_ref, cos_ref, sin_ref):
    ang = pos_ref[...] * freq_ref[...]
    lane = lax.broadcasted_iota(jnp.int32, ang.shape, 1)
    first_half = (lane % HEAD_DIM) < (HEAD_DIM // 2)
    cos_ref[...] = jnp.cos(ang)
    sin_ref[...] = jnp.where(first_half, -jnp.sin(ang), jnp.sin(ang))


def _rope_tables(positions):
    t = positions.size
    tm = min(t, 2048)
    pos = positions.reshape(t, 1).astype(F32)
    inv_freq = ROPE_THETA ** (-jnp.arange(0, HEAD_DIM, 2, dtype=F32) / HEAD_DIM)
    freq = jnp.tile(inv_freq, LANES // (HEAD_DIM // 2)).reshape(1, LANES)
    return pl.pallas_call(
        _rope_table_kernel,
        grid=(t // tm,),
        in_specs=[pl.BlockSpec((tm, 1), lambda i: (i, 0)),
                  pl.BlockSpec((1, LANES), lambda i: (0, 0))],
        out_specs=[pl.BlockSpec((tm, LANES), lambda i: (i, 0))] * 2,
        out_shape=[jax.ShapeDtypeStruct((t, LANES), F32)] * 2,
        compiler_params=_params("parallel"),
        name="rope_tables",
    )(pos, freq)


def _in_proj_kernel(x_ref, w_ref, cos_ref, sin_ref,
                    qa_ref, ka_ref, va_ref, ga_ref, qb_ref, kbe_ref, kbo_ref, vbe_ref, vbo_ref,
                    km_ref, *, n_blocks):
    tm = x_ref.shape[0]
    width = RET_WIDTH
    xb = x_ref[...].astype(BF16)
    reps = width // LANES
    cos = jnp.concatenate([cos_ref[...]] * reps, axis=1)
    sin = jnp.concatenate([sin_ref[...]] * reps, axis=1)
    lane = lax.broadcasted_iota(jnp.int32, (tm, width), 1)
    first_half = (lane % HEAD_DIM) < (HEAD_DIM // 2)

    def proj(g):
        return jnp.dot(xb, w_ref[:, g * width:(g + 1) * width], preferred_element_type=F32)

    def rope(t):
        partner = jnp.where(first_half,
                            pltpu.roll(t, width - HEAD_DIM // 2, 1),
                            pltpu.roll(t, HEAD_DIM // 2, 1))
        return t * cos + partner * sin

    qa_ref[...] = rope(proj(0)).astype(qa_ref.dtype)
    ka_ref[...] = rope(proj(1)).astype(ka_ref.dtype)
    va_ref[...] = proj(2).astype(va_ref.dtype)
    ga_ref[...] = proj(3).astype(ga_ref.dtype)
    def store_pairs(ref, val):
        for p in range(width // LANES):
            ref[p] = val[:, p * LANES:(p + 1) * LANES].astype(ref.dtype)

    store_pairs(qb_ref, rope(proj(4)))
    kb = rope(proj(5))
    vb = proj(6)
    pair_lane = lane % LANES
    even_data = pair_lane < HEAD_DIM
    row = lax.broadcasted_iota(jnp.int32, (tm, width), 0)
    block = ((pl.program_id(0) * tm + row) // MOBA_BLOCK) % n_blocks
    onehot_e = (pair_lane - HEAD_DIM == block).astype(F32)
    onehot_o = (pair_lane == block).astype(F32)
    def store_pairs_transposed(ref, val):
        for p in range(width // LANES):
            ref[p] = val[:, p * LANES:(p + 1) * LANES].T.astype(ref.dtype)

    store_pairs_transposed(kbe_ref, jnp.where(even_data, kb, onehot_e))
    store_pairs_transposed(kbo_ref, jnp.where(even_data, onehot_o, kb))
    store_pairs(vbe_ref, jnp.where(even_data, vb, 1.0))
    store_pairs(vbo_ref, jnp.where(even_data, 1.0, vb))
    for blk in range(tm // MOBA_BLOCK):
        km_ref[blk] = jnp.mean(kb[blk * MOBA_BLOCK:(blk + 1) * MOBA_BLOCK], axis=0, keepdims=True)


def _in_proj(x2d, w_bf16, cos_t, sin_t, n_blocks, tm=512):
    t, d = x2d.shape
    n = w_bf16.shape[1]
    width = RET_WIDTH
    assert n_blocks <= HEAD_DIM and tm % MOBA_BLOCK == 0
    row = lambda i: (i, 0)
    pairs = width // LANES
    row_dtypes = [BF16, F32, BF16, F32]
    pair_spec = pl.BlockSpec((pairs, tm, LANES), lambda i: (0, i, 0))
    pair_t_spec = pl.BlockSpec((pairs, LANES, tm), lambda i: (0, 0, i))
    pair_outs = [(F32, False), (BF16, True), (BF16, True), (BF16, False), (BF16, False)]
    return pl.pallas_call(
        functools.partial(_in_proj_kernel, n_blocks=n_blocks),
        grid=(t // tm,),
        in_specs=[pl.BlockSpec((tm, d), row),
                  pl.BlockSpec((d, n), lambda i: (0, 0)),
                  pl.BlockSpec((tm, LANES), row),
                  pl.BlockSpec((tm, LANES), row)],
        out_specs=[pl.BlockSpec((tm, width), row)] * len(row_dtypes)
        + [pair_t_spec if transposed else pair_spec for _, transposed in pair_outs]
        + [pl.BlockSpec((tm // MOBA_BLOCK, 1, width), lambda i: (i, 0, 0))],
        out_shape=[jax.ShapeDtypeStruct((t, width), dt) for dt in row_dtypes]
        + [jax.ShapeDtypeStruct((pairs, LANES, t) if transposed else (pairs, t, LANES), dt)
           for dt, transposed in pair_outs]
        + [jax.ShapeDtypeStruct((t // MOBA_BLOCK, 1, width), F32)],
        compiler_params=_params("parallel"),
        name="in_proj",
    )(x2d, w_bf16, cos_t, sin_t)


def _retention_kernel(lg_ref, q_ref, k_ref, v_ref, g_ref, gain_ref, o_ref, state_ref, decay_ref):
    c = RET_CHUNK
    tm, width = q_ref.shape
    lane = lax.broadcasted_iota(jnp.int32, (c, LANES), 1)
    row = lax.broadcasted_iota(jnp.int32, (c, LANES), 0)
    head0 = lane < HEAD_DIM
    same_head = (row < HEAD_DIM) == head0
    nt = (((1,), (1,)), ((), ()))
    tn = (((0,), (0,)), ((), ()))

    @pl.when(pl.program_id(1) == 0)
    def _():
        state_ref[...] = jnp.zeros_like(state_ref)
        idx = row.astype(F32)
        diff = (row - lane).astype(F32)
        dpos = jnp.maximum(diff, 0.0)
        for hp in range(width // LANES):
            lg0 = lg_ref[2 * hp]
            lg1 = lg_ref[2 * hp + 1]
            lg_lane = jnp.where(head0, lg0, lg1)
            decay_ref[hp, 0] = jnp.exp(lg_lane * (idx + 1.0))
            decay_ref[hp, 1] = jnp.exp(lg_lane * (c - 1.0 - idx))
            decay_ref[hp, 2] = jnp.exp(lg_lane * float(c))
            decay_ref[hp, 3] = jnp.where(diff >= 0, jnp.exp(lg0 * dpos), 0.0)
            decay_ref[hp, 4] = jnp.where(diff >= 0, jnp.exp(lg1 * dpos), 0.0)

    for hp in range(width // LANES):
        cols = slice(hp * LANES, (hp + 1) * LANES)
        xi = decay_ref[hp, 0]
        zeta = decay_ref[hp, 1]
        chunk_decay = decay_ref[hp, 2]
        decay0 = decay_ref[hp, 3]
        decay1 = decay_ref[hp, 4]
        gain = gain_ref[:, cols]
        for ci in range(tm // c):
            sl = slice(ci * c, (ci + 1) * c)
            q = q_ref[sl, cols]
            kf = k_ref[sl, cols] * (HEAD_DIM ** -0.5)
            v = v_ref[sl, cols]
            kb = kf.astype(BF16)
            zero = jnp.zeros_like(q)
            q0 = jnp.where(head0, q, zero)
            q1 = jnp.where(head0, zero, q)
            s0 = lax.dot_general(q0, kb, nt, preferred_element_type=F32) * decay0
            s1 = lax.dot_general(q1, kb, nt, preferred_element_type=F32) * decay1
            inner = jnp.where(head0,
                              jnp.dot(s0.astype(BF16), v, preferred_element_type=F32),
                              jnp.dot(s1.astype(BF16), v, preferred_element_type=F32))
            state = state_ref[hp]
            cross = jnp.dot(q, state.astype(BF16), preferred_element_type=F32) * xi
            kv = lax.dot_general((kf * zeta).astype(BF16), v, tn, preferred_element_type=F32)
            state_ref[hp] = state * chunk_decay + jnp.where(same_head, kv, 0.0)

            y = inner + cross
            inv = 1.0 / HEAD_DIM
            mu = jnp.where(head0,
                           jnp.sum(jnp.where(head0, y, 0.0), axis=-1, keepdims=True),
                           jnp.sum(jnp.where(head0, 0.0, y), axis=-1, keepdims=True)) * inv
            d = y - mu
            dd = d * d
            var = jnp.where(head0,
                            jnp.sum(jnp.where(head0, dd, 0.0), axis=-1, keepdims=True),
                            jnp.sum(jnp.where(head0, 0.0, dd), axis=-1, keepdims=True)) * inv
            yn = d * lax.rsqrt(var + LN_EPS) * gain
            o_ref[sl, cols] = (yn * jax.nn.silu(g_ref[sl, cols])).astype(o_ref.dtype)


def _retention(qa, ka, va, ga, gn_gain, log_gamma, batch, tm=512):
    t, width = qa.shape
    s = t // batch
    nt = s // tm
    pairs = width // LANES
    blk = lambda b, i: (b * nt + i, 0)
    return pl.pallas_call(
        _retention_kernel,
        grid=(batch, nt),
        in_specs=[pl.BlockSpec(memory_space=pltpu.SMEM)]
        + [pl.BlockSpec((tm, width), blk)] * 4
        + [pl.BlockSpec((1, width), lambda b, i: (0, 0))],
        out_specs=pl.BlockSpec((tm, width), blk),
        out_shape=jax.ShapeDtypeStruct((t, width), BF16),
        scratch_shapes=[pltpu.VMEM((pairs, LANES, LANES), F32),
                        pltpu.VMEM((pairs, 5, RET_CHUNK, LANES), F32)],
        compiler_params=_params("parallel", "arbitrary"),
        name="retention",
    )(log_gamma, qa, ka, va, ga, gn_gain.reshape(1, width))


def _moba_select_kernel(q_ref, km_ref, qe_ref, qo_ref):
    ts = q_ref.shape[0]
    nb = km_ref.shape[0]
    qf = q_ref[...]
    km = km_ref[...]
    lane = lax.broadcasted_iota(jnp.int32, (ts, LANES), 1)
    even_data = lane < HEAD_DIM
    nt = (((1,), (1,)), ((), ()))
    blk = lax.broadcasted_iota(jnp.int32, (nb, ts), 0).astype(F32)
    col = lax.broadcasted_iota(jnp.int32, (nb, ts), 1)
    own = ((pl.program_id(2) * ts + col) // MOBA_BLOCK).astype(F32)
    past = blk < own
    place_row = lax.broadcasted_iota(jnp.int32, (nb, LANES), 0)
    place_lane = lax.broadcasted_iota(jnp.int32, (nb, LANES), 1)

    for h, out_ref in enumerate((qe_ref, qo_ref)):
        data = even_data if h == 0 else jnp.logical_not(even_data)
        qh = jnp.where(data, qf, 0.0)
        gate = lax.dot_general(km, qh, nt, preferred_element_type=F32,
                               precision=lax.Precision.HIGHEST)
        g = jnp.where(past, gate, NEG_INF)
        sel = jnp.zeros((nb, ts), F32)
        for _ in range(min(MOBA_TOPK, nb)):
            m = jnp.max(g, axis=0, keepdims=True)
            first = jnp.min(jnp.where(g == m, blk, float(nb)), axis=0, keepdims=True)
            pick = blk == first
            sel = jnp.where(pick, 1.0, sel)
            g = jnp.where(pick, NEG_INF, g)
        sel = jnp.where(past, sel, 0.0)
        sel = jnp.where(blk == own, 1.0, sel)
        spare_base = HEAD_DIM if h == 0 else 0
        place = (place_lane == place_row + spare_base).astype(BF16)
        placed = lax.dot_general(sel.astype(BF16), place, (((0,), (0,)), ((), ())),
                                 preferred_element_type=F32)
        bias = (1.0 - placed) * MASK_BIAS
        out_ref[...] = jnp.where(data, qh * (HEAD_DIM ** -0.5 * LOG2_E), bias).astype(out_ref.dtype)


def _moba_select(qb, kmean, batch, ts=2048):
    pairs, t, _ = qb.shape
    s = t // batch
    nb = s // MOBA_BLOCK
    ts = min(ts, s)
    assert nb <= HEAD_DIM and ts % MOBA_BLOCK == 0
    km = kmean.reshape(batch, nb, pairs * LANES)
    qblk = lambda b, h, i: (h, b * (s // ts) + i, 0)
    return pl.pallas_call(
        _moba_select_kernel,
        grid=(batch, pairs, s // ts),
        in_specs=[pl.BlockSpec((None, ts, LANES), qblk),
                  pl.BlockSpec((None, nb, LANES), lambda b, h, i: (b, 0, h))],
        out_specs=[pl.BlockSpec((None, ts, LANES), qblk)] * 2,
        out_shape=[jax.ShapeDtypeStruct((pairs, t, LANES), BF16)] * 2,
        compiler_params=_params("parallel", "parallel", "parallel"),
        name="moba_select",
    )(qb, km)


def _moba_kernel(qe_ref, qo_ref, ke_ref, ko_ref, ve_ref, vo_ref, o_ref, s_ref, m_ref, acc_ref):
    qi = pl.program_id(2)
    tq = qe_ref.shape[0]
    slab = MOBA_GROUP * MOBA_BLOCK
    nt = (((1,), (1,)), ((), ()))
    last = qi // MOBA_GROUP
    heads = ((qe_ref, ke_ref, ve_ref), (qo_ref, ko_ref, vo_ref))
    r = lax.broadcasted_iota(jnp.int32, (tq, slab), 0)
    c = lax.broadcasted_iota(jnp.int32, (tq, slab), 1)
    causal = c - r <= (qi - last * MOBA_GROUP) * MOBA_BLOCK

    def scores(groups, masked):
        for h, (q_ref, k_ref, _) in enumerate(heads):
            mx = None if masked else m_ref[h]
            for t in groups:
                off = t * slab
                sb = jnp.dot(q_ref[...], k_ref[:, off:off + slab],
                             preferred_element_type=F32)
                if masked:
                    sb = jnp.where(causal, sb, NEG_INF)
                s_ref[h, t] = sb
                for u in range(slab // LANES):
                    part = sb[:, u * LANES:(u + 1) * LANES]
                    mx = part if mx is None else jnp.maximum(mx, part)
            m_ref[h] = mx

    def weighted_values(groups, first):
        for h, (_, _, v_ref) in enumerate(heads):
            acc = None if first else acc_ref[h]
            for t in groups:
                off = t * slab
                p = jnp.exp2(s_ref[h, t] - row_max[h]).astype(BF16)
                pv = jnp.dot(p, v_ref[pl.ds(off, slab), :], preferred_element_type=F32)
                acc = pv if acc is None else acc + pv
            acc_ref[h] = acc

    row_max = [None, None]
    for n_past in range(s_ref.shape[1]):
        @pl.when(last == n_past)
        def _(n_past=n_past):
            scores((n_past,), True)
            if n_past:
                scores(tuple(range(n_past)), False)
            for h in range(2):
                row_max[h] = jnp.max(m_ref[h], axis=-1, keepdims=True)
            weighted_values(tuple(range(n_past + 1)), True)

    lane = lax.broadcasted_iota(jnp.int32, (tq, LANES), 1)
    a0 = acc_ref[0]
    a1 = acc_ref[1]
    o0 = a0 / pltpu.roll(a0, HEAD_DIM, 1)
    o1 = a1 / pltpu.roll(a1, HEAD_DIM, 1)
    o_ref[...] = jnp.where(lane < HEAD_DIM, o0, o1).astype(o_ref.dtype)


def _moba(qbe, qbo, kbe, kbo, vbe, vbo, batch):
    pairs, t, _ = qbe.shape
    s = t // batch
    tq = MOBA_BLOCK
    nq = s // tq
    nb = s // MOBA_BLOCK
    assert nb % MOBA_GROUP == 0
    qblk = lambda b, h, i: (h, b * nq + i, 0)
    kblk = lambda b, h, i: (h, 0, b)
    vblk = lambda b, h, i: (h, b, 0)
    return pl.pallas_call(
        _moba_kernel,
        grid=(batch, pairs, nq),
        in_specs=[pl.BlockSpec((None, tq, LANES), qblk)] * 2
        + [pl.BlockSpec((None, LANES, s), kblk)] * 2 + [pl.BlockSpec((None, s, LANES), vblk)] * 2,
        out_specs=pl.BlockSpec((None, tq, LANES), qblk),
        out_shape=jax.ShapeDtypeStruct((pairs, t, LANES), BF16),
        scratch_shapes=[pltpu.VMEM((2, nb // MOBA_GROUP, tq, MOBA_GROUP * MOBA_BLOCK), F32),
                        pltpu.VMEM((2, tq, LANES), F32),
                        pltpu.VMEM((2, tq, LANES), F32)],
        compiler_params=_params("parallel", "parallel", "arbitrary"),
        name="moba",
    )(qbe, qbo, kbe, kbo, vbe, vbo)


def _out_proj_kernel(ya_ref, yb_ref, w_ref, x_ref, g_ref, b_ref, o_ref):
    ka = ya_ref.shape[1]
    yb = jnp.concatenate([yb_ref[p] for p in range(yb_ref.shape[0])], axis=1)
    y = (jnp.dot(ya_ref[...], w_ref[:ka, :], preferred_element_type=F32)
         + jnp.dot(yb, w_ref[ka:, :], preferred_element_type=F32))
    z = DEEPNORM_ALPHA * x_ref[...] + y
    o_ref[...] = _layer_norm_rows(z, g_ref[...], b_ref[...])


def _out_proj_ln(ya, yb, w_bf16, x2d, gain, bias, tm=512):
    t, d = x2d.shape
    pairs = yb.shape[0]
    row = lambda i: (i, 0)
    const = lambda i: (0, 0)
    return pl.pallas_call(
        _out_proj_kernel,
        grid=(t // tm,),
        in_specs=[pl.BlockSpec((tm, ya.shape[1]), row),
                  pl.BlockSpec((pairs, tm, LANES), lambda i: (0, i, 0)),
                  pl.BlockSpec(w_bf16.shape, const),
                  pl.BlockSpec((tm, d), row),
                  pl.BlockSpec((1, d), const),
                  pl.BlockSpec((1, d), const)],
        out_specs=pl.BlockSpec((tm, d), row),
        out_shape=jax.ShapeDtypeStruct((t, d), F32),
        compiler_params=_params("parallel"),
        name="out_proj_ln",
    )(ya, yb, w_bf16, x2d, gain.reshape(1, d), bias.reshape(1, d))


def _ffn_kernel(x_ref, w1_ref, w2_ref, g_ref, b_ref, o_ref, *, row_split, th):
    tm = x_ref.shape[0]
    hdim = w1_ref.shape[1]
    rows = tm // row_split
    for part in range(row_split):
        sl = slice(part * rows, (part + 1) * rows)
        x = x_ref[sl, :]
        xb = x.astype(BF16)
        y = None
        for c in range(hdim // th):
            h = jnp.dot(xb, w1_ref[:, c * th:(c + 1) * th], preferred_element_type=F32)
            h = jnp.square(jnp.maximum(h, 0.0)).astype(BF16)
            yc = jnp.dot(h, w2_ref[c * th:(c + 1) * th, :], preferred_element_type=F32)
            y = yc if y is None else y + yc
        z = DEEPNORM_ALPHA * x + y
        o_ref[sl, :] = _layer_norm_rows(z, g_ref[...], b_ref[...])


def _ffn_ln(x2d, w1_bf16, w2_bf16, gain, bias, tm=1024, th=1024, row_split=2):
    t, d = x2d.shape
    hdim = w1_bf16.shape[1]
    row = lambda i: (i, 0)
    const = lambda i: (0, 0)
    resident = pl.Buffered(1)
    return pl.pallas_call(
        functools.partial(_ffn_kernel, row_split=row_split, th=th),
        grid=(t // tm,),
        in_specs=[pl.BlockSpec((tm, d), row),
                  pl.BlockSpec((d, hdim), const, pipeline_mode=resident),
                  pl.BlockSpec((hdim, d), const, pipeline_mode=resident),
                  pl.BlockSpec((1, d), const),
                  pl.BlockSpec((1, d), const)],
        out_specs=pl.BlockSpec((tm, d), row),
        out_shape=jax.ShapeDtypeStruct((t, d), F32),
        compiler_params=_params("parallel"),
        name="ffn_ln",
    )(x2d, w1_bf16, w2_bf16, gain.reshape(1, d), bias.reshape(1, d))


def _gmlp_kernel(x_ref, w_ref, b_ref, lng_ref, lnb_ref, ws_ref, bst_ref, wo_ref, g_ref, beta_ref,
                 o_ref, v_ref, gated_ref, *, row_split):
    tm = x_ref.shape[0]
    gw = v_ref.shape[1]
    c = GMLP_CHUNK
    cw = gw // GMLP_GROUPS
    part_rows = tm // row_split
    r = lax.broadcasted_iota(jnp.int32, (c, c), 0)
    cc = lax.broadcasted_iota(jnp.int32, (c, c), 1)
    causal = cc <= r
    bst = bst_ref[...]
    ws = [jnp.where(causal, ws_ref[g], 0.0).astype(BF16) for g in range(GMLP_GROUPS)]
    for part in range(row_split):
        base = part * part_rows
        prow = slice(base, base + part_rows)
        x = x_ref[prow, :]
        xb = x.astype(BF16)
        v = jnp.dot(xb, w_ref[:, gw:], preferred_element_type=F32) + b_ref[:, gw:]
        v_ref[prow, :] = _layer_norm_rows(jax.nn.gelu(v), lng_ref[...], lnb_ref[...]).astype(BF16)
        for g in range(GMLP_GROUPS):
            cols = slice(g * cw, (g + 1) * cw)
            u = jax.nn.gelu(jnp.dot(xb, w_ref[:, cols], preferred_element_type=F32) + b_ref[:, cols])
            for ci in range(part_rows // c):
                rows = slice(base + ci * c, base + (ci + 1) * c)
                sv = jnp.dot(ws[g], v_ref[rows, cols], preferred_element_type=F32) + bst[:, g:g + 1]
                gated_ref[rows, cols] = (u[ci * c:(ci + 1) * c] * sv).astype(BF16)
        y = jnp.dot(gated_ref[prow, :], wo_ref[...], preferred_element_type=F32)
        z = DEEPNORM_ALPHA * x + y
        o_ref[prow, :] = _layer_norm_rows(z, g_ref[...], beta_ref[...])


def _gmlp_mixer_ln(x2d, w_uv_bf16, b_uv, ln_g, ln_b, w_s, b_s, w_out_bf16, gain, bias,
                   tm=1024, row_split=2):
    t, d = x2d.shape
    n = w_uv_bf16.shape[1]
    gw = n // 2
    row = lambda i: (i, 0)
    const = lambda i: (0, 0)
    resident = pl.Buffered(1)
    return pl.pallas_call(
        functools.partial(_gmlp_kernel, row_split=row_split),
        grid=(t // tm,),
        in_specs=[pl.BlockSpec((tm, d), row),
                  pl.BlockSpec((d, n), const, pipeline_mode=resident),
                  pl.BlockSpec((1, n), const),
                  pl.BlockSpec((1, gw), const),
                  pl.BlockSpec((1, gw), const),
                  pl.BlockSpec(w_s.shape, lambda i: (0, 0, 0)),
                  pl.BlockSpec((GMLP_CHUNK, GMLP_GROUPS), const),
                  pl.BlockSpec((gw, d), const, pipeline_mode=resident),
                  pl.BlockSpec((1, d), const),
                  pl.BlockSpec((1, d), const)],
        out_specs=pl.BlockSpec((tm, d), row),
        out_shape=jax.ShapeDtypeStruct((t, d), F32),
        scratch_shapes=[pltpu.VMEM((tm, gw), BF16), pltpu.VMEM((tm, gw), BF16)],
        compiler_params=_params("parallel"),
        name="gmlp_mixer_ln",
    )(x2d, w_uv_bf16, b_uv.reshape(1, n), ln_g.reshape(1, gw), ln_b.reshape(1, gw), w_s, b_s.T,
      w_out_bf16, gain.reshape(1, d), bias.reshape(1, d))


def kernel(x, positions, ln_gain, ln_bias, mix_w_in, ret_gn_gain, mix_w_out,
           gmlp_w_uv, gmlp_b_uv, gmlp_ln_gain, gmlp_ln_bias, gmlp_w_s, gmlp_b_s,
           gmlp_w_out, ffn_w_in, ffn_w_out):
    batch, seq, d = x.shape
    depth = ln_gain.shape[0]
    h = x.reshape(batch * seq, d)
    cos_t, sin_t = _rope_tables(positions)
    log_gamma = jnp.log1p(-jnp.exp2(-5.0 - jnp.arange(N_RET_HEADS, dtype=F32)))
    for layer in range(depth):
        i = layer // 2
        if layer % 2 == 0:
            qa, ka, va, ga, qb, kbe, kbo, vbe, vbo, kmean = _in_proj(
                h, mix_w_in[i].astype(BF16), cos_t, sin_t, seq // MOBA_BLOCK)
            ya = _retention(qa, ka, va, ga, ret_gn_gain[i], log_gamma, batch)
            qbe, qbo = _moba_select(qb, kmean, batch)
            yb = _moba(qbe, qbo, kbe, kbo, vbe, vbo, batch)
            h = _out_proj_ln(ya, yb, mix_w_out[i].astype(BF16), h,
                             ln_gain[layer, 0], ln_bias[layer, 0])
        else:
            h = _gmlp_mixer_ln(h, gmlp_w_uv[i].astype(BF16), gmlp_b_uv[i], gmlp_ln_gain[i],
                               gmlp_ln_bias[i], gmlp_w_s[i], gmlp_b_s[i],
                               gmlp_w_out[i].astype(BF16), ln_gain[layer, 0], ln_bias[layer, 0])
        h = _ffn_ln(h, ffn_w_in[layer].astype(BF16), ffn_w_out[layer].astype(BF16),
                    ln_gain[layer, 1], ln_bias[layer, 1])
    return h.reshape(batch, seq, d)
```

```python
import functools

import jax
import jax.numpy as jnp
from jax import lax
from jax.experimental import pallas as pl
from jax.experimental.pallas import tpu as pltpu

HEAD_DIM = 64
N_RET_HEADS = 8
N_MOBA_HEADS = 8
RET_WIDTH = N_RET_HEADS * HEAD_DIM
MOBA_WIDTH = N_MOBA_HEADS * HEAD_DIM
RET_CHUNK = 128
MOBA_BLOCK = 256
MOBA_TOPK = 3
GMLP_CHUNK = 128
GMLP_GROUPS = 8
ROPE_THETA = 10000.0
LN_EPS = 1e-5
DEPTH = 4
DEEPNORM_ALPHA = (2 * DEPTH) ** 0.25

LANES = 128
VMEM_LIMIT_BYTES = 52 * 1024 * 1024

BF16 = jnp.bfloat16
F32 = jnp.float32
NEG_INF = float("-inf")
MOBA_GROUP = 4
MASK_BIAS = -1e30
LOG2_E = 1.4426950408889634


def _params(*sem):
    return pltpu.CompilerParams(dimension_semantics=sem, vmem_limit_bytes=VMEM_LIMIT_BYTES)


def _layer_norm_rows(z, g, b):
    mu = jnp.mean(z, axis=-1, keepdims=True)
    d = z - mu
    var = jnp.mean(d * d, axis=-1, keepdims=True)
    return d * lax.rsqrt(var + LN_EPS) * g + b


def _rope_table_kernel(pos_ref, freq_ref, cos_ref, sin_ref):
    ang = pos_ref[...] * freq_ref[...]
    lane = lax.broadcasted_iota(jnp.int32, ang.shape, 1)
    first_half = (lane % HEAD_DIM) < (HEAD_DIM // 2)
    cos_ref[...] = jnp.cos(ang)
    sin_ref[...] = jnp.where(first_half, -jnp.sin(ang), jnp.sin(ang))


def _rope_tables(positions):
    t = positions.size
    tm = min(t, 2048)
    pos = positions.reshape(t, 1).astype(F32)
    inv_freq = ROPE_THETA ** (-jnp.arange(0, HEAD_DIM, 2, dtype=F32) / HEAD_DIM)
    freq = jnp.tile(inv_freq, LANES // (HEAD_DIM // 2)).reshape(1, LANES)
    return pl.pallas_call(
        _rope_table_kernel,
        grid=(t // tm,),
        in_specs=[pl.BlockSpec((tm, 1), lambda i: (i, 0)),
                  pl.BlockSpec((1, LANES), lambda i: (0, 0))],
        out_specs=[pl.BlockSpec((tm, LANES), lambda i: (i, 0))] * 2,
        out_shape=[jax.ShapeDtypeStruct((t, LANES), F32)] * 2,
        compiler_params=_params("parallel"),
        name="rope_tables",
    )(pos, freq)


def _in_proj_kernel(x_ref, w_ref, cos_ref, sin_ref,
                    qa_ref, ka_ref, va_ref, ga_ref, qb_ref, kbe_ref, kbo_ref, vbe_ref, vbo_ref,
                    km_ref, *, n_blocks):
    tm = x_ref.shape[0]
    width = RET_WIDTH
    xb = x_ref[...].astype(BF16)
    reps = width // LANES
    cos = jnp.concatenate([cos_ref[...]] * reps, axis=1)
    sin = jnp.concatenate([sin_ref[...]] * reps, axis=1)
    lane = lax.broadcasted_iota(jnp.int32, (tm, width), 1)
    first_half = (lane % HEAD_DIM) < (HEAD_DIM // 2)

    def proj(g):
        return jnp.dot(xb, w_ref[:, g * width:(g + 1) * width], preferred_element_type=F32)

    def rope(t):
        partner = jnp.where(first_half,
                            pltpu.roll(t, width - HEAD_DIM // 2, 1),
                            pltpu.roll(t, HEAD_DIM // 2, 1))
        return t * cos + partner * sin

    qa_ref[...] = rope(proj(0)).astype(qa_ref.dtype)
    ka_ref[...] = rope(proj(1)).astype(ka_ref.dtype)
    va_ref[...] = proj(2).astype(va_ref.dtype)
    ga_ref[...] = proj(3).astype(ga_ref.dtype)
    def store_pairs(ref, val):
        for p in range(width // LANES):
            ref[p] = val[:, p * LANES:(p + 1) * LANES].astype(ref.dtype)

    store_pairs(qb_ref, rope(proj(4)))
    kb = rope(proj(5))
    vb = proj(6)
    pair_lane = lane % LANES
    even_data = pair_lane < HEAD_DIM
    row = lax.broadcasted_iota(jnp.int32, (tm, width), 0)
    block = ((pl.program_id(0) * tm + row) // MOBA_BLOCK) % n_blocks
    onehot_e = (pair_lane - HEAD_DIM == block).astype(F32)
    onehot_o = (pair_lane == block).astype(F32)
    def store_pairs_transposed(ref, val):
        for p in range(width // LANES):
            ref[p] = val[:, p * LANES:(p + 1) * LANES].T.astype(ref.dtype)

    store_pairs_transposed(kbe_ref, jnp.where(even_data, kb, onehot_e))
    store_pairs_transposed(kbo_ref, jnp.where(even_data, onehot_o, kb))
    store_pairs(vbe_ref, jnp.where(even_data, vb, 1.0))
    store_pairs(vbo_ref, jnp.where(even_data, 1.0, vb))
    for blk in range(tm // MOBA_BLOCK):
        km_ref[blk] = jnp.mean(kb[blk * MOBA_BLOCK:(blk + 1) * MOBA_BLOCK], axis=0, keepdims=True)


def _in_proj(x2d, w_bf16, cos_t, sin_t, n_blocks, tm=512):
    t, d = x2d.shape
    n = w_bf16.shape[1]
    width = RET_WIDTH
    assert n_blocks <= HEAD_DIM and tm % MOBA_BLOCK == 0
    row = lambda i: (i, 0)
    pairs = width // LANES
    row_dtypes = [BF16, F32, BF16, F32]
    pair_spec = pl.BlockSpec((pairs, tm, LANES), lambda i: (0, i, 0))
    pair_t_spec = pl.BlockSpec((pairs, LANES, tm), lambda i: (0, 0, i))
    pair_outs = [(F32, False), (BF16, True), (BF16, True), (BF16, False), (BF16, False)]
    return pl.pallas_call(
        functools.partial(_in_proj_kernel, n_blocks=n_blocks),
        grid=(t // tm,),
        in_specs=[pl.BlockSpec((tm, d), row),
                  pl.BlockSpec((d, n), lambda i: (0, 0)),
                  pl.BlockSpec((tm, LANES), row),
                  pl.BlockSpec((tm, LANES), row)],
        out_specs=[pl.BlockSpec((tm, width), row)] * len(row_dtypes)
        + [pair_t_spec if transposed else pair_spec for _, transposed in pair_outs]
        + [pl.BlockSpec((tm // MOBA_BLOCK, 1, width), lambda i: (i, 0, 0))],
        out_shape=[jax.ShapeDtypeStruct((t, width), dt) for dt in row_dtypes]
        + [jax.ShapeDtypeStruct((pairs, LANES, t) if transposed else (pairs, t, LANES), dt)
           for dt, transposed in pair_outs]
        + [jax.ShapeDtypeStruct((t // MOBA_BLOCK, 1, width), F32)],
        compiler_params=_params("parallel"),
        name="in_proj",
    )(x2d, w_bf16, cos_t, sin_t)


def _retention_kernel(lg_ref, q_ref, k_ref, v_ref, g_ref, gain_ref, o_ref, state_ref, decay_ref):
    c = RET_CHUNK
    tm, width = q_ref.shape
    lane = lax.broadcasted_iota(jnp.int32, (c, LANES), 1)
    row = lax.broadcasted_iota(jnp.int32, (c, LANES), 0)
    head0 = lane < HEAD_DIM
    same_head = (row < HEAD_DIM) == head0
    nt = (((1,), (1,)), ((), ()))
    tn = (((0,), (0,)), ((), ()))

    @pl.when(pl.program_id(1) == 0)
    def _():
        state_ref[...] = jnp.zeros_like(state_ref)
        idx = row.astype(F32)
        diff = (row - lane).astype(F32)
        dpos = jnp.maximum(diff, 0.0)
        for hp in range(width // LANES):
            lg0 = lg_ref[2 * hp]
            lg1 = lg_ref[2 * hp + 1]
            lg_lane = jnp.where(head0, lg0, lg1)
            decay_ref[hp, 0] = jnp.exp(lg_lane * (idx + 1.0))
            decay_ref[hp, 1] = jnp.exp(lg_lane * (c - 1.0 - idx))
            decay_ref[hp, 2] = jnp.exp(lg_lane * float(c))
            decay_ref[hp, 3] = jnp.where(diff >= 0, jnp.exp(lg0 * dpos), 0.0)
            decay_ref[hp, 4] = jnp.where(diff >= 0, jnp.exp(lg1 * dpos), 0.0)

    for hp in range(width // LANES):
        cols = slice(hp * LANES, (hp + 1) * LANES)
        xi = decay_ref[hp, 0]
        zeta = decay_ref[hp, 1]
        chunk_decay = decay_ref[hp, 2]
        decay0 = decay_ref[hp, 3]
        decay1 = decay_ref[hp, 4]
        gain = gain_ref[:, cols]
        for ci in range(tm // c):
            sl = slice(ci * c, (ci + 1) * c)
            q = q_ref[sl, cols]
            kf = k_ref[sl, cols] * (HEAD_DIM ** -0.5)
            v = v_ref[sl, cols]
            kb = kf.astype(BF16)
            zero = jnp.zeros_like(q)
            q0 = jnp.where(head0, q, zero)
            q1 = jnp.where(head0, zero, q)
            s0 = lax.dot_general(q0, kb, nt, preferred_element_type=F32) * decay0
            s1 = lax.dot_general(q1, kb, nt, preferred_element_type=F32) * decay1
            inner = jnp.where(head0,
                              jnp.dot(s0.astype(BF16), v, preferred_element_type=F32),
                              jnp.dot(s1.astype(BF16), v, preferred_element_type=F32))
            state = state_ref[hp]
            cross = jnp.dot(q, state.astype(BF16), preferred_element_type=F32) * xi
            kv = lax.dot_general((kf * zeta).astype(BF16), v, tn, preferred_element_type=F32)
            state_ref[hp] = state * chunk_decay + jnp.where(same_head, kv, 0.0)

            y = inner + cross
            inv = 1.0 / HEAD_DIM
            mu = jnp.where(head0,
                           jnp.sum(jnp.where(head0, y, 0.0), axis=-1, keepdims=True),
                           jnp.sum(jnp.where(head0, 0.0, y), axis=-1, keepdims=True)) * inv
            d = y - mu
            dd = d * d
            var = jnp.where(head0,
                            jnp.sum(jnp.where(head0, dd, 0.0), axis=-1, keepdims=True),
                            jnp.sum(jnp.where(head0, 0.0, dd), axis=-1, keepdims=True)) * inv
            yn = d * lax.rsqrt(var + LN_EPS) * gain
            o_ref[sl, cols] = (yn * jax.nn.silu(g_ref[sl, cols])).astype(o_ref.dtype)


def _retention(qa, ka, va, ga, gn_gain, log_gamma, batch, tm=512):
    t, width = qa.shape
    s = t // batch
    nt = s // tm
    pairs = width // LANES
    blk = lambda b, i: (b * nt + i, 0)
    return pl.pallas_call(
        _retention_kernel,
        grid=(batch, nt),
        in_specs=[pl.BlockSpec(memory_space=pltpu.SMEM)]
        + [pl.BlockSpec((tm, width), blk)] * 4
        + [pl.BlockSpec((1, width), lambda b, i: (0, 0))],
        out_specs=pl.BlockSpec((tm, width), blk),
        out_shape=jax.ShapeDtypeStruct((t, width), BF16),
        scratch_shapes=[pltpu.VMEM((pairs, LANES, LANES), F32),
                        pltpu.VMEM((pairs, 5, RET_CHUNK, LANES), F32)],
        compiler_params=_params("parallel", "arbitrary"),
        name="retention",
    )(log_gamma, qa, ka, va, ga, gn_gain.reshape(1, width))


def _moba_select_kernel(q_ref, km_ref, qe_ref, qo_ref):
    ts = q_ref.shape[0]
    nb = km_ref.shape[0]
    qf = q_ref[...]
    km = km_ref[...]
    lane = lax.broadcasted_iota(jnp.int32, (ts, LANES), 1)
    even_data = lane < HEAD_DIM
    nt = (((1,), (1,)), ((), ()))
    blk = lax.broadcasted_iota(jnp.int32, (nb, ts), 0).astype(F32)
    col = lax.broadcasted_iota(jnp.int32, (nb, ts), 1)
    own = ((pl.program_id(2) * ts + col) // MOBA_BLOCK).astype(F32)
    past = blk < own
    place_row = lax.broadcasted_iota(jnp.int32, (nb, LANES), 0)
    place_lane = lax.broadcasted_iota(jnp.int32, (nb, LANES), 1)

    for h, out_ref in enumerate((qe_ref, qo_ref)):
        data = even_data if h == 0 else jnp.logical_not(even_data)
        qh = jnp.where(data, qf, 0.0)
        gate = lax.dot_general(km, qh, nt, preferred_element_type=F32,
                               precision=lax.Precision.HIGHEST)
        g = jnp.where(past, gate, NEG_INF)
        sel = jnp.zeros((nb, ts), F32)
        for _ in range(min(MOBA_TOPK, nb)):
            m = jnp.max(g, axis=0, keepdims=True)
            first = jnp.min(jnp.where(g == m, blk, float(nb)), axis=0, keepdims=True)
            pick = blk == first
            sel = jnp.where(pick, 1.0, sel)
            g = jnp.where(pick, NEG_INF, g)
        sel = jnp.where(past, sel, 0.0)
        sel = jnp.where(blk == own, 1.0, sel)
        spare_base = HEAD_DIM if h == 0 else 0
        place = (place_lane == place_row + spare_base).astype(BF16)
        placed = lax.dot_general(sel.astype(BF16), place, (((0,), (0,)), ((), ())),
                                 preferred_element_type=F32)
        bias = (1.0 - placed) * MASK_BIAS
        out_ref[...] = jnp.where(data, qh * (HEAD_DIM ** -0.5 * LOG2_E), bias).astype(out_ref.dtype)


def _moba_select(qb, kmean, batch, ts=2048):
    pairs, t, _ = qb.shape
    s = t // batch
    nb = s // MOBA_BLOCK
    ts = min(ts, s)
    assert nb <= HEAD_DIM and ts % MOBA_BLOCK == 0
    km = kmean.reshape(batch, nb, pairs * LANES)
    qblk = lambda b, h, i: (h, b * (s // ts) + i, 0)
    return pl.pallas_call(
        _moba_select_kernel,
        grid=(batch, pairs, s // ts),
        in_specs=[pl.BlockSpec((None, ts, LANES), qblk),
                  pl.BlockSpec((None, nb, LANES), lambda b, h, i: (b, 0, h))],
        out_specs=[pl.BlockSpec((None, ts, LANES), qblk)] * 2,
        out_shape=[jax.ShapeDtypeStruct((pairs, t, LANES), BF16)] * 2,
        compiler_params=_params("parallel", "parallel", "parallel"),
        name="moba_select",
    )(qb, km)


def _moba_kernel(qe_ref, qo_ref, ke_ref, ko_ref, ve_ref, vo_ref, o_ref, s_ref, m_ref, acc_ref):
    qi = pl.program_id(2)
    tq = qe_ref.shape[0]
    slab = MOBA_GROUP * MOBA_BLOCK
    nt = (((1,), (1,)), ((), ()))
    last = qi // MOBA_GROUP
    heads = ((qe_ref, ke_ref, ve_ref), (qo_ref, ko_ref, vo_ref))
    r = lax.broadcasted_iota(jnp.int32, (tq, slab), 0)
    c = lax.broadcasted_iota(jnp.int32, (tq, slab), 1)
    causal = c - r <= (qi - last * MOBA_GROUP) * MOBA_BLOCK

    def scores(groups, masked):
        for h, (q_ref, k_ref, _) in enumerate(heads):
            mx = None if masked else m_ref[h]
            for t in groups:
                off = t * slab
                sb = jnp.dot(q_ref[...], k_ref[:, off:off + slab],
                             preferred_element_type=F32)
                if masked:
                    sb = jnp.where(causal, sb, NEG_INF)
                s_ref[h, t] = sb
                for u in range(slab // LANES):
                    part = sb[:, u * LANES:(u + 1) * LANES]
                    mx = part if mx is None else jnp.maximum(mx, part)
            m_ref[h] = mx

    def weighted_values(groups, first):
        for h, (_, _, v_ref) in enumerate(heads):
            acc = None if first else acc_ref[h]
            for t in groups:
                off = t * slab
                p = jnp.exp2(s_ref[h, t] - row_max[h]).astype(BF16)
                pv = jnp.dot(p, v_ref[pl.ds(off, slab), :], preferred_element_type=F32)
                acc = pv if acc is None else acc + pv
            acc_ref[h] = acc

    row_max = [None, None]
    for n_past in range(s_ref.shape[1]):
        @pl.when(last == n_past)
        def _(n_past=n_past):
            scores((n_past,), True)
            if n_past:
                scores(tuple(range(n_past)), False)
            for h in range(2):
                row_max[h] = jnp.max(m_ref[h], axis=-1, keepdims=True)
            weighted_values(tuple(range(n_past + 1)), True)

    lane = lax.broadcasted_iota(jnp.int32, (tq, LANES), 1)
    a0 = acc_ref[0]
    a1 = acc_ref[1]
    o0 = a0 / pltpu.roll(a0, HEAD_DIM, 1)
    o1 = a1 / pltpu.roll(a1, HEAD_DIM, 1)
    o_ref[...] = jnp.where(lane < HEAD_DIM, o0, o1).astype(o_ref.dtype)


def _moba(qbe, qbo, kbe, kbo, vbe, vbo, batch):
    pairs, t, _ = qbe.shape
    s = t // batch
    tq = MOBA_BLOCK
    nq = s // tq
    nb = s // MOBA_BLOCK
    assert nb % MOBA_GROUP == 0
    qblk = lambda b, h, i: (h, b * nq + i, 0)
    kblk = lambda b, h, i: (h, 0, b)
    vblk = lambda b, h, i: (h, b, 0)
    return pl.pallas_call(
        _moba_kernel,
        grid=(batch, pairs, nq),
        in_specs=[pl.BlockSpec((None, tq, LANES), qblk)] * 2
        + [pl.BlockSpec((None, LANES, s), kblk)] * 2 + [pl.BlockSpec((None, s, LANES), vblk)] * 2,
        out_specs=pl.BlockSpec((None, tq, LANES), qblk),
        out_shape=jax.ShapeDtypeStruct((pairs, t, LANES), BF16),
        scratch_shapes=[pltpu.VMEM((2, nb // MOBA_GROUP, tq, MOBA_GROUP * MOBA_BLOCK), F32),
                        pltpu.VMEM((2, tq, LANES), F32),
                        pltpu.VMEM((2, tq, LANES), F32)],
        compiler_params=_params("parallel", "parallel", "arbitrary"),
        name="moba",
    )(qbe, qbo, kbe, kbo, vbe, vbo)


def _ffn_kernel(*refs, row_split, th, with_mixer):
    if with_mixer:
        x_ref, ya_ref, yb_ref, wo_ref, g0_ref, b0_ref, w1_ref, w2_ref, g_ref, b_ref, o_ref = refs
    else:
        x_ref, w1_ref, w2_ref, g_ref, b_ref, o_ref = refs
    tm = x_ref.shape[0]
    hdim = w1_ref.shape[1]
    rows = tm // row_split
    for part in range(row_split):
        sl = slice(part * rows, (part + 1) * rows)
        x = x_ref[sl, :]
        if with_mixer:
            ka = ya_ref.shape[1]
            yb = jnp.concatenate([yb_ref[p, sl, :] for p in range(yb_ref.shape[0])], axis=1)
            y0 = (jnp.dot(ya_ref[sl, :], wo_ref[:ka, :], preferred_element_type=F32)
                  + jnp.dot(yb, wo_ref[ka:, :], preferred_element_type=F32))
            x = _layer_norm_rows(DEEPNORM_ALPHA * x + y0, g0_ref[...], b0_ref[...])
        xb = x.astype(BF16)
        y = None
        for c in range(hdim // th):
            h = jnp.dot(xb, w1_ref[:, c * th:(c + 1) * th], preferred_element_type=F32)
            h = jnp.square(jnp.maximum(h, 0.0)).astype(BF16)
            yc = jnp.dot(h, w2_ref[c * th:(c + 1) * th, :], preferred_element_type=F32)
            y = yc if y is None else y + yc
        z = DEEPNORM_ALPHA * x + y
        o_ref[sl, :] = _layer_norm_rows(z, g_ref[...], b_ref[...])


def _ffn_ln(x2d, w1_bf16, w2_bf16, gain, bias, mixer=None, tm=1024, th=1024, row_split=2):
    t, d = x2d.shape
    hdim = w1_bf16.shape[1]
    row = lambda i: (i, 0)
    const = lambda i: (0, 0)
    resident = pl.Buffered(1)
    vec = pl.BlockSpec((1, d), const)
    operands = [x2d]
    in_specs = [pl.BlockSpec((tm, d), row)]
    if mixer is not None:
        ya, yb, w_out_bf16, gain0, bias0 = mixer
        operands += [ya, yb, w_out_bf16, gain0.reshape(1, d), bias0.reshape(1, d)]
        in_specs += [pl.BlockSpec((tm, ya.shape[1]), row),
                     pl.BlockSpec((yb.shape[0], tm, LANES), lambda i: (0, i, 0)),
                     pl.BlockSpec(w_out_bf16.shape, const, pipeline_mode=resident), vec, vec]
    operands += [w1_bf16, w2_bf16, gain.reshape(1, d), bias.reshape(1, d)]
    in_specs += [pl.BlockSpec((d, hdim), const, pipeline_mode=resident),
                 pl.BlockSpec((hdim, d), const, pipeline_mode=resident), vec, vec]
    return pl.pallas_call(
        functools.partial(_ffn_kernel, row_split=row_split, th=th, with_mixer=mixer is not None),
        grid=(t // tm,),
        in_specs=in_specs,
        out_specs=pl.BlockSpec((tm, d), row),
        out_shape=jax.ShapeDtypeStruct((t, d), F32),
        compiler_params=_params("parallel"),
        name="ffn_ln",
    )(*operands)


def _gmlp_kernel(x_ref, w_ref, b_ref, lng_ref, lnb_ref, ws_ref, bst_ref, wo_ref, g_ref, beta_ref,
                 o_ref, v_ref, gated_ref, *, row_split):
    tm = x_ref.shape[0]
    gw = v_ref.shape[1]
    c = GMLP_CHUNK
    cw = gw // GMLP_GROUPS
    part_rows = tm // row_split
    r = lax.broadcasted_iota(jnp.int32, (c, c), 0)
    cc = lax.broadcasted_iota(jnp.int32, (c, c), 1)
    causal = cc <= r
    bst = bst_ref[...]
    ws = [jnp.where(causal, ws_ref[g], 0.0).astype(BF16) for g in range(GMLP_GROUPS)]
    for part in range(row_split):
        base = part * part_rows
        prow = slice(base, base + part_rows)
        x = x_ref[prow, :]
        xb = x.astype(BF16)
        v = jnp.dot(xb, w_ref[:, gw:], preferred_element_type=F32) + b_ref[:, gw:]
        v_ref[prow, :] = _layer_norm_rows(jax.nn.gelu(v), lng_ref[...], lnb_ref[...]).astype(BF16)
        for g in range(GMLP_GROUPS):
            cols = slice(g * cw, (g + 1) * cw)
            u = jax.nn.gelu(jnp.dot(xb, w_ref[:, cols], preferred_element_type=F32) + b_ref[:, cols])
            for ci in range(part_rows // c):
                rows = slice(base + ci * c, base + (ci + 1) * c)
                sv = jnp.dot(ws[g], v_ref[rows, cols], preferred_element_type=F32) + bst[:, g:g + 1]
                gated_ref[rows, cols] = (u[ci * c:(ci + 1) * c] * sv).astype(BF16)
        y = jnp.dot(gated_ref[prow, :], wo_ref[...], preferred_element_type=F32)
        z = DEEPNORM_ALPHA * x + y
        o_ref[prow, :] = _layer_norm_rows(z, g_ref[...], beta_ref[...])


def _gmlp_mixer_ln(x2d, w_uv_bf16, b_uv, ln_g, ln_b, w_s, b_s, w_out_bf16, gain, bias,
                   tm=1024, row_split=2):
    t, d = x2d.shape
    n = w_uv_bf16.shape[1]
    gw = n // 2
    row = lambda i: (i, 0)
    const = lambda i: (0, 0)
    resident = pl.Buffered(1)
    return pl.pallas_call(
        functools.partial(_gmlp_kernel, row_split=row_split),
        grid=(t // tm,),
        in_specs=[pl.BlockSpec((tm, d), row),
                  pl.BlockSpec((d, n), const, pipeline_mode=resident),
                  pl.BlockSpec((1, n), const),
                  pl.BlockSpec((1, gw), const),
                  pl.BlockSpec((1, gw), const),
                  pl.BlockSpec(w_s.shape, lambda i: (0, 0, 0)),
                  pl.BlockSpec((GMLP_CHUNK, GMLP_GROUPS), const),
                  pl.BlockSpec((gw, d), const, pipeline_mode=resident),
                  pl.BlockSpec((1, d), const),
                  pl.BlockSpec((1, d), const)],
        out_specs=pl.BlockSpec((tm, d), row),
        out_shape=jax.ShapeDtypeStruct((t, d), F32),
        scratch_shapes=[pltpu.VMEM((tm, gw), BF16), pltpu.VMEM((tm, gw), BF16)],
        compiler_params=_params("parallel"),
        name="gmlp_mixer_ln",
    )(x2d, w_uv_bf16, b_uv.reshape(1, n), ln_g.reshape(1, gw), ln_b.reshape(1, gw), w_s, b_s.T,
      w_out_bf16, gain.reshape(1, d), bias.reshape(1, d))


def kernel(x, positions, ln_gain, ln_bias, mix_w_in, ret_gn_gain, mix_w_out,
           gmlp_w_uv, gmlp_b_uv, gmlp_ln_gain, gmlp_ln_bias, gmlp_w_s, gmlp_b_s,
           gmlp_w_out, ffn_w_in, ffn_w_out):
    batch, seq, d = x.shape
    depth = ln_gain.shape[0]
    h = x.reshape(batch * seq, d)
    cos_t, sin_t = _rope_tables(positions)
    log_gamma = jnp.log1p(-jnp.exp2(-5.0 - jnp.arange(N_RET_HEADS, dtype=F32)))
    for layer in range(depth):
        i = layer // 2
        if layer % 2 == 0:
            qa, ka, va, ga, qb, kbe, kbo, vbe, vbo, kmean = _in_proj(
                h, mix_w_in[i].astype(BF16), cos_t, sin_t, seq // MOBA_BLOCK)
            ya = _retention(qa, ka, va, ga, ret_gn_gain[i], log_gamma, batch)
            qbe, qbo = _moba_select(qb, kmean, batch)
            yb = _moba(qbe, qbo, kbe, kbo, vbe, vbo, batch)
            mixer = (ya, yb, mix_w_out[i].astype(BF16), ln_gain[layer, 0], ln_bias[layer, 0])
        else:
            h = _gmlp_mixer_ln(h, gmlp_w_uv[i].astype(BF16), gmlp_b_uv[i], gmlp_ln_gain[i],
                               gmlp_ln_bias[i], gmlp_w_s[i], gmlp_b_s[i],
                               gmlp_w_out[i].astype(BF16), ln_gain[layer, 0], ln_bias[layer, 0])
            mixer = None
        h = _ffn_ln(h, ffn_w_in[layer].astype(BF16), ffn_w_out[layer].astype(BF16),
                    ln_gain[layer, 1], ln_bias[layer, 1], mixer)
    return h.reshape(batch, seq, d)
```

```python
import functools

import jax
import jax.numpy as jnp
from jax import lax
from jax.experimental import pallas as pl
from jax.experimental.pallas import tpu as pltpu

HEAD_DIM = 64
N_RET_HEADS = 8
RET_WIDTH = N_RET_HEADS * HEAD_DIM
RET_CHUNK = 128
MOBA_BLOCK = 256
MOBA_TOPK = 3
GMLP_CHUNK = 128
GMLP_GROUPS = 8
ROPE_THETA = 10000.0
LN_EPS = 1e-5
DEPTH = 4
DEEPNORM_ALPHA = (2 * DEPTH) ** 0.25

LANES = 128
VMEM_LIMIT_BYTES = 52 * 1024 * 1024

BF16 = jnp.bfloat16
F32 = jnp.float32
NEG_INF = float("-inf")
MOBA_GROUP = 4
MASK_BIAS = -1e30
LOG2_E = 1.4426950408889634


def _params(*sem):
    return pltpu.CompilerParams(dimension_semantics=sem, vmem_limit_bytes=VMEM_LIMIT_BYTES)


def _layer_norm_rows(z, g, b):
    mu = jnp.mean(z, axis=-1, keepdims=True)
    d = z - mu
    var = jnp.mean(d * d, axis=-1, keepdims=True)
    return d * lax.rsqrt(var + LN_EPS) * g + b


def _rope_table_kernel(pos_ref, freq_ref, cos_ref, sin_ref):
    ang = pos_ref[...] * freq_ref[...]
    lane = lax.broadcasted_iota(jnp.int32, ang.shape, 1)
    first_half = (lane % HEAD_DIM) < (HEAD_DIM // 2)
    cos_ref[...] = jnp.cos(ang)
    sin_ref[...] = jnp.where(first_half, -jnp.sin(ang), jnp.sin(ang))


def _rope_tables(positions):
    t = positions.size
    tm = min(t, 2048)
    pos = positions.reshape(t, 1).astype(F32)
    inv_freq = ROPE_THETA ** (-jnp.arange(0, HEAD_DIM, 2, dtype=F32) / HEAD_DIM)
    freq = jnp.tile(inv_freq, LANES // (HEAD_DIM // 2)).reshape(1, LANES)
    return pl.pallas_call(
        _rope_table_kernel,
        grid=(t // tm,),
        in_specs=[pl.BlockSpec((tm, 1), lambda i: (i, 0)),
                  pl.BlockSpec((1, LANES), lambda i: (0, 0))],
        out_specs=[pl.BlockSpec((tm, LANES), lambda i: (i, 0))] * 2,
        out_shape=[jax.ShapeDtypeStruct((t, LANES), F32)] * 2,
        compiler_params=_params("parallel"),
        name="rope_tables",
    )(pos, freq)


def _in_proj_kernel(x_ref, w_ref, cos_ref, sin_ref,
                    qa_ref, ka_ref, va_ref, ga_ref, qb_ref, kbe_ref, kbo_ref, vbe_ref, vbo_ref,
                    km_ref, *, n_blocks):
    tm = x_ref.shape[0]
    width = RET_WIDTH
    pairs = width // LANES
    xb = x_ref[...].astype(BF16)
    cos = jnp.concatenate([cos_ref[...]] * pairs, axis=1)
    sin = jnp.concatenate([sin_ref[...]] * pairs, axis=1)
    lane = lax.broadcasted_iota(jnp.int32, (tm, width), 1)
    first_half = (lane % HEAD_DIM) < (HEAD_DIM // 2)

    def proj(g):
        return jnp.dot(xb, w_ref[:, g * width:(g + 1) * width], preferred_element_type=F32)

    def rope(t):
        partner = jnp.where(first_half,
                            pltpu.roll(t, width - HEAD_DIM // 2, 1),
                            pltpu.roll(t, HEAD_DIM // 2, 1))
        return t * cos + partner * sin

    def store_pairs(ref, val):
        for p in range(pairs):
            ref[p] = val[:, p * LANES:(p + 1) * LANES].astype(ref.dtype)

    qa_ref[...] = rope(proj(0)).astype(qa_ref.dtype)
    ka_ref[...] = rope(proj(1)).astype(ka_ref.dtype)
    va_ref[...] = proj(2).astype(va_ref.dtype)
    ga_ref[...] = proj(3).astype(ga_ref.dtype)
    store_pairs(qb_ref, rope(proj(4)))
    kb = rope(proj(5))
    vb = proj(6)
    pair_lane = lane % LANES
    even_data = pair_lane < HEAD_DIM
    row = lax.broadcasted_iota(jnp.int32, (tm, width), 0)
    block = ((pl.program_id(0) * tm + row) // MOBA_BLOCK) % n_blocks
    onehot_e = (pair_lane - HEAD_DIM == block).astype(F32)
    onehot_o = (pair_lane == block).astype(F32)
    store_pairs(kbe_ref, jnp.where(even_data, kb, onehot_e))
    store_pairs(kbo_ref, jnp.where(even_data, onehot_o, kb))
    store_pairs(vbe_ref, jnp.where(even_data, vb, 1.0))
    store_pairs(vbo_ref, jnp.where(even_data, 1.0, vb))
    for blk in range(tm // MOBA_BLOCK):
        km_ref[blk] = jnp.mean(kb[blk * MOBA_BLOCK:(blk + 1) * MOBA_BLOCK], axis=0, keepdims=True)


def _in_proj(x2d, w_bf16, cos_t, sin_t, n_blocks, tm=512):
    t, d = x2d.shape
    n = w_bf16.shape[1]
    width = RET_WIDTH
    assert n_blocks <= HEAD_DIM and tm % MOBA_BLOCK == 0
    row = lambda i: (i, 0)
    pairs = width // LANES
    row_dtypes = [BF16, F32, BF16, F32]
    pair_dtypes = [F32, BF16, BF16, BF16, BF16]
    return pl.pallas_call(
        functools.partial(_in_proj_kernel, n_blocks=n_blocks),
        grid=(t // tm,),
        in_specs=[pl.BlockSpec((tm, d), row),
                  pl.BlockSpec((d, n), lambda i: (0, 0)),
                  pl.BlockSpec((tm, LANES), row),
                  pl.BlockSpec((tm, LANES), row)],
        out_specs=[pl.BlockSpec((tm, width), row)] * len(row_dtypes)
        + [pl.BlockSpec((pairs, tm, LANES), lambda i: (0, i, 0))] * len(pair_dtypes)
        + [pl.BlockSpec((tm // MOBA_BLOCK, 1, width), lambda i: (i, 0, 0))],
        out_shape=[jax.ShapeDtypeStruct((t, width), dt) for dt in row_dtypes]
        + [jax.ShapeDtypeStruct((pairs, t, LANES), dt) for dt in pair_dtypes]
        + [jax.ShapeDtypeStruct((t // MOBA_BLOCK, 1, width), F32)],
        compiler_params=_params("parallel"),
        name="in_proj",
    )(x2d, w_bf16, cos_t, sin_t)


def _retention_kernel(lg_ref, q_ref, k_ref, v_ref, g_ref, gain_ref, o_ref, state_ref, decay_ref):
    c = RET_CHUNK
    tm, width = q_ref.shape
    lane = lax.broadcasted_iota(jnp.int32, (c, LANES), 1)
    row = lax.broadcasted_iota(jnp.int32, (c, LANES), 0)
    head0 = lane < HEAD_DIM
    same_head = (row < HEAD_DIM) == head0
    nt = (((1,), (1,)), ((), ()))
    tn = (((0,), (0,)), ((), ()))

    @pl.when(pl.program_id(1) == 0)
    def _():
        state_ref[...] = jnp.zeros_like(state_ref)
        idx = row.astype(F32)
        diff = (row - lane).astype(F32)
        dpos = jnp.maximum(diff, 0.0)
        for hp in range(width // LANES):
            lg0 = lg_ref[2 * hp]
            lg1 = lg_ref[2 * hp + 1]
            lg_lane = jnp.where(head0, lg0, lg1)
            decay_ref[hp, 0] = jnp.exp(lg_lane * (idx + 1.0))
            decay_ref[hp, 1] = jnp.exp(lg_lane * (c - 1.0 - idx))
            decay_ref[hp, 2] = jnp.exp(lg_lane * float(c))
            decay_ref[hp, 3] = jnp.where(diff >= 0, jnp.exp(lg0 * dpos), 0.0)
            decay_ref[hp, 4] = jnp.where(diff >= 0, jnp.exp(lg1 * dpos), 0.0)

    for hp in range(width // LANES):
        cols = slice(hp * LANES, (hp + 1) * LANES)
        xi = decay_ref[hp, 0]
        zeta = decay_ref[hp, 1]
        chunk_decay = decay_ref[hp, 2]
        decay0 = decay_ref[hp, 3]
        decay1 = decay_ref[hp, 4]
        gain = gain_ref[:, cols]
        for ci in range(tm // c):
            sl = slice(ci * c, (ci + 1) * c)
            q = q_ref[sl, cols]
            kf = k_ref[sl, cols] * (HEAD_DIM ** -0.5)
            v = v_ref[sl, cols]
            kb = kf.astype(BF16)
            zero = jnp.zeros_like(q)
            q0 = jnp.where(head0, q, zero)
            q1 = jnp.where(head0, zero, q)
            s0 = lax.dot_general(q0, kb, nt, preferred_element_type=F32) * decay0
            s1 = lax.dot_general(q1, kb, nt, preferred_element_type=F32) * decay1
            inner = jnp.where(head0,
                              jnp.dot(s0.astype(BF16), v, preferred_element_type=F32),
                              jnp.dot(s1.astype(BF16), v, preferred_element_type=F32))
            state = state_ref[hp]
            cross = jnp.dot(q, state.astype(BF16), preferred_element_type=F32) * xi
            kv = lax.dot_general((kf * zeta).astype(BF16), v, tn, preferred_element_type=F32)
            state_ref[hp] = state * chunk_decay + jnp.where(same_head, kv, 0.0)

            y = inner + cross
            inv = 1.0 / HEAD_DIM
            mu = jnp.where(head0,
                           jnp.sum(jnp.where(head0, y, 0.0), axis=-1, keepdims=True),
                           jnp.sum(jnp.where(head0, 0.0, y), axis=-1, keepdims=True)) * inv
            d = y - mu
            dd = d * d
            var = jnp.where(head0,
                            jnp.sum(jnp.where(head0, dd, 0.0), axis=-1, keepdims=True),
                            jnp.sum(jnp.where(head0, 0.0, dd), axis=-1, keepdims=True)) * inv
            yn = d * lax.rsqrt(var + LN_EPS) * gain
            o_ref[sl, cols] = (yn * jax.nn.silu(g_ref[sl, cols])).astype(o_ref.dtype)


def _retention(qa, ka, va, ga, gn_gain, log_gamma, batch, tm=1024):
    t, width = qa.shape
    s = t // batch
    tm = min(tm, s)
    nt = s // tm
    pairs = width // LANES
    blk = lambda b, i: (b * nt + i, 0)
    return pl.pallas_call(
        _retention_kernel,
        grid=(batch, nt),
        in_specs=[pl.BlockSpec(memory_space=pltpu.SMEM)]
        + [pl.BlockSpec((tm, width), blk)] * 4
        + [pl.BlockSpec((1, width), lambda b, i: (0, 0))],
        out_specs=pl.BlockSpec((tm, width), blk),
        out_shape=jax.ShapeDtypeStruct((t, width), BF16),
        scratch_shapes=[pltpu.VMEM((pairs, LANES, LANES), F32),
                        pltpu.VMEM((pairs, 5, RET_CHUNK, LANES), F32)],
        compiler_params=_params("parallel", "arbitrary"),
        name="retention",
    )(log_gamma, qa, ka, va, ga, gn_gain.reshape(1, width))


def _moba_select_kernel(q_ref, km_ref, qe_ref, qo_ref):
    ts = q_ref.shape[0]
    nb = km_ref.shape[0]
    qf = q_ref[...]
    km = km_ref[...]
    lane = lax.broadcasted_iota(jnp.int32, (ts, LANES), 1)
    even_data = lane < HEAD_DIM
    nt = (((1,), (1,)), ((), ()))
    blk = lax.broadcasted_iota(jnp.int32, (nb, ts), 0).astype(F32)
    col = lax.broadcasted_iota(jnp.int32, (nb, ts), 1)
    own = ((pl.program_id(2) * ts + col) // MOBA_BLOCK).astype(F32)
    past = blk < own
    place_row = lax.broadcasted_iota(jnp.int32, (nb, LANES), 0)
    place_lane = lax.broadcasted_iota(jnp.int32, (nb, LANES), 1)

    for h, out_ref in enumerate((qe_ref, qo_ref)):
        data = even_data if h == 0 else jnp.logical_not(even_data)
        qh = jnp.where(data, qf, 0.0)
        gate = lax.dot_general(km, qh, nt, preferred_element_type=F32,
                               precision=lax.Precision.HIGHEST)
        g = jnp.where(past, gate, NEG_INF)
        sel = jnp.zeros((nb, ts), F32)
        for _ in range(min(MOBA_TOPK, nb)):
            m = jnp.max(g, axis=0, keepdims=True)
            first = jnp.min(jnp.where(g == m, blk, float(nb)), axis=0, keepdims=True)
            pick = blk == first
            sel = jnp.where(pick, 1.0, sel)
            g = jnp.where(pick, NEG_INF, g)
        sel = jnp.where(past, sel, 0.0)
        sel = jnp.where(blk == own, 1.0, sel)
        spare_base = HEAD_DIM if h == 0 else 0
        place = (place_lane == place_row + spare_base).astype(BF16)
        placed = lax.dot_general(sel.astype(BF16), place, (((0,), (0,)), ((), ())),
                                 preferred_element_type=F32)
        bias = (1.0 - placed) * MASK_BIAS
        out_ref[...] = jnp.where(data, qh * (HEAD_DIM ** -0.5 * LOG2_E), bias).astype(out_ref.dtype)


def _moba_select(qb, kmean, batch, ts=2048):
    pairs, t, _ = qb.shape
    s = t // batch
    nb = s // MOBA_BLOCK
    ts = min(ts, s)
    assert nb <= HEAD_DIM and ts % MOBA_BLOCK == 0
    km = kmean.reshape(batch, nb, pairs * LANES)
    qblk = lambda b, h, i: (h, b * (s // ts) + i, 0)
    return pl.pallas_call(
        _moba_select_kernel,
        grid=(batch, pairs, s // ts),
        in_specs=[pl.BlockSpec((None, ts, LANES), qblk),
                  pl.BlockSpec((None, nb, LANES), lambda b, h, i: (b, 0, h))],
        out_specs=[pl.BlockSpec((None, ts, LANES), qblk)] * 2,
        out_shape=[jax.ShapeDtypeStruct((pairs, t, LANES), BF16)] * 2,
        compiler_params=_params("parallel", "parallel", "parallel"),
        name="moba_select",
    )(qb, km)


def _moba_kernel(qe_ref, qo_ref, ke_ref, ko_ref, ve_ref, vo_ref, o_ref, s_ref, m_ref, acc_ref):
    qi = pl.program_id(2)
    tq = qe_ref.shape[0]
    slab = MOBA_GROUP * MOBA_BLOCK
    nt = (((1,), (1,)), ((), ()))
    last = qi // MOBA_GROUP
    heads = ((qe_ref, ke_ref, ve_ref), (qo_ref, ko_ref, vo_ref))
    r = lax.broadcasted_iota(jnp.int32, (tq, slab), 0)
    c = lax.broadcasted_iota(jnp.int32, (tq, slab), 1)
    causal = c - r <= (qi - last * MOBA_GROUP) * MOBA_BLOCK

    def scores(groups, masked):
        for h, (q_ref, k_ref, _) in enumerate(heads):
            mx = None if masked else m_ref[h]
            for t in groups:
                off = t * slab
                sb = lax.dot_general(q_ref[...], k_ref[off:off + slab, :], nt,
                                     preferred_element_type=F32)
                if masked:
                    sb = jnp.where(causal, sb, NEG_INF)
                s_ref[h, t] = sb
                for u in range(slab // LANES):
                    part = sb[:, u * LANES:(u + 1) * LANES]
                    mx = part if mx is None else jnp.maximum(mx, part)
            m_ref[h] = mx

    def weighted_values(groups, first):
        for h, (_, _, v_ref) in enumerate(heads):
            acc = None if first else acc_ref[h]
            for t in groups:
                off = t * slab
                p = jnp.exp2(s_ref[h, t] - row_max[h]).astype(BF16)
                pv = jnp.dot(p, v_ref[off:off + slab, :], preferred_element_type=F32)
                acc = pv if acc is None else acc + pv
            acc_ref[h] = acc

    row_max = [None, None]
    for n_past in range(s_ref.shape[1]):
        @pl.when(last == n_past)
        def _(n_past=n_past):
            scores((n_past,), True)
            if n_past:
                scores(tuple(range(n_past)), False)
            for h in range(2):
                row_max[h] = jnp.max(m_ref[h], axis=-1, keepdims=True)
            weighted_values(tuple(range(n_past + 1)), True)

    lane = lax.broadcasted_iota(jnp.int32, (tq, LANES), 1)
    a0 = acc_ref[0]
    a1 = acc_ref[1]
    o0 = a0 / pltpu.roll(a0, HEAD_DIM, 1)
    o1 = a1 / pltpu.roll(a1, HEAD_DIM, 1)
    o_ref[...] = jnp.where(lane < HEAD_DIM, o0, o1).astype(o_ref.dtype)


def _moba(qbe, qbo, kbe, kbo, vbe, vbo, batch):
    pairs, t, _ = qbe.shape
    s = t // batch
    tq = MOBA_BLOCK
    nq = s // tq
    nb = s // MOBA_BLOCK
    assert nb % MOBA_GROUP == 0
    qblk = lambda b, h, i: (h, b * nq + i, 0)
    kvblk = lambda b, h, i: (h, b, 0)
    return pl.pallas_call(
        _moba_kernel,
        grid=(batch, pairs, nq),
        in_specs=[pl.BlockSpec((None, tq, LANES), qblk)] * 2
        + [pl.BlockSpec((None, s, LANES), kvblk)] * 4,
        out_specs=pl.BlockSpec((None, tq, LANES), qblk),
        out_shape=jax.ShapeDtypeStruct((pairs, t, LANES), BF16),
        scratch_shapes=[pltpu.VMEM((2, nb // MOBA_GROUP, tq, MOBA_GROUP * MOBA_BLOCK), F32),
                        pltpu.VMEM((2, tq, LANES), F32),
                        pltpu.VMEM((2, tq, LANES), F32)],
        compiler_params=_params("parallel", "parallel", "arbitrary"),
        name="moba",
    )(qbe, qbo, kbe, kbo, vbe, vbo)


def _ffn_kernel(*refs, row_split, th, with_mixer):
    if with_mixer:
        x_ref, ya_ref, yb_ref, wo_ref, g0_ref, b0_ref, w1_ref, w2_ref, g_ref, b_ref, o_ref = refs
    else:
        x_ref, w1_ref, w2_ref, g_ref, b_ref, o_ref = refs
    tm = x_ref.shape[0]
    hdim = w1_ref.shape[1]
    rows = tm // row_split
    for part in range(row_split):
        sl = slice(part * rows, (part + 1) * rows)
        x = x_ref[sl, :]
        if with_mixer:
            ka = ya_ref.shape[1]
            yb = jnp.concatenate([yb_ref[p, sl, :] for p in range(yb_ref.shape[0])], axis=1)
            y0 = (jnp.dot(ya_ref[sl, :], wo_ref[:ka, :], preferred_element_type=F32)
                  + jnp.dot(yb, wo_ref[ka:, :], preferred_element_type=F32))
            x = _layer_norm_rows(DEEPNORM_ALPHA * x + y0, g0_ref[...], b0_ref[...])
        xb = x.astype(BF16)
        y = None
        for c in range(hdim // th):
            h = jnp.dot(xb, w1_ref[:, c * th:(c + 1) * th], preferred_element_type=F32)
            h = jnp.square(jnp.maximum(h, 0.0)).astype(BF16)
            yc = jnp.dot(h, w2_ref[c * th:(c + 1) * th, :], preferred_element_type=F32)
            y = yc if y is None else y + yc
        z = DEEPNORM_ALPHA * x + y
        o_ref[sl, :] = _layer_norm_rows(z, g_ref[...], b_ref[...])


def _ffn_ln(x2d, w1_bf16, w2_bf16, gain, bias, mixer=None, tm=1024, th=1024, row_split=2):
    t, d = x2d.shape
    hdim = w1_bf16.shape[1]
    row = lambda i: (i, 0)
    const = lambda i: (0, 0)
    resident = pl.Buffered(1)
    vec = pl.BlockSpec((1, d), const)
    operands = [x2d]
    in_specs = [pl.BlockSpec((tm, d), row)]
    if mixer is not None:
        ya, yb, w_out_bf16, gain0, bias0 = mixer
        operands += [ya, yb, w_out_bf16, gain0.reshape(1, d), bias0.reshape(1, d)]
        in_specs += [pl.BlockSpec((tm, ya.shape[1]), row),
                     pl.BlockSpec((yb.shape[0], tm, LANES), lambda i: (0, i, 0)),
                     pl.BlockSpec(w_out_bf16.shape, const, pipeline_mode=resident), vec, vec]
    operands += [w1_bf16, w2_bf16, gain.reshape(1, d), bias.reshape(1, d)]
    in_specs += [pl.BlockSpec((d, hdim), const, pipeline_mode=resident),
                 pl.BlockSpec((hdim, d), const, pipeline_mode=resident), vec, vec]
    return pl.pallas_call(
        functools.partial(_ffn_kernel, row_split=row_split, th=th, with_mixer=mixer is not None),
        grid=(t // tm,),
        in_specs=in_specs,
        out_specs=pl.BlockSpec((tm, d), row),
        out_shape=jax.ShapeDtypeStruct((t, d), F32),
        compiler_params=_params("parallel"),
        name="ffn_ln",
    )(*operands)


def _gmlp_kernel(x_ref, w_ref, b_ref, lng_ref, lnb_ref, ws_ref, bst_ref, wo_ref, g_ref, beta_ref,
                 o_ref, v_ref, gated_ref, *, row_split):
    tm = x_ref.shape[0]
    gw = v_ref.shape[1]
    c = GMLP_CHUNK
    cw = gw // GMLP_GROUPS
    part_rows = tm // row_split
    r = lax.broadcasted_iota(jnp.int32, (c, c), 0)
    cc = lax.broadcasted_iota(jnp.int32, (c, c), 1)
    causal = cc <= r
    bst = bst_ref[...]
    ws = [jnp.where(causal, ws_ref[g], 0.0).astype(BF16) for g in range(GMLP_GROUPS)]
    for part in range(row_split):
        base = part * part_rows
        prow = slice(base, base + part_rows)
        x = x_ref[prow, :]
        xb = x.astype(BF16)
        v = jnp.dot(xb, w_ref[:, gw:], preferred_element_type=F32) + b_ref[:, gw:]
        v_ref[prow, :] = _layer_norm_rows(jax.nn.gelu(v), lng_ref[...], lnb_ref[...]).astype(BF16)
        for g in range(GMLP_GROUPS):
            cols = slice(g * cw, (g + 1) * cw)
            u = jax.nn.gelu(jnp.dot(xb, w_ref[:, cols], preferred_element_type=F32) + b_ref[:, cols])
            for ci in range(part_rows // c):
                rows = slice(base + ci * c, base + (ci + 1) * c)
                sv = jnp.dot(ws[g], v_ref[rows, cols], preferred_element_type=F32) + bst[:, g:g + 1]
                gated_ref[rows, cols] = (u[ci * c:(ci + 1) * c] * sv).astype(BF16)
        y = jnp.dot(gated_ref[prow, :], wo_ref[...], preferred_element_type=F32)
        z = DEEPNORM_ALPHA * x + y
        o_ref[prow, :] = _layer_norm_rows(z, g_ref[...], beta_ref[...])


def _gmlp_mixer_ln(x2d, w_uv_bf16, b_uv, ln_g, ln_b, w_s, b_s, w_out_bf16, gain, bias,
                   tm=1024, row_split=2):
    t, d = x2d.shape
    n = w_uv_bf16.shape[1]
    gw = n // 2
    row = lambda i: (i, 0)
    const = lambda i: (0, 0)
    resident = pl.Buffered(1)
    return pl.pallas_call(
        functools.partial(_gmlp_kernel, row_split=row_split),
        grid=(t // tm,),
        in_specs=[pl.BlockSpec((tm, d), row),
                  pl.BlockSpec((d, n), const, pipeline_mode=resident),
                  pl.BlockSpec((1, n), const),
                  pl.BlockSpec((1, gw), const),
                  pl.BlockSpec((1, gw), const),
                  pl.BlockSpec(w_s.shape, lambda i: (0, 0, 0)),
                  pl.BlockSpec((GMLP_CHUNK, GMLP_GROUPS), const),
                  pl.BlockSpec((gw, d), const, pipeline_mode=resident),
                  pl.BlockSpec((1, d), const),
                  pl.BlockSpec((1, d), const)],
        out_specs=pl.BlockSpec((tm, d), row),
        out_shape=jax.ShapeDtypeStruct((t, d), F32),
        scratch_shapes=[pltpu.VMEM((tm, gw), BF16), pltpu.VMEM((tm, gw), BF16)],
        compiler_params=_params("parallel"),
        name="gmlp_mixer_ln",
    )(x2d, w_uv_bf16, b_uv.reshape(1, n), ln_g.reshape(1, gw), ln_b.reshape(1, gw), w_s, b_s.T,
      w_out_bf16, gain.reshape(1, d), bias.reshape(1, d))


def kernel(x, positions, ln_gain, ln_bias, mix_w_in, ret_gn_gain, mix_w_out,
           gmlp_w_uv, gmlp_b_uv, gmlp_ln_gain, gmlp_ln_bias, gmlp_w_s, gmlp_b_s,
           gmlp_w_out, ffn_w_in, ffn_w_out):
    batch, seq, d = x.shape
    depth = ln_gain.shape[0]
    h = x.reshape(batch * seq, d)
    cos_t, sin_t = _rope_tables(positions)
    log_gamma = jnp.log1p(-jnp.exp2(-5.0 - jnp.arange(N_RET_HEADS, dtype=F32)))
    for layer in range(depth):
        i = layer // 2
        if layer % 2 == 0:
            qa, ka, va, ga, qb, kbe, kbo, vbe, vbo, kmean = _in_proj(
                h, mix_w_in[i].astype(BF16), cos_t, sin_t, seq // MOBA_BLOCK)
            ya = _retention(qa, ka, va, ga, ret_gn_gain[i], log_gamma, batch)
            qbe, qbo = _moba_select(qb, kmean, batch)
            yb = _moba(qbe, qbo, kbe, kbo, vbe, vbo, batch)
            mixer = (ya, yb, mix_w_out[i].astype(BF16), ln_gain[layer, 0], ln_bias[layer, 0])
        else:
            h = _gmlp_mixer_ln(h, gmlp_w_uv[i].astype(BF16), gmlp_b_uv[i], gmlp_ln_gain[i],
                               gmlp_ln_bias[i], gmlp_w_s[i], gmlp_b_s[i],
                               gmlp_w_out[i].astype(BF16), ln_gain[layer, 0], ln_bias[layer, 0])
            mixer = None
        h = _ffn_ln(h, ffn_w_in[layer].astype(BF16), ffn_w_out[layer].astype(BF16),
                    ln_gain[layer, 1], ln_bias[layer, 1], mixer)
    return h.reshape(batch, seq, d)
```

```python
import functools

import jax
import jax.numpy as jnp
from jax import lax
from jax.experimental import pallas as pl
from jax.experimental.pallas import tpu as pltpu

HEAD_DIM = 64
N_RET_HEADS = 8
RET_WIDTH = N_RET_HEADS * HEAD_DIM
RET_CHUNK = 128
MOBA_BLOCK = 256
MOBA_TOPK = 3
GMLP_CHUNK = 128
GMLP_GROUPS = 8
ROPE_THETA = 10000.0
LN_EPS = 1e-5
DEPTH = 4
DEEPNORM_ALPHA = (2 * DEPTH) ** 0.25

LANES = 128
VMEM_LIMIT_BYTES = 52 * 1024 * 1024

BF16 = jnp.bfloat16
F32 = jnp.float32
NEG_INF = float("-inf")
MOBA_GROUP = 4
MASK_BIAS = -1e30
LOG2_E = 1.4426950408889634


def _params(*sem):
    return pltpu.CompilerParams(dimension_semantics=sem, vmem_limit_bytes=VMEM_LIMIT_BYTES)


def _layer_norm_rows(z, g, b):
    mu = jnp.mean(z, axis=-1, keepdims=True)
    d = z - mu
    var = jnp.mean(d * d, axis=-1, keepdims=True)
    return d * lax.rsqrt(var + LN_EPS) * g + b


def _rope_table_kernel(pos_ref, freq_ref, cos_ref, sin_ref):
    ang = pos_ref[...] * freq_ref[...]
    lane = lax.broadcasted_iota(jnp.int32, ang.shape, 1)
    first_half = (lane % HEAD_DIM) < (HEAD_DIM // 2)
    cos_ref[...] = jnp.cos(ang)
    sin_ref[...] = jnp.where(first_half, -jnp.sin(ang), jnp.sin(ang))


def _rope_tables(positions):
    t = positions.size
    tm = min(t, 2048)
    pos = positions.reshape(t, 1).astype(F32)
    inv_freq = ROPE_THETA ** (-jnp.arange(0, HEAD_DIM, 2, dtype=F32) / HEAD_DIM)
    freq = jnp.tile(inv_freq, LANES // (HEAD_DIM // 2)).reshape(1, LANES)
    return pl.pallas_call(
        _rope_table_kernel,
        grid=(t // tm,),
        in_specs=[pl.BlockSpec((tm, 1), lambda i: (i, 0)),
                  pl.BlockSpec((1, LANES), lambda i: (0, 0))],
        out_specs=[pl.BlockSpec((tm, LANES), lambda i: (i, 0))] * 2,
        out_shape=[jax.ShapeDtypeStruct((t, LANES), F32)] * 2,
        compiler_params=_params("parallel"),
        name="rope_tables",
    )(pos, freq)


def _in_proj_kernel(x_ref, w_ref, cos_ref, sin_ref,
                    qa_ref, ka_ref, va_ref, ga_ref, qb_ref, kbe_ref, kbo_ref, vbe_ref, vbo_ref,
                    km_ref, *, n_blocks):
    tm = x_ref.shape[0]
    width = RET_WIDTH
    pairs = width // LANES
    xb = x_ref[...].astype(BF16)
    cos = jnp.concatenate([cos_ref[...]] * pairs, axis=1)
    sin = jnp.concatenate([sin_ref[...]] * pairs, axis=1)
    lane = lax.broadcasted_iota(jnp.int32, (tm, width), 1)
    first_half = (lane % HEAD_DIM) < (HEAD_DIM // 2)

    def proj(g):
        return jnp.dot(xb, w_ref[:, g * width:(g + 1) * width], preferred_element_type=F32)

    def rope(t):
        partner = jnp.where(first_half,
                            pltpu.roll(t, width - HEAD_DIM // 2, 1),
                            pltpu.roll(t, HEAD_DIM // 2, 1))
        return t * cos + partner * sin

    def store_pairs(ref, val):
        for p in range(pairs):
            ref[p] = val[:, p * LANES:(p + 1) * LANES].astype(ref.dtype)

    qa_ref[...] = rope(proj(0)).astype(qa_ref.dtype)
    ka_ref[...] = rope(proj(1)).astype(ka_ref.dtype)
    va_ref[...] = proj(2).astype(va_ref.dtype)
    ga_ref[...] = proj(3).astype(ga_ref.dtype)
    store_pairs(qb_ref, rope(proj(4)))
    kb = rope(proj(5))
    vb = proj(6)
    pair_lane = lane % LANES
    even_data = pair_lane < HEAD_DIM
    row = lax.broadcasted_iota(jnp.int32, (tm, width), 0)
    block = ((pl.program_id(0) * tm + row) // MOBA_BLOCK) % n_blocks
    onehot_e = (pair_lane - HEAD_DIM == block).astype(F32)
    onehot_o = (pair_lane == block).astype(F32)
    store_pairs(kbe_ref, jnp.where(even_data, kb, onehot_e))
    store_pairs(kbo_ref, jnp.where(even_data, onehot_o, kb))
    store_pairs(vbe_ref, jnp.where(even_data, vb, 1.0))
    store_pairs(vbo_ref, jnp.where(even_data, 1.0, vb))
    for blk in range(tm // MOBA_BLOCK):
        km_ref[blk] = jnp.mean(kb[blk * MOBA_BLOCK:(blk + 1) * MOBA_BLOCK], axis=0, keepdims=True)


def _in_proj(x2d, w_bf16, cos_t, sin_t, n_blocks, tm=512):
    t, d = x2d.shape
    n = w_bf16.shape[1]
    width = RET_WIDTH
    assert n_blocks <= HEAD_DIM and tm % MOBA_BLOCK == 0
    row = lambda i: (i, 0)
    pairs = width // LANES
    row_dtypes = [BF16, F32, BF16, F32]
    pair_dtypes = [F32, BF16, BF16, BF16, BF16]
    return pl.pallas_call(
        functools.partial(_in_proj_kernel, n_blocks=n_blocks),
        grid=(t // tm,),
        in_specs=[pl.BlockSpec((tm, d), row),
                  pl.BlockSpec((d, n), lambda i: (0, 0)),
                  pl.BlockSpec((tm, LANES), row),
                  pl.BlockSpec((tm, LANES), row)],
        out_specs=[pl.BlockSpec((tm, width), row)] * len(row_dtypes)
        + [pl.BlockSpec((pairs, tm, LANES), lambda i: (0, i, 0))] * len(pair_dtypes)
        + [pl.BlockSpec((tm // MOBA_BLOCK, 1, width), lambda i: (i, 0, 0))],
        out_shape=[jax.ShapeDtypeStruct((t, width), dt) for dt in row_dtypes]
        + [jax.ShapeDtypeStruct((pairs, t, LANES), dt) for dt in pair_dtypes]
        + [jax.ShapeDtypeStruct((t // MOBA_BLOCK, 1, width), F32)],
        compiler_params=_params("parallel"),
        name="in_proj",
    )(x2d, w_bf16, cos_t, sin_t)


def _retention_kernel(lg_ref, q_ref, k_ref, v_ref, g_ref, gain_ref, o_ref, state_ref, decay_ref):
    c = RET_CHUNK
    tm, width = q_ref.shape
    lane = lax.broadcasted_iota(jnp.int32, (c, LANES), 1)
    row = lax.broadcasted_iota(jnp.int32, (c, LANES), 0)
    head0 = lane < HEAD_DIM
    same_head = (row < HEAD_DIM) == head0
    nt = (((1,), (1,)), ((), ()))
    tn = (((0,), (0,)), ((), ()))

    @pl.when(pl.program_id(1) == 0)
    def _():
        state_ref[...] = jnp.zeros_like(state_ref)
        idx = row.astype(F32)
        diff = (row - lane).astype(F32)
        dpos = jnp.maximum(diff, 0.0)
        for hp in range(width // LANES):
            lg0 = lg_ref[2 * hp]
            lg1 = lg_ref[2 * hp + 1]
            lg_lane = jnp.where(head0, lg0, lg1)
            decay_ref[hp, 0] = jnp.exp(lg_lane * (idx + 1.0))
            decay_ref[hp, 1] = jnp.exp(lg_lane * (c - 1.0 - idx))
            decay_ref[hp, 2] = jnp.exp(lg_lane * float(c))
            decay_ref[hp, 3] = jnp.where(diff >= 0, jnp.exp(lg0 * dpos), 0.0)
            decay_ref[hp, 4] = jnp.where(diff >= 0, jnp.exp(lg1 * dpos), 0.0)

    for hp in range(width // LANES):
        cols = slice(hp * LANES, (hp + 1) * LANES)
        xi = decay_ref[hp, 0]
        zeta = decay_ref[hp, 1]
        chunk_decay = decay_ref[hp, 2]
        decay0 = decay_ref[hp, 3]
        decay1 = decay_ref[hp, 4]
        gain = gain_ref[:, cols]
        for ci in range(tm // c):
            sl = slice(ci * c, (ci + 1) * c)
            q = q_ref[sl, cols]
            kf = k_ref[sl, cols] * (HEAD_DIM ** -0.5)
            v = v_ref[sl, cols]
            kb = kf.astype(BF16)
            zero = jnp.zeros_like(q)
            q0 = jnp.where(head0, q, zero)
            q1 = jnp.where(head0, zero, q)
            s0 = lax.dot_general(q0, kb, nt, preferred_element_type=F32) * decay0
            s1 = lax.dot_general(q1, kb, nt, preferred_element_type=F32) * decay1
            inner = jnp.where(head0,
                              jnp.dot(s0.astype(BF16), v, preferred_element_type=F32),
                              jnp.dot(s1.astype(BF16), v, preferred_element_type=F32))
            state = state_ref[hp]
            cross = jnp.dot(q, state.astype(BF16), preferred_element_type=F32) * xi
            kv = lax.dot_general((kf * zeta).astype(BF16), v, tn, preferred_element_type=F32)
            state_ref[hp] = state * chunk_decay + jnp.where(same_head, kv, 0.0)

            y = inner + cross
            inv = 1.0 / HEAD_DIM
            mu = jnp.where(head0,
                           jnp.sum(jnp.where(head0, y, 0.0), axis=-1, keepdims=True),
                           jnp.sum(jnp.where(head0, 0.0, y), axis=-1, keepdims=True)) * inv
            d = y - mu
            dd = d * d
            var = jnp.where(head0,
                            jnp.sum(jnp.where(head0, dd, 0.0), axis=-1, keepdims=True),
                            jnp.sum(jnp.where(head0, 0.0, dd), axis=-1, keepdims=True)) * inv
            yn = d * lax.rsqrt(var + LN_EPS) * gain
            o_ref[sl, cols] = (yn * jax.nn.silu(g_ref[sl, cols])).astype(o_ref.dtype)


def _retention(qa, ka, va, ga, gn_gain, log_gamma, batch, tm=2048):
    t, width = qa.shape
    s = t // batch
    tm = min(tm, s)
    nt = s // tm
    pairs = width // LANES
    blk = lambda b, i: (b * nt + i, 0)
    return pl.pallas_call(
        _retention_kernel,
        grid=(batch, nt),
        in_specs=[pl.BlockSpec(memory_space=pltpu.SMEM)]
        + [pl.BlockSpec((tm, width), blk)] * 4
        + [pl.BlockSpec((1, width), lambda b, i: (0, 0))],
        out_specs=pl.BlockSpec((tm, width), blk),
        out_shape=jax.ShapeDtypeStruct((t, width), BF16),
        scratch_shapes=[pltpu.VMEM((pairs, LANES, LANES), F32),
                        pltpu.VMEM((pairs, 5, RET_CHUNK, LANES), F32)],
        compiler_params=_params("parallel", "arbitrary"),
        name="retention",
    )(log_gamma, qa, ka, va, ga, gn_gain.reshape(1, width))


def _moba_select_kernel(q_ref, km_ref, qe_ref, qo_ref):
    ts = q_ref.shape[0]
    nb = km_ref.shape[0]
    qf = q_ref[...]
    km = km_ref[...]
    lane = lax.broadcasted_iota(jnp.int32, (ts, LANES), 1)
    even_data = lane < HEAD_DIM
    nt = (((1,), (1,)), ((), ()))
    blk = lax.broadcasted_iota(jnp.int32, (nb, ts), 0).astype(F32)
    col = lax.broadcasted_iota(jnp.int32, (nb, ts), 1)
    own = ((pl.program_id(2) * ts + col) // MOBA_BLOCK).astype(F32)
    past = blk < own
    place_row = lax.broadcasted_iota(jnp.int32, (nb, LANES), 0)
    place_lane = lax.broadcasted_iota(jnp.int32, (nb, LANES), 1)

    for h, out_ref in enumerate((qe_ref, qo_ref)):
        data = even_data if h == 0 else jnp.logical_not(even_data)
        qh = jnp.where(data, qf, 0.0)
        gate = lax.dot_general(km, qh, nt, preferred_element_type=F32,
                               precision=lax.Precision.HIGHEST)
        g = jnp.where(past, gate, NEG_INF)
        sel = jnp.zeros((nb, ts), F32)
        for _ in range(min(MOBA_TOPK, nb)):
            m = jnp.max(g, axis=0, keepdims=True)
            first = jnp.min(jnp.where(g == m, blk, float(nb)), axis=0, keepdims=True)
            pick = blk == first
            sel = jnp.where(pick, 1.0, sel)
            g = jnp.where(pick, NEG_INF, g)
        sel = jnp.where(past, sel, 0.0)
        sel = jnp.where(blk == own, 1.0, sel)
        spare_base = HEAD_DIM if h == 0 else 0
        place = (place_lane == place_row + spare_base).astype(BF16)
        placed = lax.dot_general(sel.astype(BF16), place, (((0,), (0,)), ((), ())),
                                 preferred_element_type=F32)
        bias = (1.0 - placed) * MASK_BIAS
        out_ref[...] = jnp.where(data, qh * (HEAD_DIM ** -0.5 * LOG2_E), bias).astype(out_ref.dtype)


def _moba_select(qb, kmean, batch, ts=2048):
    pairs, t, _ = qb.shape
    s = t // batch
    nb = s // MOBA_BLOCK
    ts = min(ts, s)
    assert nb <= HEAD_DIM and ts % MOBA_BLOCK == 0
    km = kmean.reshape(batch, nb, pairs * LANES)
    qblk = lambda b, h, i: (h, b * (s // ts) + i, 0)
    return pl.pallas_call(
        _moba_select_kernel,
        grid=(batch, pairs, s // ts),
        in_specs=[pl.BlockSpec((None, ts, LANES), qblk),
                  pl.BlockSpec((None, nb, LANES), lambda b, h, i: (b, 0, h))],
        out_specs=[pl.BlockSpec((None, ts, LANES), qblk)] * 2,
        out_shape=[jax.ShapeDtypeStruct((pairs, t, LANES), BF16)] * 2,
        compiler_params=_params("parallel", "parallel", "parallel"),
        name="moba_select",
    )(qb, km)


def _moba_kernel(qe_ref, qo_ref, ke_ref, ko_ref, ve_ref, vo_ref, o_ref, s_ref, m_ref, acc_ref):
    qi = pl.program_id(2)
    tq = qe_ref.shape[0]
    slab = MOBA_GROUP * MOBA_BLOCK
    nt = (((1,), (1,)), ((), ()))
    last = qi // MOBA_GROUP
    heads = ((qe_ref, ke_ref, ve_ref), (qo_ref, ko_ref, vo_ref))
    r = lax.broadcasted_iota(jnp.int32, (tq, slab), 0)
    c = lax.broadcasted_iota(jnp.int32, (tq, slab), 1)
    causal = c - r <= (qi - last * MOBA_GROUP) * MOBA_BLOCK

    def scores(groups, masked):
        for h, (q_ref, k_ref, _) in enumerate(heads):
            mx = None if masked else m_ref[h]
            for t in groups:
                off = t * slab
                sb = lax.dot_general(q_ref[...], k_ref[off:off + slab, :], nt,
                                     preferred_element_type=F32)
                if masked:
                    sb = jnp.where(causal, sb, NEG_INF)
                s_ref[h, t] = sb
                for u in range(slab // LANES):
                    part = sb[:, u * LANES:(u + 1) * LANES]
                    mx = part if mx is None else jnp.maximum(mx, part)
            m_ref[h] = mx

    def weighted_values(groups, first):
        for h, (_, _, v_ref) in enumerate(heads):
            acc = None if first else acc_ref[h]
            for t in groups:
                off = t * slab
                p = jnp.exp2(s_ref[h, t] - row_max[h]).astype(BF16)
                pv = jnp.dot(p, v_ref[off:off + slab, :], preferred_element_type=F32)
                acc = pv if acc is None else acc + pv
            acc_ref[h] = acc

    row_max = [None, None]
    for n_past in range(s_ref.shape[1]):
        @pl.when(last == n_past)
        def _(n_past=n_past):
            scores((n_past,), True)
            if n_past:
                scores(tuple(range(n_past)), False)
            for h in range(2):
                row_max[h] = jnp.max(m_ref[h], axis=-1, keepdims=True)
            weighted_values(tuple(range(n_past + 1)), True)

    lane = lax.broadcasted_iota(jnp.int32, (tq, LANES), 1)
    a0 = acc_ref[0]
    a1 = acc_ref[1]
    o0 = a0 / pltpu.roll(a0, HEAD_DIM, 1)
    o1 = a1 / pltpu.roll(a1, HEAD_DIM, 1)
    o_ref[...] = jnp.where(lane < HEAD_DIM, o0, o1).astype(o_ref.dtype)


def _moba(qbe, qbo, kbe, kbo, vbe, vbo, batch):
    pairs, t, _ = qbe.shape
    s = t // batch
    tq = MOBA_BLOCK
    nq = s // tq
    nb = s // MOBA_BLOCK
    assert nb % MOBA_GROUP == 0
    qblk = lambda b, h, i: (h, b * nq + i, 0)
    kvblk = lambda b, h, i: (h, b, 0)
    return pl.pallas_call(
        _moba_kernel,
        grid=(batch, pairs, nq),
        in_specs=[pl.BlockSpec((None, tq, LANES), qblk)] * 2
        + [pl.BlockSpec((None, s, LANES), kvblk)] * 4,
        out_specs=pl.BlockSpec((None, tq, LANES), qblk),
        out_shape=jax.ShapeDtypeStruct((pairs, t, LANES), BF16),
        scratch_shapes=[pltpu.VMEM((2, nb // MOBA_GROUP, tq, MOBA_GROUP * MOBA_BLOCK), F32),
                        pltpu.VMEM((2, tq, LANES), F32),
                        pltpu.VMEM((2, tq, LANES), F32)],
        compiler_params=_params("parallel", "parallel", "arbitrary"),
        name="moba",
    )(qbe, qbo, kbe, kbo, vbe, vbo)


def _ffn_kernel(*refs, row_split, th, with_mixer):
    if with_mixer:
        x_ref, ya_ref, yb_ref, wo_ref, g0_ref, b0_ref, w1_ref, w2_ref, g_ref, b_ref, o_ref = refs
    else:
        x_ref, w1_ref, w2_ref, g_ref, b_ref, o_ref = refs
    tm = x_ref.shape[0]
    hdim = w1_ref.shape[1]
    rows = tm // row_split
    for part in range(row_split):
        sl = slice(part * rows, (part + 1) * rows)
        x = x_ref[sl, :]
        if with_mixer:
            ka = ya_ref.shape[1]
            yb = jnp.concatenate([yb_ref[p, sl, :] for p in range(yb_ref.shape[0])], axis=1)
            y0 = (jnp.dot(ya_ref[sl, :], wo_ref[:ka, :], preferred_element_type=F32)
                  + jnp.dot(yb, wo_ref[ka:, :], preferred_element_type=F32))
            x = _layer_norm_rows(DEEPNORM_ALPHA * x + y0, g0_ref[...], b0_ref[...])
        xb = x.astype(BF16)
        y = None
        for c in range(hdim // th):
            h = jnp.dot(xb, w1_ref[:, c * th:(c + 1) * th], preferred_element_type=F32)
            h = jnp.square(jnp.maximum(h, 0.0)).astype(BF16)
            yc = jnp.dot(h, w2_ref[c * th:(c + 1) * th, :], preferred_element_type=F32)
            y = yc if y is None else y + yc
        z = DEEPNORM_ALPHA * x + y
        o_ref[sl, :] = _layer_norm_rows(z, g_ref[...], b_ref[...])


def _ffn_ln(x2d, w1_bf16, w2_bf16, gain, bias, mixer=None, tm=1024, th=1024, row_split=2):
    t, d = x2d.shape
    hdim = w1_bf16.shape[1]
    row = lambda i: (i, 0)
    const = lambda i: (0, 0)
    resident = pl.Buffered(1)
    vec = pl.BlockSpec((1, d), const)
    operands = [x2d]
    in_specs = [pl.BlockSpec((tm, d), row)]
    if mixer is not None:
        ya, yb, w_out_bf16, gain0, bias0 = mixer
        operands += [ya, yb, w_out_bf16, gain0.reshape(1, d), bias0.reshape(1, d)]
        in_specs += [pl.BlockSpec((tm, ya.shape[1]), row),
                     pl.BlockSpec((yb.shape[0], tm, LANES), lambda i: (0, i, 0)),
                     pl.BlockSpec(w_out_bf16.shape, const, pipeline_mode=resident), vec, vec]
    operands += [w1_bf16, w2_bf16, gain.reshape(1, d), bias.reshape(1, d)]
    in_specs += [pl.BlockSpec((d, hdim), const, pipeline_mode=resident),
                 pl.BlockSpec((hdim, d), const, pipeline_mode=resident), vec, vec]
    return pl.pallas_call(
        functools.partial(_ffn_kernel, row_split=row_split, th=th, with_mixer=mixer is not None),
        grid=(t // tm,),
        in_specs=in_specs,
        out_specs=pl.BlockSpec((tm, d), row),
        out_shape=jax.ShapeDtypeStruct((t, d), F32),
        compiler_params=_params("parallel"),
        name="ffn_ln",
    )(*operands)


def _gmlp_kernel(x_ref, w_ref, b_ref, lng_ref, lnb_ref, ws_ref, bst_ref, wo_ref, g_ref, beta_ref,
                 o_ref, v_ref, gated_ref, *, row_split):
    tm = x_ref.shape[0]
    gw = v_ref.shape[1]
    c = GMLP_CHUNK
    cw = gw // GMLP_GROUPS
    part_rows = tm // row_split
    r = lax.broadcasted_iota(jnp.int32, (c, c), 0)
    cc = lax.broadcasted_iota(jnp.int32, (c, c), 1)
    causal = cc <= r
    bst = bst_ref[...]
    ws = [jnp.where(causal, ws_ref[g], 0.0).astype(BF16) for g in range(GMLP_GROUPS)]
    for part in range(row_split):
        prow = slice(part * part_rows, (part + 1) * part_rows)
        xb = x_ref[prow, :].astype(BF16)
        v = jnp.dot(xb, w_ref[:, gw:], preferred_element_type=F32) + b_ref[:, gw:]
        v_ref[prow, :] = _layer_norm_rows(jax.nn.gelu(v), lng_ref[...], lnb_ref[...]).astype(BF16)
    for part in range(row_split):
        base = part * part_rows
        prow = slice(base, base + part_rows)
        x = x_ref[prow, :]
        xb = x.astype(BF16)
        for g in range(GMLP_GROUPS):
            cols = slice(g * cw, (g + 1) * cw)
            u = jax.nn.gelu(jnp.dot(xb, w_ref[:, cols], preferred_element_type=F32) + b_ref[:, cols])
            for ci in range(part_rows // c):
                rows = slice(base + ci * c, base + (ci + 1) * c)
                sv = jnp.dot(ws[g], v_ref[rows, cols], preferred_element_type=F32) + bst[:, g:g + 1]
                gated_ref[rows, cols] = (u[ci * c:(ci + 1) * c] * sv).astype(BF16)
        y = jnp.dot(gated_ref[prow, :], wo_ref[...], preferred_element_type=F32)
        z = DEEPNORM_ALPHA * x + y
        o_ref[prow, :] = _layer_norm_rows(z, g_ref[...], beta_ref[...])


def _gmlp_mixer_ln(x2d, w_uv_bf16, b_uv, ln_g, ln_b, w_s, b_s, w_out_bf16, gain, bias,
                   tm=1024, row_split=2):
    t, d = x2d.shape
    n = w_uv_bf16.shape[1]
    gw = n // 2
    row = lambda i: (i, 0)
    const = lambda i: (0, 0)
    resident = pl.Buffered(1)
    return pl.pallas_call(
        functools.partial(_gmlp_kernel, row_split=row_split),
        grid=(t // tm,),
        in_specs=[pl.BlockSpec((tm, d), row),
                  pl.BlockSpec((d, n), const, pipeline_mode=resident),
                  pl.BlockSpec((1, n), const),
                  pl.BlockSpec((1, gw), const),
                  pl.BlockSpec((1, gw), const),
                  pl.BlockSpec(w_s.shape, lambda i: (0, 0, 0)),
                  pl.BlockSpec((GMLP_CHUNK, GMLP_GROUPS), const),
                  pl.BlockSpec((gw, d), const, pipeline_mode=resident),
                  pl.BlockSpec((1, d), const),
                  pl.BlockSpec((1, d), const)],
        out_specs=pl.BlockSpec((tm, d), row),
        out_shape=jax.ShapeDtypeStruct((t, d), F32),
        scratch_shapes=[pltpu.VMEM((tm, gw), BF16), pltpu.VMEM((tm, gw), BF16)],
        compiler_params=_params("parallel"),
        name="gmlp_mixer_ln",
    )(x2d, w_uv_bf16, b_uv.reshape(1, n), ln_g.reshape(1, gw), ln_b.reshape(1, gw), w_s, b_s.T,
      w_out_bf16, gain.reshape(1, d), bias.reshape(1, d))


def kernel(x, positions, ln_gain, ln_bias, mix_w_in, ret_gn_gain, mix_w_out,
           gmlp_w_uv, gmlp_b_uv, gmlp_ln_gain, gmlp_ln_bias, gmlp_w_s, gmlp_b_s,
           gmlp_w_out, ffn_w_in, ffn_w_out):
    batch, seq, d = x.shape
    depth = ln_gain.shape[0]
    h = x.reshape(batch * seq, d)
    cos_t, sin_t = _rope_tables(positions)
    log_gamma = jnp.log1p(-jnp.exp2(-5.0 - jnp.arange(N_RET_HEADS, dtype=F32)))
    for layer in range(depth):
        i = layer // 2
        if layer % 2 == 0:
            qa, ka, va, ga, qb, kbe, kbo, vbe, vbo, kmean = _in_proj(
                h, mix_w_in[i].astype(BF16), cos_t, sin_t, seq // MOBA_BLOCK)
            ya = _retention(qa, ka, va, ga, ret_gn_gain[i], log_gamma, batch)
            qbe, qbo = _moba_select(qb, kmean, batch)
            yb = _moba(qbe, qbo, kbe, kbo, vbe, vbo, batch)
            mixer = (ya, yb, mix_w_out[i].astype(BF16), ln_gain[layer, 0], ln_bias[layer, 0])
        else:
            h = _gmlp_mixer_ln(h, gmlp_w_uv[i].astype(BF16), gmlp_b_uv[i], gmlp_ln_gain[i],
                               gmlp_ln_bias[i], gmlp_w_s[i], gmlp_b_s[i],
                               gmlp_w_out[i].astype(BF16), ln_gain[layer, 0], ln_bias[layer, 0])
            mixer = None
        h = _ffn_ln(h, ffn_w_in[layer].astype(BF16), ffn_w_out[layer].astype(BF16),
                    ln_gain[layer, 1], ln_bias[layer, 1], mixer)
    return h.reshape(batch, seq, d)
```

```python
import functools

import jax
import jax.numpy as jnp
from jax import lax
from jax.experimental import pallas as pl
from jax.experimental.pallas import tpu as pltpu

HEAD_DIM = 64
N_RET_HEADS = 8
RET_WIDTH = N_RET_HEADS * HEAD_DIM
RET_CHUNK = 128
MOBA_BLOCK = 256
MOBA_TOPK = 3
GMLP_CHUNK = 128
GMLP_GROUPS = 8
ROPE_THETA = 10000.0
LN_EPS = 1e-5
DEPTH = 4
DEEPNORM_ALPHA = (2 * DEPTH) ** 0.25

LANES = 128
VMEM_LIMIT_BYTES = 52 * 1024 * 1024

BF16 = jnp.bfloat16
F32 = jnp.float32
NEG_INF = float("-inf")
MOBA_GROUP = 4
MASK_BIAS = -1e30
LOG2_E = 1.4426950408889634


def _params(*sem):
    return pltpu.CompilerParams(dimension_semantics=sem, vmem_limit_bytes=VMEM_LIMIT_BYTES)


def _layer_norm_rows(z, g, b):
    mu = jnp.mean(z, axis=-1, keepdims=True)
    d = z - mu
    var = jnp.mean(d * d, axis=-1, keepdims=True)
    return d * lax.rsqrt(var + LN_EPS) * g + b


def _rope_table_kernel(pos_ref, freq_ref, cos_ref, sin_ref):
    ang = pos_ref[...] * freq_ref[...]
    lane = lax.broadcasted_iota(jnp.int32, ang.shape, 1)
    first_half = (lane % HEAD_DIM) < (HEAD_DIM // 2)
    cos_ref[...] = jnp.cos(ang)
    sin_ref[...] = jnp.where(first_half, -jnp.sin(ang), jnp.sin(ang))


def _rope_tables(positions):
    t = positions.size
    tm = min(t, 2048)
    pos = positions.reshape(t, 1).astype(F32)
    inv_freq = ROPE_THETA ** (-jnp.arange(0, HEAD_DIM, 2, dtype=F32) / HEAD_DIM)
    freq = jnp.tile(inv_freq, LANES // (HEAD_DIM // 2)).reshape(1, LANES)
    return pl.pallas_call(
        _rope_table_kernel,
        grid=(t // tm,),
        in_specs=[pl.BlockSpec((tm, 1), lambda i: (i, 0)),
                  pl.BlockSpec((1, LANES), lambda i: (0, 0))],
        out_specs=[pl.BlockSpec((tm, LANES), lambda i: (i, 0))] * 2,
        out_shape=[jax.ShapeDtypeStruct((t, LANES), F32)] * 2,
        compiler_params=_params("parallel"),
        name="rope_tables",
    )(pos, freq)


def _in_proj_kernel(x_ref, w_ref, cos_ref, sin_ref,
                    qa_ref, ka_ref, va_ref, ga_ref, qb_ref, kbe_ref, kbo_ref, vbe_ref, vbo_ref,
                    km_ref, *, n_blocks):
    tm = x_ref.shape[0]
    width = RET_WIDTH
    pairs = width // LANES
    xb = x_ref[...].astype(BF16)
    cos = jnp.concatenate([cos_ref[...]] * pairs, axis=1)
    sin = jnp.concatenate([sin_ref[...]] * pairs, axis=1)
    lane = lax.broadcasted_iota(jnp.int32, (tm, width), 1)
    first_half = (lane % HEAD_DIM) < (HEAD_DIM // 2)

    def proj(g):
        return jnp.dot(xb, w_ref[:, g * width:(g + 1) * width], preferred_element_type=F32)

    def rope(t):
        partner = jnp.where(first_half,
                            pltpu.roll(t, width - HEAD_DIM // 2, 1),
                            pltpu.roll(t, HEAD_DIM // 2, 1))
        return t * cos + partner * sin

    def store_pairs(ref, val):
        for p in range(pairs):
            ref[p] = val[:, p * LANES:(p + 1) * LANES].astype(ref.dtype)

    qa_ref[...] = rope(proj(0)).astype(qa_ref.dtype)
    ka_ref[...] = rope(proj(1)).astype(ka_ref.dtype)
    va_ref[...] = proj(2).astype(va_ref.dtype)
    ga_ref[...] = proj(3).astype(ga_ref.dtype)
    store_pairs(qb_ref, rope(proj(4)))
    kb = rope(proj(5))
    vb = proj(6)
    pair_lane = lane % LANES
    even_data = pair_lane < HEAD_DIM
    row = lax.broadcasted_iota(jnp.int32, (tm, width), 0)
    block = ((pl.program_id(0) * tm + row) // MOBA_BLOCK) % n_blocks
    onehot_e = (pair_lane - HEAD_DIM == block).astype(F32)
    onehot_o = (pair_lane == block).astype(F32)
    store_pairs(kbe_ref, jnp.where(even_data, kb, onehot_e))
    store_pairs(kbo_ref, jnp.where(even_data, onehot_o, kb))
    store_pairs(vbe_ref, jnp.where(even_data, vb, 1.0))
    store_pairs(vbo_ref, jnp.where(even_data, 1.0, vb))
    for blk in range(tm // MOBA_BLOCK):
        km_ref[blk] = jnp.mean(kb[blk * MOBA_BLOCK:(blk + 1) * MOBA_BLOCK], axis=0, keepdims=True)


def _in_proj(x2d, w_stack_bf16, layer, cos_t, sin_t, n_blocks, tm=512):
    t, d = x2d.shape
    n = w_stack_bf16.shape[2]
    width = RET_WIDTH
    assert n_blocks <= HEAD_DIM and tm % MOBA_BLOCK == 0
    row = lambda i: (i, 0)
    pairs = width // LANES
    row_dtypes = [BF16, F32, BF16, F32]
    pair_dtypes = [F32, BF16, BF16, BF16, BF16]
    return pl.pallas_call(
        functools.partial(_in_proj_kernel, n_blocks=n_blocks),
        grid=(t // tm,),
        in_specs=[pl.BlockSpec((tm, d), row),
                  pl.BlockSpec((None, d, n), lambda i: (layer, 0, 0)),
                  pl.BlockSpec((tm, LANES), row),
                  pl.BlockSpec((tm, LANES), row)],
        out_specs=[pl.BlockSpec((tm, width), row)] * len(row_dtypes)
        + [pl.BlockSpec((pairs, tm, LANES), lambda i: (0, i, 0))] * len(pair_dtypes)
        + [pl.BlockSpec((tm // MOBA_BLOCK, 1, width), lambda i: (i, 0, 0))],
        out_shape=[jax.ShapeDtypeStruct((t, width), dt) for dt in row_dtypes]
        + [jax.ShapeDtypeStruct((pairs, t, LANES), dt) for dt in pair_dtypes]
        + [jax.ShapeDtypeStruct((t // MOBA_BLOCK, 1, width), F32)],
        compiler_params=_params("parallel"),
        name="in_proj",
    )(x2d, w_stack_bf16, cos_t, sin_t)


def _retention_kernel(lg_ref, q_ref, k_ref, v_ref, g_ref, gain_ref, o_ref, state_ref, decay_ref):
    c = RET_CHUNK
    tm, width = q_ref.shape
    lane = lax.broadcasted_iota(jnp.int32, (c, LANES), 1)
    row = lax.broadcasted_iota(jnp.int32, (c, LANES), 0)
    head0 = lane < HEAD_DIM
    same_head = (row < HEAD_DIM) == head0
    nt = (((1,), (1,)), ((), ()))
    tn = (((0,), (0,)), ((), ()))

    @pl.when(pl.program_id(1) == 0)
    def _():
        state_ref[...] = jnp.zeros_like(state_ref)
        idx = row.astype(F32)
        diff = (row - lane).astype(F32)
        dpos = jnp.maximum(diff, 0.0)
        for hp in range(width // LANES):
            lg0 = lg_ref[2 * hp]
            lg1 = lg_ref[2 * hp + 1]
            lg_lane = jnp.where(head0, lg0, lg1)
            decay_ref[hp, 0] = jnp.exp(lg_lane * (idx + 1.0))
            decay_ref[hp, 1] = jnp.exp(lg_lane * (c - 1.0 - idx))
            decay_ref[hp, 2] = jnp.exp(lg_lane * float(c))
            decay_ref[hp, 3] = jnp.where(diff >= 0, jnp.exp(lg0 * dpos), 0.0)
            decay_ref[hp, 4] = jnp.where(diff >= 0, jnp.exp(lg1 * dpos), 0.0)

    for hp in range(width // LANES):
        cols = slice(hp * LANES, (hp + 1) * LANES)
        xi = decay_ref[hp, 0]
        zeta = decay_ref[hp, 1]
        chunk_decay = decay_ref[hp, 2]
        decay0 = decay_ref[hp, 3]
        decay1 = decay_ref[hp, 4]
        gain = gain_ref[:, cols]
        for ci in range(tm // c):
            sl = slice(ci * c, (ci + 1) * c)
            q = q_ref[sl, cols]
            kf = k_ref[sl, cols] * (HEAD_DIM ** -0.5)
            v = v_ref[sl, cols]
            kb = kf.astype(BF16)
            zero = jnp.zeros_like(q)
            q0 = jnp.where(head0, q, zero)
            q1 = jnp.where(head0, zero, q)
            s0 = lax.dot_general(q0, kb, nt, preferred_element_type=F32) * decay0
            s1 = lax.dot_general(q1, kb, nt, preferred_element_type=F32) * decay1
            inner = jnp.where(head0,
                              jnp.dot(s0.astype(BF16), v, preferred_element_type=F32),
                              jnp.dot(s1.astype(BF16), v, preferred_element_type=F32))
            state = state_ref[hp]
            cross = jnp.dot(q, state.astype(BF16), preferred_element_type=F32) * xi
            kv = lax.dot_general((kf * zeta).astype(BF16), v, tn, preferred_element_type=F32)
            state_ref[hp] = state * chunk_decay + jnp.where(same_head, kv, 0.0)

            y = inner + cross
            inv = 1.0 / HEAD_DIM
            mu = jnp.where(head0,
                           jnp.sum(jnp.where(head0, y, 0.0), axis=-1, keepdims=True),
                           jnp.sum(jnp.where(head0, 0.0, y), axis=-1, keepdims=True)) * inv
            d = y - mu
            dd = d * d
            var = jnp.where(head0,
                            jnp.sum(jnp.where(head0, dd, 0.0), axis=-1, keepdims=True),
                            jnp.sum(jnp.where(head0, 0.0, dd), axis=-1, keepdims=True)) * inv
            yn = d * lax.rsqrt(var + LN_EPS) * gain
            o_ref[sl, cols] = (yn * jax.nn.silu(g_ref[sl, cols])).astype(o_ref.dtype)


def _retention(qa, ka, va, ga, gn_gain, log_gamma, batch, tm=1024):
    t, width = qa.shape
    s = t // batch
    tm = min(tm, s)
    nt = s // tm
    pairs = width // LANES
    blk = lambda b, i: (b * nt + i, 0)
    return pl.pallas_call(
        _retention_kernel,
        grid=(batch, nt),
        in_specs=[pl.BlockSpec(memory_space=pltpu.SMEM)]
        + [pl.BlockSpec((tm, width), blk)] * 4
        + [pl.BlockSpec((1, width), lambda b, i: (0, 0))],
        out_specs=pl.BlockSpec((tm, width), blk),
        out_shape=jax.ShapeDtypeStruct((t, width), BF16),
        scratch_shapes=[pltpu.VMEM((pairs, LANES, LANES), F32),
                        pltpu.VMEM((pairs, 5, RET_CHUNK, LANES), F32)],
        compiler_params=_params("parallel", "arbitrary"),
        name="retention",
    )(log_gamma, qa, ka, va, ga, gn_gain.reshape(1, width))


def _moba_select_kernel(q_ref, km_ref, qe_ref, qo_ref):
    ts = q_ref.shape[0]
    nb = km_ref.shape[0]
    qf = q_ref[...]
    km = km_ref[...]
    lane = lax.broadcasted_iota(jnp.int32, (ts, LANES), 1)
    even_data = lane < HEAD_DIM
    nt = (((1,), (1,)), ((), ()))
    blk = lax.broadcasted_iota(jnp.int32, (nb, ts), 0).astype(F32)
    col = lax.broadcasted_iota(jnp.int32, (nb, ts), 1)
    own = ((pl.program_id(2) * ts + col) // MOBA_BLOCK).astype(F32)
    past = blk < own
    place_row = lax.broadcasted_iota(jnp.int32, (nb, LANES), 0)
    place_lane = lax.broadcasted_iota(jnp.int32, (nb, LANES), 1)

    for h, out_ref in enumerate((qe_ref, qo_ref)):
        data = even_data if h == 0 else jnp.logical_not(even_data)
        qh = jnp.where(data, qf, 0.0)
        gate = lax.dot_general(km, qh, nt, preferred_element_type=F32,
                               precision=lax.Precision.HIGHEST)
        g = jnp.where(past, gate, NEG_INF)
        sel = jnp.zeros((nb, ts), F32)
        for _ in range(min(MOBA_TOPK, nb)):
            m = jnp.max(g, axis=0, keepdims=True)
            first = jnp.min(jnp.where(g == m, blk, float(nb)), axis=0, keepdims=True)
            pick = blk == first
            sel = jnp.where(pick, 1.0, sel)
            g = jnp.where(pick, NEG_INF, g)
        sel = jnp.where(past, sel, 0.0)
        sel = jnp.where(blk == own, 1.0, sel)
        spare_base = HEAD_DIM if h == 0 else 0
        place = (place_lane == place_row + spare_base).astype(BF16)
        placed = lax.dot_general(sel.astype(BF16), place, (((0,), (0,)), ((), ())),
                                 preferred_element_type=F32)
        bias = (1.0 - placed) * MASK_BIAS
        out_ref[...] = jnp.where(data, qh * (HEAD_DIM ** -0.5 * LOG2_E), bias).astype(out_ref.dtype)


def _moba_select(qb, kmean, batch, ts=2048):
    pairs, t, _ = qb.shape
    s = t // batch
    nb = s // MOBA_BLOCK
    ts = min(ts, s)
    assert nb <= HEAD_DIM and ts % MOBA_BLOCK == 0
    km = kmean.reshape(batch, nb, pairs * LANES)
    qblk = lambda b, h, i: (h, b * (s // ts) + i, 0)
    return pl.pallas_call(
        _moba_select_kernel,
        grid=(batch, pairs, s // ts),
        in_specs=[pl.BlockSpec((None, ts, LANES), qblk),
                  pl.BlockSpec((None, nb, LANES), lambda b, h, i: (b, 0, h))],
        out_specs=[pl.BlockSpec((None, ts, LANES), qblk)] * 2,
        out_shape=[jax.ShapeDtypeStruct((pairs, t, LANES), BF16)] * 2,
        compiler_params=_params("parallel", "parallel", "parallel"),
        name="moba_select",
    )(qb, km)


def _moba_kernel(qe_ref, qo_ref, ke_ref, ko_ref, ve_ref, vo_ref, o_ref, s_ref, m_ref, acc_ref):
    qi = pl.program_id(2)
    tq = qe_ref.shape[0]
    slab = MOBA_GROUP * MOBA_BLOCK
    nt = (((1,), (1,)), ((), ()))
    last = qi // MOBA_GROUP
    heads = ((qe_ref, ke_ref, ve_ref), (qo_ref, ko_ref, vo_ref))
    r = lax.broadcasted_iota(jnp.int32, (tq, slab), 0)
    c = lax.broadcasted_iota(jnp.int32, (tq, slab), 1)
    causal = c - r <= (qi - last * MOBA_GROUP) * MOBA_BLOCK

    def scores(groups, masked):
        for h, (q_ref, k_ref, _) in enumerate(heads):
            mx = None if masked else m_ref[h]
            for t in groups:
                off = t * slab
                sb = lax.dot_general(q_ref[...], k_ref[off:off + slab, :], nt,
                                     preferred_element_type=F32)
                if masked:
                    sb = jnp.where(causal, sb, NEG_INF)
                s_ref[h, t] = sb
                for u in range(slab // LANES):
                    part = sb[:, u * LANES:(u + 1) * LANES]
                    mx = part if mx is None else jnp.maximum(mx, part)
            m_ref[h] = mx

    def weighted_values(groups, first):
        for h, (_, _, v_ref) in enumerate(heads):
            acc = None if first else acc_ref[h]
            for t in groups:
                off = t * slab
                p = jnp.exp2(s_ref[h, t] - row_max[h]).astype(BF16)
                pv = jnp.dot(p, v_ref[off:off + slab, :], preferred_element_type=F32)
                acc = pv if acc is None else acc + pv
            acc_ref[h] = acc

    row_max = [None, None]
    for n_past in range(s_ref.shape[1]):
        @pl.when(last == n_past)
        def _(n_past=n_past):
            scores((n_past,), True)
            if n_past:
                scores(tuple(range(n_past)), False)
            for h in range(2):
                row_max[h] = jnp.max(m_ref[h], axis=-1, keepdims=True)
            weighted_values(tuple(range(n_past + 1)), True)

    lane = lax.broadcasted_iota(jnp.int32, (tq, LANES), 1)
    a0 = acc_ref[0]
    a1 = acc_ref[1]
    o0 = a0 / pltpu.roll(a0, HEAD_DIM, 1)
    o1 = a1 / pltpu.roll(a1, HEAD_DIM, 1)
    o_ref[...] = jnp.where(lane < HEAD_DIM, o0, o1).astype(o_ref.dtype)


def _moba(qbe, qbo, kbe, kbo, vbe, vbo, batch):
    pairs, t, _ = qbe.shape
    s = t // batch
    tq = MOBA_BLOCK
    nq = s // tq
    nb = s // MOBA_BLOCK
    assert nb % MOBA_GROUP == 0
    qblk = lambda b, h, i: (h, b * nq + i, 0)
    kvblk = lambda b, h, i: (h, b, 0)
    return pl.pallas_call(
        _moba_kernel,
        grid=(batch, pairs, nq),
        in_specs=[pl.BlockSpec((None, tq, LANES), qblk)] * 2
        + [pl.BlockSpec((None, s, LANES), kvblk)] * 4,
        out_specs=pl.BlockSpec((None, tq, LANES), qblk),
        out_shape=jax.ShapeDtypeStruct((pairs, t, LANES), BF16),
        scratch_shapes=[pltpu.VMEM((2, nb // MOBA_GROUP, tq, MOBA_GROUP * MOBA_BLOCK), F32),
                        pltpu.VMEM((2, tq, LANES), F32),
                        pltpu.VMEM((2, tq, LANES), F32)],
        compiler_params=_params("parallel", "parallel", "arbitrary"),
        name="moba",
    )(qbe, qbo, kbe, kbo, vbe, vbo)


def _ffn_kernel(*refs, row_split, th, with_mixer):
    if with_mixer:
        x_ref, ya_ref, yb_ref, wo_ref, g0_ref, b0_ref, w1_ref, w2_ref, g_ref, b_ref, o_ref = refs
    else:
        x_ref, w1_ref, w2_ref, g_ref, b_ref, o_ref = refs
    tm = x_ref.shape[0]
    hdim = w1_ref.shape[1]
    rows = tm // row_split
    for part in range(row_split):
        sl = slice(part * rows, (part + 1) * rows)
        x = x_ref[sl, :]
        if with_mixer:
            ka = ya_ref.shape[1]
            yb = jnp.concatenate([yb_ref[p, sl, :] for p in range(yb_ref.shape[0])], axis=1)
            y0 = (jnp.dot(ya_ref[sl, :], wo_ref[:ka, :], preferred_element_type=F32)
                  + jnp.dot(yb, wo_ref[ka:, :], preferred_element_type=F32))
            x = _layer_norm_rows(DEEPNORM_ALPHA * x + y0, g0_ref[...], b0_ref[...])
        xb = x.astype(BF16)
        y = None
        for c in range(hdim // th):
            h = jnp.dot(xb, w1_ref[:, c * th:(c + 1) * th], preferred_element_type=F32)
            h = jnp.square(jnp.maximum(h, 0.0)).astype(BF16)
            yc = jnp.dot(h, w2_ref[c * th:(c + 1) * th, :], preferred_element_type=F32)
            y = yc if y is None else y + yc
        z = DEEPNORM_ALPHA * x + y
        o_ref[sl, :] = _layer_norm_rows(z, g_ref[...], b_ref[...])


def _ffn_ln(x2d, w1_stack_bf16, w2_stack_bf16, layer, gain, bias, mixer=None,
            tm=1024, th=1024, row_split=2):
    t, d = x2d.shape
    hdim = w1_stack_bf16.shape[2]
    row = lambda i: (i, 0)
    const = lambda i: (0, 0)
    resident = pl.Buffered(1)
    vec = pl.BlockSpec((1, d), const)
    operands = [x2d]
    in_specs = [pl.BlockSpec((tm, d), row)]
    if mixer is not None:
        ya, yb, w_out_stack_bf16, mixer_layer, gain0, bias0 = mixer
        operands += [ya, yb, w_out_stack_bf16, gain0.reshape(1, d), bias0.reshape(1, d)]
        in_specs += [pl.BlockSpec((tm, ya.shape[1]), row),
                     pl.BlockSpec((yb.shape[0], tm, LANES), lambda i: (0, i, 0)),
                     pl.BlockSpec((None,) + w_out_stack_bf16.shape[1:], lambda i: (mixer_layer, 0, 0),
                                  pipeline_mode=resident), vec, vec]
    operands += [w1_stack_bf16, w2_stack_bf16, gain.reshape(1, d), bias.reshape(1, d)]
    in_specs += [pl.BlockSpec((None, d, hdim), lambda i: (layer, 0, 0), pipeline_mode=resident),
                 pl.BlockSpec((None, hdim, d), lambda i: (layer, 0, 0), pipeline_mode=resident),
                 vec, vec]
    return pl.pallas_call(
        functools.partial(_ffn_kernel, row_split=row_split, th=th, with_mixer=mixer is not None),
        grid=(t // tm,),
        in_specs=in_specs,
        out_specs=pl.BlockSpec((tm, d), row),
        out_shape=jax.ShapeDtypeStruct((t, d), F32),
        compiler_params=_params("parallel"),
        name="ffn_ln",
    )(*operands)


def _gmlp_kernel(x_ref, w_ref, b_ref, lng_ref, lnb_ref, ws_ref, bst_ref, wo_ref, g_ref, beta_ref,
                 o_ref, v_ref, gated_ref, *, row_split):
    tm = x_ref.shape[0]
    gw = v_ref.shape[1]
    c = GMLP_CHUNK
    cw = gw // GMLP_GROUPS
    part_rows = tm // row_split
    r = lax.broadcasted_iota(jnp.int32, (c, c), 0)
    cc = lax.broadcasted_iota(jnp.int32, (c, c), 1)
    causal = cc <= r
    bst = bst_ref[...]
    ws = [jnp.where(causal, ws_ref[g], 0.0).astype(BF16) for g in range(GMLP_GROUPS)]
    for part in range(row_split):
        base = part * part_rows
        prow = slice(base, base + part_rows)
        x = x_ref[prow, :]
        xb = x.astype(BF16)
        v = jnp.dot(xb, w_ref[:, gw:], preferred_element_type=F32) + b_ref[:, gw:]
        v_ref[prow, :] = _layer_norm_rows(jax.nn.gelu(v), lng_ref[...], lnb_ref[...]).astype(BF16)
        for g in range(GMLP_GROUPS):
            cols = slice(g * cw, (g + 1) * cw)
            u = jax.nn.gelu(jnp.dot(xb, w_ref[:, cols], preferred_element_type=F32) + b_ref[:, cols])
            for ci in range(part_rows // c):
                rows = slice(base + ci * c, base + (ci + 1) * c)
                sv = jnp.dot(ws[g], v_ref[rows, cols], preferred_element_type=F32) + bst[:, g:g + 1]
                gated_ref[rows, cols] = (u[ci * c:(ci + 1) * c] * sv).astype(BF16)
        y = jnp.dot(gated_ref[prow, :], wo_ref[...], preferred_element_type=F32)
        z = DEEPNORM_ALPHA * x + y
        o_ref[prow, :] = _layer_norm_rows(z, g_ref[...], beta_ref[...])


def _gmlp_mixer_ln(x2d, w_uv_stack_bf16, w_out_stack_bf16, layer, b_uv, ln_g, ln_b, w_s, b_s,
                   gain, bias, tm=1024, row_split=2):
    t, d = x2d.shape
    n = w_uv_stack_bf16.shape[2]
    gw = n // 2
    row = lambda i: (i, 0)
    const = lambda i: (0, 0)
    resident = pl.Buffered(1)
    return pl.pallas_call(
        functools.partial(_gmlp_kernel, row_split=row_split),
        grid=(t // tm,),
        in_specs=[pl.BlockSpec((tm, d), row),
                  pl.BlockSpec((None, d, n), lambda i: (layer, 0, 0), pipeline_mode=resident),
                  pl.BlockSpec((1, n), const),
                  pl.BlockSpec((1, gw), const),
                  pl.BlockSpec((1, gw), const),
                  pl.BlockSpec(w_s.shape, lambda i: (0, 0, 0)),
                  pl.BlockSpec((GMLP_CHUNK, GMLP_GROUPS), const),
                  pl.BlockSpec((None, gw, d), lambda i: (layer, 0, 0), pipeline_mode=resident),
                  pl.BlockSpec((1, d), const),
                  pl.BlockSpec((1, d), const)],
        out_specs=pl.BlockSpec((tm, d), row),
        out_shape=jax.ShapeDtypeStruct((t, d), F32),
        scratch_shapes=[pltpu.VMEM((tm, gw), BF16), pltpu.VMEM((tm, gw), BF16)],
        compiler_params=_params("parallel"),
        name="gmlp_mixer_ln",
    )(x2d, w_uv_stack_bf16, b_uv.reshape(1, n), ln_g.reshape(1, gw), ln_b.reshape(1, gw), w_s, b_s.T,
      w_out_stack_bf16, gain.reshape(1, d), bias.reshape(1, d))


def kernel(x, positions, ln_gain, ln_bias, mix_w_in, ret_gn_gain, mix_w_out,
           gmlp_w_uv, gmlp_b_uv, gmlp_ln_gain, gmlp_ln_bias, gmlp_w_s, gmlp_b_s,
           gmlp_w_out, ffn_w_in, ffn_w_out):
    batch, seq, d = x.shape
    depth = ln_gain.shape[0]
    h = x.reshape(batch * seq, d)
    cos_t, sin_t = _rope_tables(positions)
    log_gamma = jnp.log1p(-jnp.exp2(-5.0 - jnp.arange(N_RET_HEADS, dtype=F32)))
    mix_w_in_b, mix_w_out_b = mix_w_in.astype(BF16), mix_w_out.astype(BF16)
    gmlp_w_uv_b, gmlp_w_out_b = gmlp_w_uv.astype(BF16), gmlp_w_out.astype(BF16)
    ffn_w_in_b, ffn_w_out_b = ffn_w_in.astype(BF16), ffn_w_out.astype(BF16)
    for layer in range(depth):
        i = layer // 2
        if layer % 2 == 0:
            qa, ka, va, ga, qb, kbe, kbo, vbe, vbo, kmean = _in_proj(
                h, mix_w_in_b, i, cos_t, sin_t, seq // MOBA_BLOCK)
            ya = _retention(qa, ka, va, ga, ret_gn_gain[i], log_gamma, batch)
            qbe, qbo = _moba_select(qb, kmean, batch)
            yb = _moba(qbe, qbo, kbe, kbo, vbe, vbo, batch)
            mixer = (ya, yb, mix_w_out_b, i, ln_gain[layer, 0], ln_bias[layer, 0])
        else:
            h = _gmlp_mixer_ln(h, gmlp_w_uv_b, gmlp_w_out_b, i, gmlp_b_uv[i], gmlp_ln_gain[i],
                               gmlp_ln_bias[i], gmlp_w_s[i], gmlp_b_s[i],
                               ln_gain[layer, 0], ln_bias[layer, 0])
            mixer = None
        h = _ffn_ln(h, ffn_w_in_b, ffn_w_out_b, layer, ln_gain[layer, 1], ln_bias[layer, 1], mixer)
    return h.reshape(batch, seq, d)
```

```python
import functools

import jax
import jax.numpy as jnp
from jax import lax
from jax.experimental import pallas as pl
from jax.experimental.pallas import tpu as pltpu

HEAD_DIM = 64
N_RET_HEADS = 8
RET_WIDTH = N_RET_HEADS * HEAD_DIM
RET_CHUNK = 128
MOBA_BLOCK = 256
MOBA_TOPK = 3
GMLP_CHUNK = 128
GMLP_GROUPS = 8
ROPE_THETA = 10000.0
LN_EPS = 1e-5
DEPTH = 4
DEEPNORM_ALPHA = (2 * DEPTH) ** 0.25

LANES = 128
VMEM_LIMIT_BYTES = 52 * 1024 * 1024

BF16 = jnp.bfloat16
F32 = jnp.float32
NEG_INF = float("-inf")
MOBA_GROUP = 4
MASK_BIAS = -1e30
LOG2_E = 1.4426950408889634


def _params(*sem):
    return pltpu.CompilerParams(dimension_semantics=sem, vmem_limit_bytes=VMEM_LIMIT_BYTES)


def _layer_norm_rows(z, g, b):
    mu = jnp.mean(z, axis=-1, keepdims=True)
    d = z - mu
    var = jnp.mean(d * d, axis=-1, keepdims=True)
    return d * lax.rsqrt(var + LN_EPS) * g + b


def _rope_table_kernel(pos_ref, freq_ref, cos_ref, sin_ref):
    ang = pos_ref[...] * freq_ref[...]
    lane = lax.broadcasted_iota(jnp.int32, ang.shape, 1)
    first_half = (lane % HEAD_DIM) < (HEAD_DIM // 2)
    cos_ref[...] = jnp.cos(ang)
    sin_ref[...] = jnp.where(first_half, -jnp.sin(ang), jnp.sin(ang))


def _rope_tables(positions):
    t = positions.size
    tm = min(t, 2048)
    pos = positions.reshape(t, 1).astype(F32)
    inv_freq = ROPE_THETA ** (-jnp.arange(0, HEAD_DIM, 2, dtype=F32) / HEAD_DIM)
    freq = jnp.tile(inv_freq, LANES // (HEAD_DIM // 2)).reshape(1, LANES)
    return pl.pallas_call(
        _rope_table_kernel,
        grid=(t // tm,),
        in_specs=[pl.BlockSpec((tm, 1), lambda i: (i, 0)),
                  pl.BlockSpec((1, LANES), lambda i: (0, 0))],
        out_specs=[pl.BlockSpec((tm, LANES), lambda i: (i, 0))] * 2,
        out_shape=[jax.ShapeDtypeStruct((t, LANES), F32)] * 2,
        compiler_params=_params("parallel"),
        name="rope_tables",
    )(pos, freq)


def _in_proj_kernel(x_ref, w_ref, cos_ref, sin_ref,
                    qa_ref, ka_ref, va_ref, ga_ref, qb_ref, kbe_ref, kbo_ref, vbe_ref, vbo_ref,
                    km_ref, *, n_blocks):
    tm = x_ref.shape[0]
    width = RET_WIDTH
    pairs = width // LANES
    xb = x_ref[...].astype(BF16)
    cos = jnp.concatenate([cos_ref[...]] * pairs, axis=1)
    sin = jnp.concatenate([sin_ref[...]] * pairs, axis=1)
    lane = lax.broadcasted_iota(jnp.int32, (tm, width), 1)
    first_half = (lane % HEAD_DIM) < (HEAD_DIM // 2)

    def proj(g):
        return jnp.dot(xb, w_ref[:, g * width:(g + 1) * width], preferred_element_type=F32)

    def rope(t):
        partner = jnp.where(first_half,
                            pltpu.roll(t, width - HEAD_DIM // 2, 1),
                            pltpu.roll(t, HEAD_DIM // 2, 1))
        return t * cos + partner * sin

    def store_pairs(ref, val):
        for p in range(pairs):
            ref[p] = val[:, p * LANES:(p + 1) * LANES].astype(ref.dtype)

    qa_ref[...] = rope(proj(0)).astype(qa_ref.dtype)
    ka_ref[...] = rope(proj(1)).astype(ka_ref.dtype)
    va_ref[...] = proj(2).astype(va_ref.dtype)
    ga_ref[...] = proj(3).astype(ga_ref.dtype)
    store_pairs(qb_ref, rope(proj(4)))
    kb = rope(proj(5))
    vb = proj(6)
    pair_lane = lane % LANES
    even_data = pair_lane < HEAD_DIM
    row = lax.broadcasted_iota(jnp.int32, (tm, width), 0)
    block = ((pl.program_id(0) * tm + row) // MOBA_BLOCK) % n_blocks
    onehot_e = (pair_lane - HEAD_DIM == block).astype(F32)
    onehot_o = (pair_lane == block).astype(F32)
    store_pairs(kbe_ref, jnp.where(even_data, kb, onehot_e))
    store_pairs(kbo_ref, jnp.where(even_data, onehot_o, kb))
    store_pairs(vbe_ref, jnp.where(even_data, vb, 1.0))
    store_pairs(vbo_ref, jnp.where(even_data, 1.0, vb))
    for blk in range(tm // MOBA_BLOCK):
        km_ref[blk] = jnp.mean(kb[blk * MOBA_BLOCK:(blk + 1) * MOBA_BLOCK], axis=0, keepdims=True)


def _in_proj(x2d, w_stack_bf16, layer, cos_t, sin_t, n_blocks, tm=512):
    t, d = x2d.shape
    n = w_stack_bf16.shape[2]
    width = RET_WIDTH
    assert n_blocks <= HEAD_DIM and tm % MOBA_BLOCK == 0
    row = lambda i: (i, 0)
    pairs = width // LANES
    row_dtypes = [BF16, F32, BF16, F32]
    pair_dtypes = [F32, BF16, BF16, BF16, BF16]
    return pl.pallas_call(
        functools.partial(_in_proj_kernel, n_blocks=n_blocks),
        grid=(t // tm,),
        in_specs=[pl.BlockSpec((tm, d), row),
                  pl.BlockSpec((None, d, n), lambda i: (layer, 0, 0)),
                  pl.BlockSpec((tm, LANES), row),
                  pl.BlockSpec((tm, LANES), row)],
        out_specs=[pl.BlockSpec((tm, width), row)] * len(row_dtypes)
        + [pl.BlockSpec((pairs, tm, LANES), lambda i: (0, i, 0))] * len(pair_dtypes)
        + [pl.BlockSpec((tm // MOBA_BLOCK, 1, width), lambda i: (i, 0, 0))],
        out_shape=[jax.ShapeDtypeStruct((t, width), dt) for dt in row_dtypes]
        + [jax.ShapeDtypeStruct((pairs, t, LANES), dt) for dt in pair_dtypes]
        + [jax.ShapeDtypeStruct((t // MOBA_BLOCK, 1, width), F32)],
        compiler_params=_params("parallel"),
        name="in_proj",
    )(x2d, w_stack_bf16, cos_t, sin_t)


def _retention_kernel(lg_ref, q_ref, k_ref, v_ref, g_ref, gain_ref, o_ref, state_ref, decay_ref):
    c = RET_CHUNK
    tm, width = q_ref.shape
    lane = lax.broadcasted_iota(jnp.int32, (c, LANES), 1)
    row = lax.broadcasted_iota(jnp.int32, (c, LANES), 0)
    head0 = lane < HEAD_DIM
    same_head = (row < HEAD_DIM) == head0
    nt = (((1,), (1,)), ((), ()))
    tn = (((0,), (0,)), ((), ()))

    @pl.when(pl.program_id(1) == 0)
    def _():
        state_ref[...] = jnp.zeros_like(state_ref)
        idx = row.astype(F32)
        diff = (row - lane).astype(F32)
        dpos = jnp.maximum(diff, 0.0)
        for hp in range(width // LANES):
            lg0 = lg_ref[2 * hp]
            lg1 = lg_ref[2 * hp + 1]
            lg_lane = jnp.where(head0, lg0, lg1)
            decay_ref[hp, 0] = jnp.exp(lg_lane * (idx + 1.0))
            decay_ref[hp, 1] = jnp.exp(lg_lane * (c - 1.0 - idx))
            decay_ref[hp, 2] = jnp.exp(lg_lane * float(c))
            decay_ref[hp, 3] = jnp.where(diff >= 0, jnp.exp(lg0 * dpos), 0.0)
            decay_ref[hp, 4] = jnp.where(diff >= 0, jnp.exp(lg1 * dpos), 0.0)

    for hp in range(width // LANES):
        cols = slice(hp * LANES, (hp + 1) * LANES)
        xi = decay_ref[hp, 0]
        zeta = decay_ref[hp, 1]
        chunk_decay = decay_ref[hp, 2]
        decay0 = decay_ref[hp, 3]
        decay1 = decay_ref[hp, 4]
        gain = gain_ref[:, cols]
        for ci in range(tm // c):
            sl = slice(ci * c, (ci + 1) * c)
            q = q_ref[sl, cols]
            kf = k_ref[sl, cols] * (HEAD_DIM ** -0.5)
            v = v_ref[sl, cols]
            kb = kf.astype(BF16)
            zero = jnp.zeros_like(q)
            q0 = jnp.where(head0, q, zero)
            q1 = jnp.where(head0, zero, q)
            s0 = lax.dot_general(q0, kb, nt, preferred_element_type=F32) * decay0
            s1 = lax.dot_general(q1, kb, nt, preferred_element_type=F32) * decay1
            inner = jnp.where(head0,
                              jnp.dot(s0.astype(BF16), v, preferred_element_type=F32),
                              jnp.dot(s1.astype(BF16), v, preferred_element_type=F32))
            state = state_ref[hp]
            cross = jnp.dot(q, state.astype(BF16), preferred_element_type=F32) * xi
            kv = lax.dot_general((kf * zeta).astype(BF16), v, tn, preferred_element_type=F32)
            state_ref[hp] = state * chunk_decay + jnp.where(same_head, kv, 0.0)

            y = inner + cross
            inv = 1.0 / HEAD_DIM
            mu = jnp.where(head0,
                           jnp.sum(jnp.where(head0, y, 0.0), axis=-1, keepdims=True),
                           jnp.sum(jnp.where(head0, 0.0, y), axis=-1, keepdims=True)) * inv
            d = y - mu
            dd = d * d
            var = jnp.where(head0,
                            jnp.sum(jnp.where(head0, dd, 0.0), axis=-1, keepdims=True),
                            jnp.sum(jnp.where(head0, 0.0, dd), axis=-1, keepdims=True)) * inv
            yn = d * lax.rsqrt(var + LN_EPS) * gain
            o_ref[sl, cols] = (yn * jax.nn.silu(g_ref[sl, cols])).astype(o_ref.dtype)


def _retention(qa, ka, va, ga, gn_gain, log_gamma, batch, tm=2048):
    t, width = qa.shape
    s = t // batch
    tm = min(tm, s)
    nt = s // tm
    pairs = width // LANES
    blk = lambda b, i: (b * nt + i, 0)
    return pl.pallas_call(
        _retention_kernel,
        grid=(batch, nt),
        in_specs=[pl.BlockSpec(memory_space=pltpu.SMEM)]
        + [pl.BlockSpec((tm, width), blk)] * 4
        + [pl.BlockSpec((1, width), lambda b, i: (0, 0))],
        out_specs=pl.BlockSpec((tm, width), blk),
        out_shape=jax.ShapeDtypeStruct((t, width), BF16),
        scratch_shapes=[pltpu.VMEM((pairs, LANES, LANES), F32),
                        pltpu.VMEM((pairs, 5, RET_CHUNK, LANES), F32)],
        compiler_params=_params("parallel", "arbitrary"),
        name="retention",
    )(log_gamma, qa, ka, va, ga, gn_gain.reshape(1, width))


def _moba_select_kernel(q_ref, km_ref, qe_ref, qo_ref):
    ts = q_ref.shape[0]
    nb = km_ref.shape[0]
    qf = q_ref[...]
    km = km_ref[...]
    lane = lax.broadcasted_iota(jnp.int32, (ts, LANES), 1)
    even_data = lane < HEAD_DIM
    nt = (((1,), (1,)), ((), ()))
    blk = lax.broadcasted_iota(jnp.int32, (nb, ts), 0).astype(F32)
    col = lax.broadcasted_iota(jnp.int32, (nb, ts), 1)
    own = ((pl.program_id(2) * ts + col) // MOBA_BLOCK).astype(F32)
    past = blk < own
    place_row = lax.broadcasted_iota(jnp.int32, (nb, LANES), 0)
    place_lane = lax.broadcasted_iota(jnp.int32, (nb, LANES), 1)

    for h, out_ref in enumerate((qe_ref, qo_ref)):
        data = even_data if h == 0 else jnp.logical_not(even_data)
        qh = jnp.where(data, qf, 0.0)
        gate = lax.dot_general(km, qh, nt, preferred_element_type=F32,
                               precision=lax.Precision.HIGHEST)
        g = jnp.where(past, gate, NEG_INF)
        sel = jnp.zeros((nb, ts), F32)
        for _ in range(min(MOBA_TOPK, nb)):
            m = jnp.max(g, axis=0, keepdims=True)
            first = jnp.min(jnp.where(g == m, blk, float(nb)), axis=0, keepdims=True)
            pick = blk == first
            sel = jnp.where(pick, 1.0, sel)
            g = jnp.where(pick, NEG_INF, g)
        sel = jnp.where(past, sel, 0.0)
        sel = jnp.where(blk == own, 1.0, sel)
        spare_base = HEAD_DIM if h == 0 else 0
        place = (place_lane == place_row + spare_base).astype(BF16)
        placed = lax.dot_general(sel.astype(BF16), place, (((0,), (0,)), ((), ())),
                                 preferred_element_type=F32)
        bias = (1.0 - placed) * MASK_BIAS
        out_ref[...] = jnp.where(data, qh * (HEAD_DIM ** -0.5 * LOG2_E), bias).astype(out_ref.dtype)


def _moba_select(qb, kmean, batch, ts=4096):
    pairs, t, _ = qb.shape
    s = t // batch
    nb = s // MOBA_BLOCK
    ts = min(ts, s)
    assert nb <= HEAD_DIM and ts % MOBA_BLOCK == 0
    km = kmean.reshape(batch, nb, pairs * LANES)
    qblk = lambda b, h, i: (h, b * (s // ts) + i, 0)
    return pl.pallas_call(
        _moba_select_kernel,
        grid=(batch, pairs, s // ts),
        in_specs=[pl.BlockSpec((None, ts, LANES), qblk),
                  pl.BlockSpec((None, nb, LANES), lambda b, h, i: (b, 0, h))],
        out_specs=[pl.BlockSpec((None, ts, LANES), qblk)] * 2,
        out_shape=[jax.ShapeDtypeStruct((pairs, t, LANES), BF16)] * 2,
        compiler_params=_params("parallel", "parallel", "parallel"),
        name="moba_select",
    )(qb, km)


def _moba_kernel(qe_ref, qo_ref, ke_ref, ko_ref, ve_ref, vo_ref, o_ref, s_ref, m_ref, acc_ref):
    qi = pl.program_id(2)
    tq = qe_ref.shape[0]
    slab = MOBA_GROUP * MOBA_BLOCK
    nt = (((1,), (1,)), ((), ()))
    last = qi // MOBA_GROUP
    heads = ((qe_ref, ke_ref, ve_ref), (qo_ref, ko_ref, vo_ref))
    r = lax.broadcasted_iota(jnp.int32, (tq, slab), 0)
    c = lax.broadcasted_iota(jnp.int32, (tq, slab), 1)
    causal = c - r <= (qi - last * MOBA_GROUP) * MOBA_BLOCK

    def scores(groups, masked):
        for h, (q_ref, k_ref, _) in enumerate(heads):
            mx = None if masked else m_ref[h]
            for t in groups:
                off = t * slab
                sb = lax.dot_general(q_ref[...], k_ref[off:off + slab, :], nt,
                                     preferred_element_type=F32)
                if masked:
                    sb = jnp.where(causal, sb, NEG_INF)
                s_ref[h, t] = sb
                for u in range(slab // LANES):
                    part = sb[:, u * LANES:(u + 1) * LANES]
                    mx = part if mx is None else jnp.maximum(mx, part)
            m_ref[h] = mx

    def weighted_values(groups, first):
        for h, (_, _, v_ref) in enumerate(heads):
            acc = None if first else acc_ref[h]
            for t in groups:
                off = t * slab
                p = jnp.exp2(s_ref[h, t] - row_max[h]).astype(BF16)
                pv = jnp.dot(p, v_ref[off:off + slab, :], preferred_element_type=F32)
                acc = pv if acc is None else acc + pv
            acc_ref[h] = acc

    row_max = [None, None]
    for n_past in range(s_ref.shape[1]):
        @pl.when(last == n_past)
        def _(n_past=n_past):
            scores((n_past,), True)
            if n_past:
                scores(tuple(range(n_past)), False)
            for h in range(2):
                row_max[h] = jnp.max(m_ref[h], axis=-1, keepdims=True)
            weighted_values(tuple(range(n_past + 1)), True)

    lane = lax.broadcasted_iota(jnp.int32, (tq, LANES), 1)
    a0 = acc_ref[0]
    a1 = acc_ref[1]
    o0 = a0 / pltpu.roll(a0, HEAD_DIM, 1)
    o1 = a1 / pltpu.roll(a1, HEAD_DIM, 1)
    o_ref[...] = jnp.where(lane < HEAD_DIM, o0, o1).astype(o_ref.dtype)


def _moba(qbe, qbo, kbe, kbo, vbe, vbo, batch):
    pairs, t, _ = qbe.shape
    s = t // batch
    tq = MOBA_BLOCK
    nq = s // tq
    nb = s // MOBA_BLOCK
    assert nb % MOBA_GROUP == 0
    qblk = lambda b, h, i: (h, b * nq + i, 0)
    kvblk = lambda b, h, i: (h, b, 0)
    return pl.pallas_call(
        _moba_kernel,
        grid=(batch, pairs, nq),
        in_specs=[pl.BlockSpec((None, tq, LANES), qblk)] * 2
        + [pl.BlockSpec((None, s, LANES), kvblk)] * 4,
        out_specs=pl.BlockSpec((None, tq, LANES), qblk),
        out_shape=jax.ShapeDtypeStruct((pairs, t, LANES), BF16),
        scratch_shapes=[pltpu.VMEM((2, nb // MOBA_GROUP, tq, MOBA_GROUP * MOBA_BLOCK), F32),
                        pltpu.VMEM((2, tq, LANES), F32),
                        pltpu.VMEM((2, tq, LANES), F32)],
        compiler_params=_params("parallel", "parallel", "arbitrary"),
        name="moba",
    )(qbe, qbo, kbe, kbo, vbe, vbo)


def _ffn_kernel(*refs, row_split, th, with_mixer):
    if with_mixer:
        x_ref, ya_ref, yb_ref, wo_ref, g0_ref, b0_ref, w1_ref, w2_ref, g_ref, b_ref, o_ref = refs
    else:
        x_ref, w1_ref, w2_ref, g_ref, b_ref, o_ref = refs
    tm = x_ref.shape[0]
    hdim = w1_ref.shape[1]
    rows = tm // row_split
    for part in range(row_split):
        sl = slice(part * rows, (part + 1) * rows)
        x = x_ref[sl, :]
        if with_mixer:
            ka = ya_ref.shape[1]
            yb = jnp.concatenate([yb_ref[p, sl, :] for p in range(yb_ref.shape[0])], axis=1)
            y0 = (jnp.dot(ya_ref[sl, :], wo_ref[:ka, :], preferred_element_type=F32)
                  + jnp.dot(yb, wo_ref[ka:, :], preferred_element_type=F32))
            x = _layer_norm_rows(DEEPNORM_ALPHA * x + y0, g0_ref[...], b0_ref[...])
        xb = x.astype(BF16)
        y = None
        for c in range(hdim // th):
            h = jnp.dot(xb, w1_ref[:, c * th:(c + 1) * th], preferred_element_type=F32)
            h = jnp.square(jnp.maximum(h, 0.0)).astype(BF16)
            yc = jnp.dot(h, w2_ref[c * th:(c + 1) * th, :], preferred_element_type=F32)
            y = yc if y is None else y + yc
        z = DEEPNORM_ALPHA * x + y
        o_ref[sl, :] = _layer_norm_rows(z, g_ref[...], b_ref[...])


def _ffn_ln(x2d, w1_stack_bf16, w2_stack_bf16, layer, gain, bias, mixer=None,
            tm=1024, th=1024, row_split=2):
    t, d = x2d.shape
    hdim = w1_stack_bf16.shape[2]
    row = lambda i: (i, 0)
    const = lambda i: (0, 0)
    resident = pl.Buffered(1)
    vec = pl.BlockSpec((1, d), const)
    operands = [x2d]
    in_specs = [pl.BlockSpec((tm, d), row)]
    if mixer is not None:
        ya, yb, w_out_stack_bf16, mixer_layer, gain0, bias0 = mixer
        operands += [ya, yb, w_out_stack_bf16, gain0.reshape(1, d), bias0.reshape(1, d)]
        in_specs += [pl.BlockSpec((tm, ya.shape[1]), row),
                     pl.BlockSpec((yb.shape[0], tm, LANES), lambda i: (0, i, 0)),
                     pl.BlockSpec((None,) + w_out_stack_bf16.shape[1:], lambda i: (mixer_layer, 0, 0),
                                  pipeline_mode=resident), vec, vec]
    operands += [w1_stack_bf16, w2_stack_bf16, gain.reshape(1, d), bias.reshape(1, d)]
    in_specs += [pl.BlockSpec((None, d, hdim), lambda i: (layer, 0, 0), pipeline_mode=resident),
                 pl.BlockSpec((None, hdim, d), lambda i: (layer, 0, 0), pipeline_mode=resident),
                 vec, vec]
    return pl.pallas_call(
        functools.partial(_ffn_kernel, row_split=row_split, th=th, with_mixer=mixer is not None),
        grid=(t // tm,),
        in_specs=in_specs,
        out_specs=pl.BlockSpec((tm, d), row),
        out_shape=jax.ShapeDtypeStruct((t, d), F32),
        compiler_params=_params("parallel"),
        name="ffn_ln",
    )(*operands)


def _gmlp_kernel(x_ref, w_ref, b_ref, lng_ref, lnb_ref, ws_ref, bst_ref, wo_ref, g_ref, beta_ref,
                 o_ref, v_ref, gated_ref, *, row_split):
    tm = x_ref.shape[0]
    gw = v_ref.shape[1]
    c = GMLP_CHUNK
    cw = gw // GMLP_GROUPS
    part_rows = tm // row_split
    r = lax.broadcasted_iota(jnp.int32, (c, c), 0)
    cc = lax.broadcasted_iota(jnp.int32, (c, c), 1)
    causal = cc <= r
    bst = bst_ref[...]
    ws = [jnp.where(causal, ws_ref[g], 0.0).astype(BF16) for g in range(GMLP_GROUPS)]
    for part in range(row_split):
        base = part * part_rows
        prow = slice(base, base + part_rows)
        x = x_ref[prow, :]
        xb = x.astype(BF16)
        v = jnp.dot(xb, w_ref[:, gw:], preferred_element_type=F32) + b_ref[:, gw:]
        v_ref[prow, :] = _layer_norm_rows(jax.nn.gelu(v), lng_ref[...], lnb_ref[...]).astype(BF16)
        for g in range(GMLP_GROUPS):
            cols = slice(g * cw, (g + 1) * cw)
            u = jax.nn.gelu(jnp.dot(xb, w_ref[:, cols], preferred_element_type=F32) + b_ref[:, cols])
            for ci in range(part_rows // c):
                rows = slice(base + ci * c, base + (ci + 1) * c)
                sv = jnp.dot(ws[g], v_ref[rows, cols], preferred_element_type=F32) + bst[:, g:g + 1]
                gated_ref[rows, cols] = (u[ci * c:(ci + 1) * c] * sv).astype(BF16)
        y = jnp.dot(gated_ref[prow, :], wo_ref[...], preferred_element_type=F32)
        z = DEEPNORM_ALPHA * x + y
        o_ref[prow, :] = _layer_norm_rows(z, g_ref[...], beta_ref[...])


def _gmlp_mixer_ln(x2d, w_uv_stack_bf16, w_out_stack_bf16, layer, b_uv, ln_g, ln_b, w_s, b_s,
                   gain, bias, tm=1024, row_split=2):
    t, d = x2d.shape
    n = w_uv_stack_bf16.shape[2]
    gw = n // 2
    row = lambda i: (i, 0)
    const = lambda i: (0, 0)
    resident = pl.Buffered(1)
    return pl.pallas_call(
        functools.partial(_gmlp_kernel, row_split=row_split),
        grid=(t // tm,),
        in_specs=[pl.BlockSpec((tm, d), row),
                  pl.BlockSpec((None, d, n), lambda i: (layer, 0, 0), pipeline_mode=resident),
                  pl.BlockSpec((1, n), const),
                  pl.BlockSpec((1, gw), const),
                  pl.BlockSpec((1, gw), const),
                  pl.BlockSpec(w_s.shape, lambda i: (0, 0, 0)),
                  pl.BlockSpec((GMLP_CHUNK, GMLP_GROUPS), const),
                  pl.BlockSpec((None, gw, d), lambda i: (layer, 0, 0), pipeline_mode=resident),
                  pl.BlockSpec((1, d), const),
                  pl.BlockSpec((1, d), const)],
        out_specs=pl.BlockSpec((tm, d), row),
        out_shape=jax.ShapeDtypeStruct((t, d), F32),
        scratch_shapes=[pltpu.VMEM((tm, gw), BF16), pltpu.VMEM((tm, gw), BF16)],
        compiler_params=_params("parallel"),
        name="gmlp_mixer_ln",
    )(x2d, w_uv_stack_bf16, b_uv.reshape(1, n), ln_g.reshape(1, gw), ln_b.reshape(1, gw), w_s, b_s.T,
      w_out_stack_bf16, gain.reshape(1, d), bias.reshape(1, d))


def kernel(x, positions, ln_gain, ln_bias, mix_w_in, ret_gn_gain, mix_w_out,
           gmlp_w_uv, gmlp_b_uv, gmlp_ln_gain, gmlp_ln_bias, gmlp_w_s, gmlp_b_s,
           gmlp_w_out, ffn_w_in, ffn_w_out):
    batch, seq, d = x.shape
    depth = ln_gain.shape[0]
    h = x.reshape(batch * seq, d)
    cos_t, sin_t = _rope_tables(positions)
    log_gamma = jnp.log1p(-jnp.exp2(-5.0 - jnp.arange(N_RET_HEADS, dtype=F32)))
    mix_w_in_b, mix_w_out_b = mix_w_in.astype(BF16), mix_w_out.astype(BF16)
    gmlp_w_uv_b, gmlp_w_out_b = gmlp_w_uv.astype(BF16), gmlp_w_out.astype(BF16)
    ffn_w_in_b, ffn_w_out_b = ffn_w_in.astype(BF16), ffn_w_out.astype(BF16)
    for layer in range(depth):
        i = layer // 2
        if layer % 2 == 0:
            qa, ka, va, ga, qb, kbe, kbo, vbe, vbo, kmean = _in_proj(
                h, mix_w_in_b, i, cos_t, sin_t, seq // MOBA_BLOCK)
            ya = _retention(qa, ka, va, ga, ret_gn_gain[i], log_gamma, batch)
            qbe, qbo = _moba_select(qb, kmean, batch)
            yb = _moba(qbe, qbo, kbe, kbo, vbe, vbo, batch)
            mixer = (ya, yb, mix_w_out_b, i, ln_gain[layer, 0], ln_bias[layer, 0])
        else:
            h = _gmlp_mixer_ln(h, gmlp_w_uv_b, gmlp_w_out_b, i, gmlp_b_uv[i], gmlp_ln_gain[i],
                               gmlp_ln_bias[i], gmlp_w_s[i], gmlp_b_s[i],
                               ln_gain[layer, 0], ln_bias[layer, 0])
            mixer = None
        h = _ffn_ln(h, ffn_w_in_b, ffn_w_out_b, layer, ln_gain[layer, 1], ln_bias[layer, 1], mixer)
    return h.reshape(batch, seq, d)
```
